```python
import math
import jax, jax.numpy as jnp
from jax import lax
import numpy as np

D_MODEL = 2048
BATCH = 2
SEQ = 4096
DEPTH = 4
DEC_BATCH = 8
DEC_SEQ = 8
PAST_LEN = 16384
PAGE_SIZE = 128

N_MIXERS = 3
N_A_LAYERS = (DEPTH + 2) // 3
N_B_LAYERS = (DEPTH + 1) // 3
N_C_LAYERS = DEPTH // 3

A_CHUNK = 128
A_WIDTH = D_MODEL
A_GROUPS = 8
A_GROUP_DIM = A_WIDTH // A_GROUPS
B_HEAD_DIM = 64
B_HEADS = D_MODEL // (2 * B_HEAD_DIM)
B_V_DIM = 2 * B_HEAD_DIM
B_QK_WIDTH = B_HEADS * 2 * B_HEAD_DIM
B_V_WIDTH = B_HEADS * B_V_DIM
B_Q_BLOCK = 128
C_EXPAND = 128
C_HEADS = D_MODEL // C_EXPAND
C_K_DIM = C_EXPAND
C_V_DIM = D_MODEL // C_HEADS
C_CHUNK = 64
FFN_HIDDEN = -(-8 * D_MODEL // (3 * 256)) * 256
EPS = 1e-6

kernel_name = 'hybrid_gmlp_diffattn_hgrn2_decode_step'


def rmsnorm(x, gain):
    xf = x.astype(jnp.float32)
    y = xf * lax.rsqrt(jnp.mean(xf * xf, axis=-1, keepdims=True) + EPS)
    return (y * gain.astype(jnp.float32)).astype(x.dtype)


def swiglu_ffn(x, w_gu, w_down):
    gate, up = jnp.split(x @ w_gu, 2, axis=-1)
    return (jax.nn.silu(gate) * up) @ w_down


def chunk_mlp_mixer(h, w_in, v_gain, w_s, b_s, w_out):
    bsz, L, _ = h.shape
    u, v = jnp.split(jax.nn.gelu(h @ w_in, approximate=False), 2, axis=-1)
    v = rmsnorm(v, v_gain)
    n_chunks = -(-L // A_CHUNK)
    pad = n_chunks * A_CHUNK - L
    vc = jnp.pad(v, ((0, 0), (0, pad), (0, 0))).reshape(bsz, n_chunks, A_CHUNK, A_GROUPS, A_GROUP_DIM)
    causal = jnp.tril(jnp.ones((A_CHUNK, A_CHUNK), dtype=bool))
    w_masked = jnp.where(causal[None], w_s, jnp.zeros((), w_s.dtype))
    s = jnp.einsum('gts,bnsgc->bntgc', w_masked, vc) + b_s.T[:, :, None]
    s = s.reshape(bsz, n_chunks * A_CHUNK, A_WIDTH)[:, :L]
    return (u * s) @ w_out, v


def diff_attn_qkv(h, w_in, q_gain, k_gain):
    bsz, L, _ = h.shape
    proj = h @ w_in
    q = proj[..., :B_QK_WIDTH].reshape(bsz, L, B_HEADS, 2, B_HEAD_DIM)
    k = proj[..., B_QK_WIDTH:2 * B_QK_WIDTH].reshape(bsz, L, B_HEADS, 2, B_HEAD_DIM)
    v = proj[..., 2 * B_QK_WIDTH:].reshape(bsz, L, B_HEADS, B_V_DIM)
    return rmsnorm(q, q_gain), rmsnorm(k, k_gain), v


def diff_attention_core(q, k, v, q_start, lam, lam_init, subln_gain):
    bsz, Lq = q.shape[:2]
    Lk = k.shape[1]
    qb = math.gcd(Lq, B_Q_BLOCK)
    n_blk = Lq // qb
    q_blocks = q.reshape(bsz, n_blk, qb, B_HEADS, 2, B_HEAD_DIM).swapaxes(0, 1)
    k_pos = jnp.arange(Lk)
    scale = B_HEAD_DIM ** -0.5

    def one_block(args):
        q_blk, blk = args
        q_pos = q_start + blk * qb + jnp.arange(qb)
        s = jnp.einsum('bqhcd,bkhcd->bhcqk', q_blk, k).astype(jnp.float32) * scale
        s = jnp.where(k_pos[None, :] <= q_pos[:, None], s, -jnp.inf)
        p = jax.nn.softmax(s, axis=-1)
        a = p[:, :, 0] - lam * p[:, :, 1]
        return jnp.einsum('bhqk,bkhd->bqhd', a.astype(v.dtype), v)

    o = lax.map(one_block, (q_blocks, jnp.arange(n_blk)))
    o = o.swapaxes(0, 1).reshape(bsz, Lq, B_HEADS, B_V_DIM)
    o = rmsnorm(o, subln_gain) * (1.0 - lam_init)
    return o.reshape(bsz, Lq, B_V_WIDTH)


def gated_linear_recurrence(q, k, v, log_f, s0):
    bsz, L, H, K = q.shape
    c = math.gcd(L, C_CHUNK)
    n = L // c

    def to_chunks(t):
        return t.reshape(bsz, n, c, *t.shape[2:]).swapaxes(0, 1)

    causal = jnp.tril(jnp.ones((c, c), dtype=bool))[None, :, :, None, None]

    def step(S, inp):
        qc, kc, vc, lfc = inp
        G = jnp.cumsum(lfc, axis=1)
        o_inter = jnp.einsum('bthk,bhkv->bthv', qc * jnp.exp(G), S)
        decay = jnp.exp(jnp.where(causal, G[:, :, None] - G[:, None, :], -jnp.inf))
        A = jnp.einsum('bthk,bshk,btshk->btsh', qc, kc, decay)
        o = o_inter + jnp.einsum('btsh,bshv->bthv', A, vc)
        G_last = G[:, -1]
        S_new = jnp.exp(G_last)[..., None] * S + jnp.einsum(
            'bshk,bshv->bhkv', kc * jnp.exp(G_last[:, None] - G), vc)
        return S_new, o

    S, o = lax.scan(step, s0, (to_chunks(q), to_chunks(k), to_chunks(v), to_chunks(log_f)))
    return o.swapaxes(0, 1).reshape(bsz, L, H, v.shape[-1]), S


def hgrn2_mixer(h, w_in, g_gain, lb, w_out, state0):
    bsz, L, _ = h.shape
    proj = h @ w_in
    hk = C_HEADS * C_K_DIM
    q = jax.nn.silu(proj[..., :hk].astype(jnp.float32)).reshape(bsz, L, C_HEADS, C_K_DIM)
    f_raw = proj[..., hk:2 * hk].astype(jnp.float32)
    f = lb + (1.0 - lb) * jax.nn.sigmoid(f_raw)
    k = ((1.0 - lb) * jax.nn.sigmoid(-f_raw)).reshape(bsz, L, C_HEADS, C_K_DIM)
    log_f = jnp.log(f).reshape(bsz, L, C_HEADS, C_K_DIM)
    v = proj[..., 2 * hk:2 * hk + C_HEADS * C_V_DIM].astype(jnp.float32).reshape(bsz, L, C_HEADS, C_V_DIM)
    g = proj[..., 2 * hk + C_HEADS * C_V_DIM:].astype(jnp.float32).reshape(bsz, L, C_HEADS, C_V_DIM)
    o, state = gated_linear_recurrence(q, k, v, log_f, state0.astype(jnp.float32))
    o = rmsnorm(o, g_gain) * jax.nn.silu(g)
    y = o.reshape(bsz, L, C_HEADS * C_V_DIM).astype(h.dtype) @ w_out
    return y, state.astype(h.dtype)


def setup_inputs(seed: int = 0) -> dict:
    key = jax.random.key(seed)
    ks = list(jax.random.split(key, 40))

    def nrm(shape, scale):
        return jax.random.normal(ks.pop(), shape, jnp.float32) * scale

    def gain(shape):
        return 1.0 + nrm(shape, 0.02)

    n_pages = PAST_LEN // PAGE_SIZE
    n_used = DEC_BATCH * n_pages
    n_phys = n_used + (n_used + 3) // 4
    x_prompt = nrm((BATCH, SEQ, D_MODEL), 1.0)
    x_sample = nrm((DEC_BATCH, DEC_SEQ, D_MODEL), 1.0)
    cache_k = nrm((N_B_LAYERS, n_phys, PAGE_SIZE, B_HEADS, 2 * B_HEAD_DIM), 1.0)
    cache_v = nrm((N_B_LAYERS, n_phys, PAGE_SIZE, B_HEADS, B_V_DIM), 1.0)
    perm = jax.random.permutation(ks.pop(), n_phys)
    page_table = perm[:n_used].reshape(DEC_BATCH, n_pages).astype(jnp.int32)
    state_hgrn = nrm((N_C_LAYERS, DEC_BATCH, C_HEADS, C_K_DIM, C_V_DIM), 0.5)
    c_in_width = 2 * C_HEADS * C_K_DIM + 2 * C_HEADS * C_V_DIM
    return {
        'x_prompt': x_prompt,
        'x_sample': x_sample,
        'cache_k': cache_k,
        'cache_v': cache_v,
        'page_table': page_table,
        'state_hgrn': state_hgrn,
        'norm_mix': gain((DEPTH, D_MODEL)),
        'norm_ffn': gain((DEPTH, D_MODEL)),
        'ffn_w_gu': nrm((DEPTH, D_MODEL, 2 * FFN_HIDDEN), D_MODEL ** -0.5),
        'ffn_w_down': nrm((DEPTH, FFN_HIDDEN, D_MODEL), FFN_HIDDEN ** -0.5),
        'a_w_in': nrm((N_A_LAYERS, D_MODEL, 2 * A_WIDTH), D_MODEL ** -0.5),
        'a_v_norm': gain((N_A_LAYERS, A_WIDTH)),
        'a_w_s': nrm((N_A_LAYERS, A_GROUPS, A_CHUNK, A_CHUNK), A_CHUNK ** -0.5),
        'a_b_s': 1.0 + nrm((N_A_LAYERS, A_GROUPS, A_CHUNK), 0.1),
        'a_w_out': nrm((N_A_LAYERS, A_WIDTH, D_MODEL), A_WIDTH ** -0.5),
        'b_w_in': nrm((N_B_LAYERS, D_MODEL, 2 * B_QK_WIDTH + B_V_WIDTH), D_MODEL ** -0.5),
        'b_q_norm': gain((N_B_LAYERS, B_HEAD_DIM)),
        'b_k_norm': gain((N_B_LAYERS, B_HEAD_DIM)),
        'b_lambda_q1': nrm((N_B_LAYERS, B_HEAD_DIM), 0.1),
        'b_lambda_k1': nrm((N_B_LAYERS, B_HEAD_DIM), 0.1),
        'b_lambda_q2': nrm((N_B_LAYERS, B_HEAD_DIM), 0.1),
        'b_lambda_k2': nrm((N_B_LAYERS, B_HEAD_DIM), 0.1),
        'b_subln': gain((N_B_LAYERS, B_V_DIM)),
        'b_w_out': nrm((N_B_LAYERS, B_V_WIDTH, D_MODEL), B_V_WIDTH ** -0.5),
        'c_w_in': nrm((N_C_LAYERS, D_MODEL, c_in_width), D_MODEL ** -0.5),
        'c_g_norm': gain((N_C_LAYERS, C_V_DIM)),
        'c_lower_bounds': nrm((DEPTH, C_HEADS * C_K_DIM), 0.1),
        'c_w_out': nrm((N_C_LAYERS, C_HEADS * C_V_DIM, D_MODEL), (C_HEADS * C_V_DIM) ** -0.5),
    }


def reference(x_prompt, x_sample, cache_k, cache_v, page_table, state_hgrn,
              norm_mix, norm_ffn, ffn_w_gu, ffn_w_down,
              a_w_in, a_v_norm, a_w_s, a_b_s, a_w_out,
              b_w_in, b_q_norm, b_k_norm, b_lambda_q1, b_lambda_k1, b_lambda_q2, b_lambda_k2,
              b_subln, b_w_out,
              c_w_in, c_g_norm, c_lower_bounds, c_w_out):
    dec_b = x_sample.shape[0]
    past_len = page_table.shape[1] * PAGE_SIZE
    probs = jax.nn.softmax(c_lower_bounds.astype(jnp.float32), axis=0)
    lower_bound = jnp.cumsum(probs, axis=0) - probs[0]
    h_p, h_s = x_prompt, x_sample
    k_p_rows, v_p_rows, k_s_rows, v_s_rows = [], [], [], []
    hgrn_p, hgrn_s, chunk_v_s = [], [], []
    for i in range(DEPTH):
        kind, j = i % N_MIXERS, i // N_MIXERS
        a_p = rmsnorm(h_p, norm_mix[i])
        a_s = rmsnorm(h_s, norm_mix[i])
        if kind == 0:
            y_p, _ = chunk_mlp_mixer(a_p, a_w_in[j], a_v_norm[j], a_w_s[j], a_b_s[j], a_w_out[j])
            y_s, v_rows = chunk_mlp_mixer(a_s, a_w_in[j], a_v_norm[j], a_w_s[j], a_b_s[j], a_w_out[j])
            chunk_v_s.append(v_rows)
        elif kind == 1:
            lam_init = 0.8 - 0.6 * math.exp(-0.3 * i)
            lam = (jnp.exp(jnp.sum(b_lambda_q1[j].astype(jnp.float32) * b_lambda_k1[j].astype(jnp.float32)))
                   - jnp.exp(jnp.sum(b_lambda_q2[j].astype(jnp.float32) * b_lambda_k2[j].astype(jnp.float32)))
                   + lam_init)
            q, k, v = diff_attn_qkv(a_p, b_w_in[j], b_q_norm[j], b_k_norm[j])
            y_p = diff_attention_core(q, k, v, 0, lam, lam_init, b_subln[j]) @ b_w_out[j]
            k_p_rows.append(k.reshape(k.shape[0], k.shape[1], B_HEADS, 2 * B_HEAD_DIM))
            v_p_rows.append(v)
            q, k, v = diff_attn_qkv(a_s, b_w_in[j], b_q_norm[j], b_k_norm[j])
            k_past = cache_k[j, page_table].reshape(dec_b, past_len, B_HEADS, 2, B_HEAD_DIM)
            v_past = cache_v[j, page_table].reshape(dec_b, past_len, B_HEADS, B_V_DIM)
            k_all = jnp.concatenate([k_past, k.astype(k_past.dtype)], axis=1)
            v_all = jnp.concatenate([v_past, v.astype(v_past.dtype)], axis=1)
            y_s = diff_attention_core(q, k_all, v_all, past_len, lam, lam_init, b_subln[j]) @ b_w_out[j]
            k_s_rows.append(k.reshape(dec_b, k.shape[1], B_HEADS, 2 * B_HEAD_DIM))
            v_s_rows.append(v)
        else:
            zero_state = jnp.zeros((x_prompt.shape[0], C_HEADS, C_K_DIM, C_V_DIM), jnp.float32)
            y_p, st_p = hgrn2_mixer(a_p, c_w_in[j], c_g_norm[j], lower_bound[i], c_w_out[j], zero_state)
            y_s, st_s = hgrn2_mixer(a_s, c_w_in[j], c_g_norm[j], lower_bound[i], c_w_out[j], state_hgrn[j])
            hgrn_p.append(st_p)
            hgrn_s.append(st_s)
        h_p = h_p + y_p.astype(h_p.dtype)
        h_s = h_s + y_s.astype(h_s.dtype)
        h_p = h_p + swiglu_ffn(rmsnorm(h_p, norm_ffn[i]), ffn_w_gu[i], ffn_w_down[i])
        h_s = h_s + swiglu_ffn(rmsnorm(h_s, norm_ffn[i]), ffn_w_gu[i], ffn_w_down[i])
    return (h_p, h_s, jnp.stack(k_p_rows), jnp.stack(v_p_rows), jnp.stack(k_s_rows), jnp.stack(v_s_rows),
            jnp.stack(hgrn_p), jnp.stack(hgrn_s), jnp.stack(chunk_v_s))
```

```python
import functools
import math

import jax
import jax.numpy as jnp
import numpy as np
from jax import lax
from jax.experimental import pallas as pl
from jax.experimental.pallas import tpu as pltpu

F32 = jnp.float32
BF16 = jnp.bfloat16
EPS = 1e-6

LANES = 128
SUBLANES = 8
VMEM_LIMIT_BYTES = 56 << 20

A_CHUNK = 128
A_GROUPS = 8
B_HEAD_DIM = 64
B_V_DIM = 2 * B_HEAD_DIM
C_HEAD_DIM = 128
PAGE_SIZE = 128
HGRN_SUB = SUBLANES


def _cparams(n_axes):
    return pltpu.CompilerParams(
        dimension_semantics=("arbitrary",) * n_axes,
        vmem_limit_bytes=VMEM_LIMIT_BYTES,
    )


def _rmsnorm_f32(x, gain):
    return x * lax.rsqrt(jnp.mean(x * x, axis=-1, keepdims=True) + EPS) * gain


def _sigmoid(x):
    return 1.0 / (1.0 + jnp.exp(-x))


def _norm_rows_body(x_ref, g_ref, o_ref):
    o_ref[...] = _rmsnorm_f32(x_ref[...], g_ref[...]).astype(o_ref.dtype)


def norm_rows(x, gain, *, tm):
    m, d = x.shape
    return pl.pallas_call(
        _norm_rows_body,
        grid=(m // tm,),
        in_specs=[pl.BlockSpec((tm, d), lambda i: (i, 0)),
                  pl.BlockSpec((1, d), lambda i: (0, 0))],
        out_specs=pl.BlockSpec((tm, d), lambda i: (i, 0)),
        out_shape=jax.ShapeDtypeStruct((m, d), BF16),
        compiler_params=_cparams(1),
        name="norm_rows",
    )(x, gain.reshape(1, d))


def _seg_matmul_body(*refs, ns, nv, nc, epilogue):
    x_ref = refs[0]
    w_refs = refs[1:1 + ns]
    vec_refs = refs[1 + ns:1 + ns + nv]
    const_refs = refs[1 + ns + nv:1 + ns + nv + nc]
    out_refs = refs[1 + ns + nv + nc:-1]
    wb_ref = refs[-1]

    @pl.when(pl.program_id(1) == 0)
    def _():
        for s in range(ns):
            wb_ref[s] = w_refs[s][...].astype(BF16)

    x = x_ref[...]
    accs = [jnp.dot(x, wb_ref[s], preferred_element_type=F32) for s in range(ns)]
    outs = epilogue(accs, [r[...] for r in vec_refs], [r[...] for r in const_refs])
    for r, o in zip(out_refs, outs):
        r[...] = o.astype(r.dtype)


def seg_matmul(x, w, seg_starts, seg_width, epilogue, out_dtypes, vecs=(), consts=(),
               *, tm, tn, name):
    m, k = x.shape
    ns = len(seg_starts)
    in_specs = [pl.BlockSpec((tm, k), lambda j, i: (i, 0))]
    for st in seg_starts:
        in_specs.append(pl.BlockSpec((k, tn), lambda j, i, off=st // tn: (0, off + j)))
    for _ in vecs:
        in_specs.append(pl.BlockSpec((1, tn), lambda j, i: (0, j)))
    for c in consts:
        in_specs.append(pl.BlockSpec(c.shape, lambda j, i, nd=c.ndim: (0,) * nd))
    return pl.pallas_call(
        functools.partial(_seg_matmul_body, ns=ns, nv=len(vecs), nc=len(consts), epilogue=epilogue),
        grid=(seg_width // tn, m // tm),
        in_specs=in_specs,
        out_specs=[pl.BlockSpec((tm, tn), lambda j, i: (i, j)) for _ in out_dtypes],
        out_shape=[jax.ShapeDtypeStruct((m, seg_width), dt) for dt in out_dtypes],
        scratch_shapes=[pltpu.VMEM((ns, k, tn), BF16)],
        compiler_params=_cparams(2),
        name=name,
    )(x, *([w] * ns), *vecs, *consts)


def _erf_f32(x):
    x = jnp.clip(x, -4.0, 4.0)
    x2 = x * x
    alpha = (-2.72614225801306e-10, 2.77068142495902e-08, -2.10102402082508e-06,
             -5.69250639462346e-05, -7.34990630326855e-04, -2.95459980854025e-03,
             -1.60960333262415e-02)
    beta = (-1.45660718464996e-05, -2.13374055278905e-04, -1.68282697438203e-03,
            -7.37332916720468e-03, -1.42647390514189e-02)
    p = jnp.full_like(x2, alpha[0])
    for c in alpha[1:]:
        p = p * x2 + c
    q = jnp.full_like(x2, beta[0])
    for c in beta[1:]:
        q = q * x2 + c
    return x * p / q


def _gelu_epilogue(accs, vecs, consts):
    (a,) = accs
    return [0.5 * a * (1.0 + _erf_f32(a * (2.0 ** -0.5)))]


def _headnorm_epilogue(accs, vecs, consts):
    aq, ak, av = accs
    gq, gk = vecs
    (group_ones,) = consts

    def head_norm(a, g):
        sq = a * a
        hi = sq.astype(BF16)
        lo = (sq - hi.astype(F32)).astype(BF16)
        ms = (jnp.dot(hi, group_ones, preferred_element_type=F32)
              + jnp.dot(lo, group_ones, preferred_element_type=F32)) * (1.0 / B_HEAD_DIM)
        return a * lax.rsqrt(ms + EPS) * g

    return [head_norm(aq, gq), head_norm(ak, gk), av]


def _hgrn_gate_epilogue(accs, vecs, consts):
    aq, af, av, ag = accs
    (lb,) = vecs
    q = aq * _sigmoid(aq)
    f = lb + (1.0 - lb) * _sigmoid(af)
    k = (1.0 - lb) * _sigmoid(-af)
    return [q, k, jnp.log(f), av, ag]


def _out_proj_body(y_ref, w_ref, h_ref, g_ref, ho_ref, xo_ref, wb_ref, *, cast_rows):
    @pl.when(pl.program_id(0) == 0)
    def _():
        def cast(r, carry):
            sl = pl.ds(pl.multiple_of(r * cast_rows, cast_rows), cast_rows)
            wb_ref[sl, :] = w_ref[sl, :].astype(BF16)
            return carry
        lax.fori_loop(0, w_ref.shape[0] // cast_rows, cast, 0)

    hn = h_ref[...] + jnp.dot(y_ref[...].astype(BF16), wb_ref[...], preferred_element_type=F32)
    ho_ref[...] = hn
    xo_ref[...] = _rmsnorm_f32(hn, g_ref[...]).astype(xo_ref.dtype)


def out_proj(y, w, h, gain_next, *, tm):
    m, k = y.shape
    n = w.shape[1]
    return pl.pallas_call(
        functools.partial(_out_proj_body, cast_rows=256),
        grid=(m // tm,),
        in_specs=[pl.BlockSpec((tm, k), lambda i: (i, 0)),
                  pl.BlockSpec((k, n), lambda i: (0, 0), pipeline_mode=pl.Buffered(1)),
                  pl.BlockSpec((tm, n), lambda i: (i, 0)),
                  pl.BlockSpec((1, n), lambda i: (0, 0))],
        out_specs=[pl.BlockSpec((tm, n), lambda i: (i, 0)),
                   pl.BlockSpec((tm, n), lambda i: (i, 0))],
        out_shape=[jax.ShapeDtypeStruct((m, n), F32), jax.ShapeDtypeStruct((m, n), BF16)],
        scratch_shapes=[pltpu.VMEM((k, n), BF16)],
        compiler_params=_cparams(1),
        name="out_proj",
    )(y, w, h, gain_next.reshape(1, n))


def _ffn_body(x_ref, h_ref, wg_ref, wu_ref, wd_ref, g_ref, ho_ref, *maybe_xo, n_t):
    t = pl.program_id(1)
    x = x_ref[...]
    gate = jnp.dot(x, wg_ref[...].astype(BF16), preferred_element_type=F32)
    up = jnp.dot(x, wu_ref[...].astype(BF16), preferred_element_type=F32)
    act = (gate * _sigmoid(gate) * up).astype(BF16)
    contrib = jnp.dot(act, wd_ref[...].astype(BF16), preferred_element_type=F32)

    @pl.when(t == 0)
    def _():
        ho_ref[...] = h_ref[...] + contrib

    @pl.when(t > 0)
    def _():
        ho_ref[...] += contrib

    if maybe_xo:
        @pl.when(t == n_t - 1)
        def _():
            maybe_xo[0][...] = _rmsnorm_f32(ho_ref[...], g_ref[...]).astype(BF16)


def ffn(x, h, w_gu, w_down, gain_next, *, tm, th):
    m, d = x.shape
    hidden = w_down.shape[0]
    n_t = hidden // th
    with_norm = gain_next is not None
    gain = gain_next if with_norm else jnp.ones((d,), F32)
    out_specs = [pl.BlockSpec((tm, d), lambda i, t: (i, 0))]
    out_shape = [jax.ShapeDtypeStruct((m, d), F32)]
    if with_norm:
        out_specs.append(pl.BlockSpec((tm, d), lambda i, t: (i, 0)))
        out_shape.append(jax.ShapeDtypeStruct((m, d), BF16))
    res = pl.pallas_call(
        functools.partial(_ffn_body, n_t=n_t),
        grid=(m // tm, n_t),
        in_specs=[pl.BlockSpec((tm, d), lambda i, t: (i, 0)),
                  pl.BlockSpec((tm, d), lambda i, t: (i, 0)),
                  pl.BlockSpec((d, th), lambda i, t: (0, t)),
                  pl.BlockSpec((d, th), lambda i, t, n_t=n_t: (0, n_t + t)),
                  pl.BlockSpec((th, d), lambda i, t: (t, 0)),
                  pl.BlockSpec((1, d), lambda i, t: (0, 0))],
        out_specs=out_specs,
        out_shape=out_shape,
        compiler_params=_cparams(2),
        name="ffn",
    )(x, h, w_gu, w_gu, w_down, gain.reshape(1, d))
    return (res[0], res[1]) if with_norm else (res[0], None)


def _spatial_body(u_ref, v_ref, wm_ref, bs_ref, vg_ref, p_ref, vn_ref, *, chunk, groups):
    v = v_ref[...].astype(F32)
    vn = _rmsnorm_f32(v, vg_ref[...])
    vn_ref[...] = vn.astype(vn_ref.dtype)
    vnb = vn.astype(BF16)
    rows, width = v.shape
    gw = width // groups
    for c in range(rows // chunk):
        r0 = c * chunk
        for g in range(groups):
            c0 = g * gw
            s = jnp.dot(wm_ref[g], vnb[r0:r0 + chunk, c0:c0 + gw], preferred_element_type=F32)
            s = s + bs_ref[g]
            u = u_ref[r0:r0 + chunk, c0:c0 + gw].astype(F32)
            p_ref[r0:r0 + chunk, c0:c0 + gw] = (u * s).astype(p_ref.dtype)


def spatial_mix(uv, wm, bs, v_gain, *, chunk, tm, vn_dtype):
    m, w2 = uv.shape
    width = w2 // 2
    groups = wm.shape[0]
    return pl.pallas_call(
        functools.partial(_spatial_body, chunk=chunk, groups=groups),
        grid=(m // tm,),
        in_specs=[pl.BlockSpec((tm, width), lambda i: (i, 0)),
                  pl.BlockSpec((tm, width), lambda i: (i, 1)),
                  pl.BlockSpec(wm.shape, lambda i: (0, 0, 0)),
                  pl.BlockSpec(bs.shape, lambda i: (0, 0, 0)),
                  pl.BlockSpec((1, width), lambda i: (0, 0))],
        out_specs=[pl.BlockSpec((tm, width), lambda i: (i, 0)),
                   pl.BlockSpec((tm, width), lambda i: (i, 0))],
        out_shape=[jax.ShapeDtypeStruct((m, width), BF16),
                   jax.ShapeDtypeStruct((m, width), vn_dtype)],
        compiler_params=_cparams(1),
        name="spatial_mix",
    )(uv, uv, wm, bs, v_gain.reshape(1, width))


def _diff_attn_body(qt_tab, kt_tab, q_ref, k_ref, v_ref, lam_ref, sub_ref, o_ref,
                    qs_ref, m_ref, l_ref, acc_ref, *, tq, out_scale):
    t = pl.program_id(2)
    qi = qt_tab[t]
    ki = kt_tab[t]

    @pl.when(ki == 0)
    def _():
        q = q_ref[...].astype(F32) * (B_HEAD_DIM ** -0.5)
        lane = lax.broadcasted_iota(jnp.int32, q.shape, 1)
        qs_ref[0:tq, :] = jnp.where(lane < B_HEAD_DIM, q, 0.0).astype(BF16)
        qs_ref[tq:2 * tq, :] = jnp.where(lane >= B_HEAD_DIM, q, 0.0).astype(BF16)
        m_ref[...] = jnp.full(m_ref.shape, -jnp.inf, F32)
        l_ref[...] = jnp.zeros(l_ref.shape, F32)
        acc_ref[...] = jnp.zeros(acc_ref.shape, F32)

    kb = k_ref[...].astype(BF16)
    vb = v_ref[...].astype(BF16)
    s = lax.dot_general(qs_ref[...], kb, (((1,), (1,)), ((), ())), preferred_element_type=F32)
    row = lax.broadcasted_iota(jnp.int32, s.shape, 0)
    col = lax.broadcasted_iota(jnp.int32, s.shape, 1)
    row = jnp.where(row >= tq, row - tq, row)
    s = jnp.where((ki < qi) | (col <= row), s, -jnp.inf)
    m_prev = m_ref[...]
    m_new = jnp.maximum(m_prev, jnp.max(s, axis=1, keepdims=True))
    alpha = jnp.exp(m_prev - m_new)
    p = jnp.exp(s - m_new)
    l_ref[...] = alpha * l_ref[...] + jnp.sum(p, axis=1, keepdims=True)
    acc_ref[...] = alpha * acc_ref[...] + jnp.dot(p.astype(BF16), vb, preferred_element_type=F32)
    m_ref[...] = m_new

    @pl.when(ki == qi)
    def _():
        o = acc_ref[...] / l_ref[...]
        d = o[0:tq] - lam_ref[0] * o[tq:2 * tq]
        o_ref[...] = (_rmsnorm_f32(d, sub_ref[...]) * out_scale).astype(o_ref.dtype)


def diff_attn_prompt(q, k, v, lam, subln, lam_init, *, batch, seq, heads, tq):
    nq = seq // tq
    tri = [(qi, ki) for qi in range(nq) for ki in range(qi + 1)]
    qt_tab = jnp.asarray([a for a, _ in tri], jnp.int32)
    kt_tab = jnp.asarray([b for _, b in tri], jnp.int32)
    hd = 2 * B_HEAD_DIM
    grid_spec = pltpu.PrefetchScalarGridSpec(
        num_scalar_prefetch=2,
        grid=(batch, heads, len(tri)),
        in_specs=[pl.BlockSpec((tq, hd), lambda b, h, t, qt, kt: (b * nq + qt[t], h)),
                  pl.BlockSpec((tq, hd), lambda b, h, t, qt, kt: (b * nq + kt[t], h)),
                  pl.BlockSpec((tq, B_V_DIM), lambda b, h, t, qt, kt: (b * nq + kt[t], h)),
                  pl.BlockSpec(memory_space=pltpu.SMEM),
                  pl.BlockSpec((1, B_V_DIM), lambda b, h, t, qt, kt: (0, 0))],
        out_specs=pl.BlockSpec((tq, B_V_DIM), lambda b, h, t, qt, kt: (b * nq + qt[t], h)),
        scratch_shapes=[pltpu.VMEM((2 * tq, hd), BF16),
                        pltpu.VMEM((2 * tq, 1), F32),
                        pltpu.VMEM((2 * tq, 1), F32),
                        pltpu.VMEM((2 * tq, B_V_DIM), F32)],
    )
    return pl.pallas_call(
        functools.partial(_diff_attn_body, tq=tq, out_scale=1.0 - lam_init),
        grid_spec=grid_spec,
        out_shape=jax.ShapeDtypeStruct((batch * seq, heads * B_V_DIM), BF16),
        compiler_params=_cparams(3),
        name="diff_attn_prompt",
    )(qt_tab, kt_tab, q, k, v, lam.reshape(1), subln.reshape(1, B_V_DIM))


def _decode_attn_body(pt_ref, *refs, pages, heads, n_q, n_groups, out_scale):
    k_refs = refs[:pages]
    v_refs = refs[pages:2 * pages]
    kn_ref, vn_ref, qs_ref, lam_ref, sub_ref, o_ref, s_ref, m_ref, l_ref, acc_ref = refs[2 * pages:]
    g = pl.program_id(1)
    hc = 2 * n_q

    @pl.when(g == 0)
    def _():
        m_ref[...] = jnp.full(m_ref.shape, -jnp.inf, F32)
        l_ref[...] = jnp.zeros(l_ref.shape, F32)
        acc_ref[...] = jnp.zeros(acc_ref.shape, F32)

    def head_rows(ref, h):
        return ref[pl.ds(h, PAGE_SIZE, stride=heads), :]

    def process(page_k_refs, page_v_refs, mask):
        n_p = len(page_k_refs)
        width = n_p * PAGE_SIZE
        for h in range(heads):
            qh = qs_ref[h]
            for p_i in range(n_p):
                s_ref[h * hc:(h + 1) * hc, p_i * PAGE_SIZE:(p_i + 1) * PAGE_SIZE] = lax.dot_general(
                    qh, head_rows(page_k_refs[p_i], h), (((1,), (1,)), ((), ())),
                    preferred_element_type=F32)
        s = s_ref[:, 0:width]
        if mask is not None:
            s = jnp.where(mask, s, -jnp.inf)
        m_prev = m_ref[...]
        m_new = jnp.maximum(m_prev, jnp.max(s, axis=1, keepdims=True))
        alpha = jnp.exp(m_prev - m_new)
        p = jnp.exp(s - m_new)
        l_ref[...] = alpha * l_ref[...] + jnp.sum(p, axis=1, keepdims=True)
        m_ref[...] = m_new
        for h in range(heads):
            r0 = h * hc
            pv = jnp.zeros((hc, B_V_DIM), F32)
            for p_i in range(n_p):
                pv = pv + jnp.dot(p[r0:r0 + hc, p_i * PAGE_SIZE:(p_i + 1) * PAGE_SIZE],
                                  head_rows(page_v_refs[p_i], h), preferred_element_type=F32)
            acc_ref[r0:r0 + hc, :] = acc_ref[r0:r0 + hc, :] * alpha[r0:r0 + hc] + pv

    process(k_refs, v_refs, None)

    @pl.when(g == n_groups - 1)
    def _():
        pos = lax.broadcasted_iota(jnp.int32, (heads * hc, PAGE_SIZE), 1)
        qry = lax.broadcasted_iota(jnp.int32, (heads * hc, PAGE_SIZE), 0) % n_q
        process([kn_ref], [vn_ref], pos <= qry)
        o = acc_ref[...] / l_ref[...]
        o = o.reshape(heads, 2, n_q, B_V_DIM)
        d = o[:, 0] - lam_ref[0] * o[:, 1]
        d = d * lax.rsqrt(jnp.mean(d * d, axis=-1, keepdims=True) + EPS) * sub_ref[...]
        o_ref[...] = (d * out_scale).astype(o_ref.dtype)


def diff_attn_decode(qs, cache_k, cache_v, page_table, k_new, v_new, lam, subln, lam_init,
                     *, layer, heads, n_q, pages):
    dec_b, n_pages = page_table.shape
    hd = 2 * B_HEAD_DIM
    rows = PAGE_SIZE * heads
    hc = 2 * n_q
    n_groups = n_pages // pages
    page_spec = lambda p_i: pl.BlockSpec(
        (None, None, rows, hd),
        lambda b, g, pt, p_i=p_i: (layer, pt[b, g * pages + p_i], 0, 0))
    grid_spec = pltpu.PrefetchScalarGridSpec(
        num_scalar_prefetch=1,
        grid=(dec_b, n_groups),
        in_specs=([page_spec(p_i) for p_i in range(pages)] * 2
                  + [pl.BlockSpec((None, rows, hd), lambda b, g, pt: (b, 0, 0)),
                     pl.BlockSpec((None, rows, hd), lambda b, g, pt: (b, 0, 0)),
                     pl.BlockSpec((None, heads, hc, hd), lambda b, g, pt: (b, 0, 0, 0)),
                     pl.BlockSpec(memory_space=pltpu.SMEM),
                     pl.BlockSpec((1, B_V_DIM), lambda b, g, pt: (0, 0))]),
        out_specs=pl.BlockSpec((None, heads, n_q, B_V_DIM), lambda b, g, pt: (b, 0, 0, 0)),
        scratch_shapes=[pltpu.VMEM((heads * hc, pages * PAGE_SIZE), F32),
                        pltpu.VMEM((heads * hc, 1), F32),
                        pltpu.VMEM((heads * hc, 1), F32),
                        pltpu.VMEM((heads * hc, B_V_DIM), F32)],
    )
    return pl.pallas_call(
        functools.partial(_decode_attn_body, pages=pages, heads=heads, n_q=n_q,
                          n_groups=n_groups, out_scale=1.0 - lam_init),
        grid_spec=grid_spec,
        out_shape=jax.ShapeDtypeStruct((dec_b, heads, n_q, B_V_DIM), F32),
        compiler_params=_cparams(2),
        name="diff_attn_decode",
    )(page_table, *([cache_k] * pages), *([cache_v] * pages), k_new, v_new, qs,
      lam.reshape(1), subln.reshape(1, B_V_DIM))


def _hgrn_chunk(q, k, lf, v, st, tri):
    c = q.shape[0]
    nb = c // HGRN_SUB
    if c > HGRN_SUB:
        gcum = jnp.dot(tri, lf, preferred_element_type=F32, precision=lax.Precision.HIGHEST)
    else:
        first_row = lax.broadcasted_iota(jnp.int32, (HGRN_SUB, C_HEAD_DIM), 0)
        gcum = lf[0:1] + jnp.zeros_like(lf)
        for s in range(1, HGRN_SUB):
            gcum = gcum + jnp.where(first_row >= s, lf[s:s + 1], 0.0)
    o = lax.dot_general(q * jnp.exp(gcum), st, (((1,), (1,)), ((), ())), preferred_element_type=F32)
    sub_row = lax.broadcasted_iota(jnp.int32, (HGRN_SUB, C_HEAD_DIM), 0)
    o_blocks = []
    for i in range(nb):
        r0 = i * HGRN_SUB
        gi = gcum[r0:r0 + HGRN_SUB]
        qi = q[r0:r0 + HGRN_SUB]
        ki = k[r0:r0 + HGRN_SUB]
        vi = v[r0:r0 + HGRN_SUB]
        oi = o[r0:r0 + HGRN_SUB]
        if i > 0:
            gb = gcum[r0 - 1:r0]
            qd = qi * jnp.exp(gi - gb)
            kd = k[0:r0] * jnp.exp(gb - gcum[0:r0])
            a = lax.dot_general(qd, kd, (((1,), (1,)), ((), ())), preferred_element_type=F32)
            oi = oi + jnp.dot(a, v[0:r0], preferred_element_type=F32)
        for s in range(HGRN_SUB):
            dec = jnp.exp(jnp.where(sub_row >= s, gi - gi[s:s + 1], -jnp.inf))
            a_col = jnp.sum(qi * ki[s:s + 1] * dec, axis=-1, keepdims=True)
            oi = oi + a_col * vi[s:s + 1]
        o_blocks.append(oi)
    o = jnp.concatenate(o_blocks, axis=0) if nb > 1 else o_blocks[0]
    g_last = gcum[c - 1:c]
    kd = k * jnp.exp(g_last - gcum)
    if c < C_HEAD_DIM:
        pad = jnp.zeros((C_HEAD_DIM - c, C_HEAD_DIM), F32)
        kd = jnp.concatenate([kd, pad], axis=0)
        v = jnp.concatenate([v, pad], axis=0)
    st_new = st * jnp.exp(g_last) + jnp.dot(v.T, kd, preferred_element_type=F32)
    return o, st_new


def _hgrn_body(*refs, chunk, n_chunks, n_r, with_state):
    if with_state:
        q_ref, k_ref, lf_ref, v_ref, g_ref, gg_ref, s0_ref, o_ref, so_ref, st_ref = refs
    else:
        q_ref, k_ref, lf_ref, v_ref, g_ref, gg_ref, o_ref, so_ref, st_ref = refs
    r = pl.program_id(2)

    @pl.when(r == 0)
    def _():
        if with_state:
            st_ref[...] = s0_ref[...].astype(F32).T
        else:
            st_ref[...] = jnp.zeros(st_ref.shape, F32)

    row = lax.broadcasted_iota(jnp.int32, (chunk, chunk), 0)
    col = lax.broadcasted_iota(jnp.int32, (chunk, chunk), 1)
    tri = (row >= col).astype(F32)

    def step(ci, carry):
        sl = pl.ds(pl.multiple_of(ci * chunk, chunk), chunk)
        o, st_new = _hgrn_chunk(q_ref[sl, :], k_ref[sl, :], lf_ref[sl, :], v_ref[sl, :],
                                st_ref[...], tri)
        st_ref[...] = st_new
        gate = g_ref[sl, :]
        o = _rmsnorm_f32(o, gg_ref[...]) * (gate * _sigmoid(gate))
        o_ref[sl, :] = o.astype(o_ref.dtype)
        return carry

    lax.fori_loop(0, n_chunks, step, 0)

    @pl.when(r == n_r - 1)
    def _():
        so_ref[...] = st_ref[...].T.astype(so_ref.dtype)


def hgrn_recurrence(q, k, lf, v, g, g_gain, state0, *, batch, seq, heads, rows, chunk, out_dtype):
    n_r = seq // rows
    hd = C_HEAD_DIM
    with_state = state0 is not None
    row_spec = pl.BlockSpec((rows, hd), lambda b, h, r: (b * n_r + r, h))
    in_specs = [row_spec] * 5 + [pl.BlockSpec((1, hd), lambda b, h, r: (0, 0))]
    args = [q, k, lf, v, g, g_gain.reshape(1, hd)]
    if with_state:
        in_specs.append(pl.BlockSpec((None, None, hd, hd), lambda b, h, r: (b, h, 0, 0)))
        args.append(state0)
    return pl.pallas_call(
        functools.partial(_hgrn_body, chunk=chunk, n_chunks=rows // chunk, n_r=n_r,
                          with_state=with_state),
        grid=(batch, heads, n_r),
        in_specs=in_specs,
        out_specs=[row_spec,
                   pl.BlockSpec((None, None, hd, hd), lambda b, h, r: (b, h, 0, 0))],
        out_shape=[jax.ShapeDtypeStruct((batch * seq, heads * hd), out_dtype),
                   jax.ShapeDtypeStruct((batch, heads, hd, hd), F32)],
        scratch_shapes=[pltpu.VMEM((hd, hd), F32)],
        compiler_params=_cparams(3),
        name="hgrn_recurrence",
    )(*args)


def _tiles(m):
    big = m >= 1024
    return dict(
        norm_tm=512 if big else m,
        proj_tm=1024 if big else m,
        proj_tn=256,
        out_tm=512 if big else m,
        ffn_tm=512 if big else m,
        ffn_th=256,
        spatial_tm=256 if big else m,
        attn_tq=512,
        hgrn_rows=512,
    )


def _mixer_a(xn, w_in, v_gain, w_s, b_s, *, chunk_len, n_seq, tiles, vn_dtype):
    width = w_in.shape[1] // 2
    (uv,) = seg_matmul(xn, w_in, (0,), 2 * width, _gelu_epilogue, (BF16,),
                       tm=tiles["proj_tm"], tn=2 * tiles["proj_tn"], name="a_in_proj")
    causal = jnp.tril(jnp.ones((A_CHUNK, A_CHUNK), bool))
    w_masked = jnp.where(causal[None], w_s, 0.0)
    if chunk_len == A_CHUNK:
        wm, bs, chunk = w_masked, b_s, A_CHUNK
    else:
        eye = jnp.eye(n_seq, dtype=w_s.dtype)
        small = w_masked[:, :chunk_len, :chunk_len]
        wm = jnp.einsum("ab,gts->gatbs", eye, small).reshape(
            A_GROUPS, n_seq * chunk_len, n_seq * chunk_len)
        bs = jnp.tile(b_s[:, :chunk_len], (1, n_seq))
        chunk = n_seq * chunk_len
    p, vn = spatial_mix(uv, wm.astype(BF16), bs[:, :, None], v_gain, chunk=chunk,
                        tm=max(tiles["spatial_tm"], chunk) if chunk_len == A_CHUNK else chunk,
                        vn_dtype=vn_dtype)
    return p, vn


def _mixer_b_proj(xn, w_in, q_gain, k_gain, *, heads, tiles):
    width = heads * 2 * B_HEAD_DIM
    tn = tiles["proj_tn"]
    lane_group = np.arange(tn) // B_HEAD_DIM
    group_ones = jnp.asarray(lane_group[:, None] == lane_group[None, :], BF16)
    reps = width // B_HEAD_DIM
    gq = jnp.tile(q_gain.astype(F32), reps).reshape(1, width)
    gk = jnp.tile(k_gain.astype(F32), reps).reshape(1, width)
    return seg_matmul(xn, w_in, (0, width, 2 * width), width, _headnorm_epilogue,
                      (F32, F32, F32), vecs=(gq, gk), consts=(group_ones,),
                      tm=tiles["proj_tm"], tn=tn, name="b_in_proj")


def _mixer_c_proj(xn, w_in, lower_bound, *, tiles):
    width = w_in.shape[1] // 4
    return seg_matmul(xn, w_in, (0, width, 2 * width, 3 * width), width, _hgrn_gate_epilogue,
                      (F32,) * 5, vecs=(lower_bound.reshape(1, width),),
                      tm=tiles["proj_tm"], tn=tiles["proj_tn"], name="c_in_proj")


def kernel(x_prompt, x_sample, cache_k, cache_v, page_table, state_hgrn, norm_mix, norm_ffn, ffn_w_gu, ffn_w_down, a_w_in, a_v_norm, a_w_s, a_b_s, a_w_out, b_w_in, b_q_norm, b_k_norm, b_lambda_q1, b_lambda_k1, b_lambda_q2, b_lambda_k2, b_subln, b_w_out, c_w_in, c_g_norm, c_lower_bounds, c_w_out):
    batch, seq, d_model = x_prompt.shape
    dec_b, dec_seq, _ = x_sample.shape
    depth = norm_mix.shape[0]
    b_heads = d_model // (2 * B_HEAD_DIM)
    c_heads = d_model // C_HEAD_DIM
    mp, ms = batch * seq, dec_b * dec_seq
    tp, ts = _tiles(mp), _tiles(ms)

    probs = jax.nn.softmax(c_lower_bounds.astype(F32), axis=0)
    lower_bound = jnp.cumsum(probs, axis=0) - probs[0]

    h_p = x_prompt.reshape(mp, d_model)
    h_s = x_sample.reshape(ms, d_model)
    xn_p = norm_rows(h_p, norm_mix[0], tm=tp["norm_tm"])
    xn_s = norm_rows(h_s, norm_mix[0], tm=ts["norm_tm"])

    n_phys = cache_k.shape[1]
    cache_k2 = cache_k.reshape(cache_k.shape[0], n_phys, PAGE_SIZE * b_heads, 2 * B_HEAD_DIM)
    cache_v2 = cache_v.reshape(cache_v.shape[0], n_phys, PAGE_SIZE * b_heads, B_V_DIM)

    k_p_rows, v_p_rows, k_s_rows, v_s_rows = [], [], [], []
    hgrn_p, hgrn_s, chunk_v_s = [], [], []
    for i in range(depth):
        kind, j = i % 3, i // 3
        if kind == 0:
            y_p, _ = _mixer_a(xn_p, a_w_in[j], a_v_norm[j], a_w_s[j], a_b_s[j],
                              chunk_len=A_CHUNK, n_seq=batch, tiles=tp, vn_dtype=BF16)
            y_s, vn_s = _mixer_a(xn_s, a_w_in[j], a_v_norm[j], a_w_s[j], a_b_s[j],
                                 chunk_len=dec_seq, n_seq=dec_b, tiles=ts, vn_dtype=F32)
            chunk_v_s.append(vn_s.reshape(dec_b, dec_seq, -1))
            w_out = a_w_out[j]
        elif kind == 1:
            lam_init = 0.8 - 0.6 * math.exp(-0.3 * i)
            lam = (jnp.exp(jnp.sum(b_lambda_q1[j].astype(F32) * b_lambda_k1[j].astype(F32)))
                   - jnp.exp(jnp.sum(b_lambda_q2[j].astype(F32) * b_lambda_k2[j].astype(F32)))
                   + lam_init)
            q_p, k_p, v_p = _mixer_b_proj(xn_p, b_w_in[j], b_q_norm[j], b_k_norm[j],
                                          heads=b_heads, tiles=tp)
            y_p = diff_attn_prompt(q_p, k_p, v_p, lam, b_subln[j], lam_init,
                                   batch=batch, seq=seq, heads=b_heads, tq=tp["attn_tq"])
            k_p_rows.append(k_p.reshape(batch, seq, b_heads, 2 * B_HEAD_DIM))
            v_p_rows.append(v_p.reshape(batch, seq, b_heads, B_V_DIM))

            q_s, k_s, v_s = _mixer_b_proj(xn_s, b_w_in[j], b_q_norm[j], b_k_norm[j],
                                          heads=b_heads, tiles=ts)
            q5 = (q_s * (B_HEAD_DIM ** -0.5)).reshape(dec_b, dec_seq, b_heads, 2, B_HEAD_DIM)
            qs = jnp.einsum("bthcd,ce->bhcted", q5, jnp.eye(2, dtype=F32))
            qs = qs.reshape(dec_b, b_heads, 2 * dec_seq, 2 * B_HEAD_DIM)
            pad = ((0, 0), (0, PAGE_SIZE - dec_seq), (0, 0))
            k_new = jnp.pad(k_s.reshape(dec_b, dec_seq, d_model), pad).reshape(
                dec_b, PAGE_SIZE * b_heads, 2 * B_HEAD_DIM)
            v_new = jnp.pad(v_s.reshape(dec_b, dec_seq, d_model), pad).reshape(
                dec_b, PAGE_SIZE * b_heads, B_V_DIM)
            o_s = diff_attn_decode(qs, cache_k2, cache_v2, page_table, k_new, v_new, lam,
                                   b_subln[j], lam_init, layer=j, heads=b_heads, n_q=dec_seq,
                                   pages=4)
            y_s = o_s.transpose(0, 2, 1, 3).reshape(ms, d_model)
            k_s_rows.append(k_s.reshape(dec_b, dec_seq, b_heads, 2 * B_HEAD_DIM))
            v_s_rows.append(v_s.reshape(dec_b, dec_seq, b_heads, B_V_DIM))
            w_out = b_w_out[j]
        else:
            qkv_p = _mixer_c_proj(xn_p, c_w_in[j], lower_bound[i], tiles=tp)
            y_p, st_p = hgrn_recurrence(*qkv_p, c_g_norm[j], None, batch=batch, seq=seq,
                                        heads=c_heads, rows=tp["hgrn_rows"], chunk=C_HEAD_DIM,
                                        out_dtype=BF16)
            qkv_s = _mixer_c_proj(xn_s, c_w_in[j], lower_bound[i], tiles=ts)
            y_s, st_s = hgrn_recurrence(*qkv_s, c_g_norm[j], state_hgrn[j], batch=dec_b,
                                        seq=dec_seq, heads=c_heads, rows=dec_seq, chunk=dec_seq,
                                        out_dtype=F32)
            hgrn_p.append(st_p)
            hgrn_s.append(st_s)
            w_out = c_w_out[j]
        h_p, xf_p = out_proj(y_p, w_out, h_p, norm_ffn[i], tm=tp["out_tm"])
        h_s, xf_s = out_proj(y_s, w_out, h_s, norm_ffn[i], tm=ts["out_tm"])
        gain_next = norm_mix[i + 1] if i + 1 < depth else None
        h_p, xn_p = ffn(xf_p, h_p, ffn_w_gu[i], ffn_w_down[i], gain_next, tm=tp["ffn_tm"], th=tp["ffn_th"])
        h_s, xn_s = ffn(xf_s, h_s, ffn_w_gu[i], ffn_w_down[i], gain_next, tm=ts["ffn_tm"], th=ts["ffn_th"])
    return (h_p.reshape(batch, seq, d_model), h_s.reshape(dec_b, dec_seq, d_model),
            jnp.stack(k_p_rows), jnp.stack(v_p_rows), jnp.stack(k_s_rows), jnp.stack(v_s_rows),
            jnp.stack(hgrn_p), jnp.stack(hgrn_s), jnp.stack(chunk_v_s))
```

```python
import functools
import math

import jax
import jax.numpy as jnp
import numpy as np
from jax import lax
from jax.experimental import pallas as pl
from jax.experimental.pallas import tpu as pltpu

F32 = jnp.float32
BF16 = jnp.bfloat16
EPS = 1e-6

LANES = 128
SUBLANES = 8
VMEM_LIMIT_BYTES = 56 << 20
FFN_VMEM_LIMIT_BYTES = 60 << 20

A_CHUNK = 128
A_GROUPS = 8
B_HEAD_DIM = 64
B_V_DIM = 2 * B_HEAD_DIM
C_HEAD_DIM = 128
PAGE_SIZE = 128
HGRN_SUB = SUBLANES


def _cparams(n_axes):
    return pltpu.CompilerParams(
        dimension_semantics=("arbitrary",) * n_axes,
        vmem_limit_bytes=VMEM_LIMIT_BYTES,
    )


def _rmsnorm_f32(x, gain):
    return x * lax.rsqrt(jnp.mean(x * x, axis=-1, keepdims=True) + EPS) * gain


def _sigmoid(x):
    return 1.0 / (1.0 + jnp.exp(-x))


def _norm_rows_body(x_ref, g_ref, o_ref):
    o_ref[...] = _rmsnorm_f32(x_ref[...], g_ref[...]).astype(o_ref.dtype)


def norm_rows(x, gain, *, tm):
    m, d = x.shape
    return pl.pallas_call(
        _norm_rows_body,
        grid=(m // tm,),
        in_specs=[pl.BlockSpec((tm, d), lambda i: (i, 0)),
                  pl.BlockSpec((1, d), lambda i: (0, 0))],
        out_specs=pl.BlockSpec((tm, d), lambda i: (i, 0)),
        out_shape=jax.ShapeDtypeStruct((m, d), BF16),
        compiler_params=_cparams(1),
        name="norm_rows",
    )(x, gain.reshape(1, d))


def _seg_matmul_body(*refs, ns, nv, nc, epilogue):
    x_ref = refs[0]
    w_refs = refs[1:1 + ns]
    vec_refs = refs[1 + ns:1 + ns + nv]
    const_refs = refs[1 + ns + nv:1 + ns + nv + nc]
    out_refs = refs[1 + ns + nv + nc:-1]
    wb_ref = refs[-1]

    @pl.when(pl.program_id(1) == 0)
    def _():
        for s in range(ns):
            wb_ref[s] = w_refs[s][...].astype(BF16)

    x = x_ref[...]
    accs = [jnp.dot(x, wb_ref[s], preferred_element_type=F32) for s in range(ns)]
    outs = epilogue(accs, [r[...] for r in vec_refs], [r[...] for r in const_refs])
    for r, o in zip(out_refs, outs):
        r[...] = o.astype(r.dtype)


def seg_matmul(x, w, layer, seg_starts, seg_width, epilogue, out_dtypes, vecs=(), consts=(),
               *, tm, tn, name):
    m, k = x.shape
    ns = len(seg_starts)
    in_specs = [pl.BlockSpec((tm, k), lambda j, i: (i, 0))]
    for st in seg_starts:
        in_specs.append(pl.BlockSpec((None, k, tn), lambda j, i, off=st // tn: (layer, 0, off + j)))
    for _ in vecs:
        in_specs.append(pl.BlockSpec((1, tn), lambda j, i: (0, j)))
    for c in consts:
        in_specs.append(pl.BlockSpec(c.shape, lambda j, i, nd=c.ndim: (0,) * nd))
    return pl.pallas_call(
        functools.partial(_seg_matmul_body, ns=ns, nv=len(vecs), nc=len(consts), epilogue=epilogue),
        grid=(seg_width // tn, m // tm),
        in_specs=in_specs,
        out_specs=[pl.BlockSpec((tm, tn), lambda j, i: (i, j)) for _ in out_dtypes],
        out_shape=[jax.ShapeDtypeStruct((m, seg_width), dt) for dt in out_dtypes],
        scratch_shapes=[pltpu.VMEM((ns, k, tn), BF16)],
        compiler_params=_cparams(2),
        name=name,
    )(x, *([w] * ns), *vecs, *consts)


def _erf_f32(x):
    x = jnp.clip(x, -4.0, 4.0)
    x2 = x * x
    alpha = (-2.72614225801306e-10, 2.77068142495902e-08, -2.10102402082508e-06,
             -5.69250639462346e-05, -7.34990630326855e-04, -2.95459980854025e-03,
             -1.60960333262415e-02)
    beta = (-1.45660718464996e-05, -2.13374055278905e-04, -1.68282697438203e-03,
            -7.37332916720468e-03, -1.42647390514189e-02)
    p = jnp.full_like(x2, alpha[0])
    for c in alpha[1:]:
        p = p * x2 + c
    q = jnp.full_like(x2, beta[0])
    for c in beta[1:]:
        q = q * x2 + c
    return x * p / q


def _gelu_epilogue(accs, vecs, consts):
    (a,) = accs
    return [0.5 * a * (1.0 + _erf_f32(a * (2.0 ** -0.5)))]


def _headnorm_epilogue(accs, vecs, consts):
    aq, ak, av = accs
    gq, gk = vecs
    (group_ones,) = consts

    def head_norm(a, g):
        sq = a * a
        hi = sq.astype(BF16)
        lo = (sq - hi.astype(F32)).astype(BF16)
        ms = (jnp.dot(hi, group_ones, preferred_element_type=F32)
              + jnp.dot(lo, group_ones, preferred_element_type=F32)) * (1.0 / B_HEAD_DIM)
        return a * lax.rsqrt(ms + EPS) * g

    return [head_norm(aq, gq), head_norm(ak, gk), av]


def _hgrn_gate_epilogue(accs, vecs, consts):
    aq, af, av, ag = accs
    (lb,) = vecs
    q = aq * _sigmoid(aq)
    f = lb + (1.0 - lb) * _sigmoid(af)
    k = (1.0 - lb) * _sigmoid(-af)
    return [q, k, jnp.log(f), av, ag]


def _out_proj_body(y_ref, w_ref, h_ref, g_ref, ho_ref, xo_ref, wb_ref, *, cast_rows):
    @pl.when(pl.program_id(0) == 0)
    def _():
        def cast(r, carry):
            sl = pl.ds(pl.multiple_of(r * cast_rows, cast_rows), cast_rows)
            wb_ref[sl, :] = w_ref[sl, :].astype(BF16)
            return carry
        lax.fori_loop(0, w_ref.shape[0] // cast_rows, cast, 0)

    hn = h_ref[...] + jnp.dot(y_ref[...].astype(BF16), wb_ref[...], preferred_element_type=F32)
    ho_ref[...] = hn
    xo_ref[...] = _rmsnorm_f32(hn, g_ref[...]).astype(xo_ref.dtype)


def out_proj(y, w, layer, h, gain_next, *, tm):
    m, k = y.shape
    n = w.shape[2]
    return pl.pallas_call(
        functools.partial(_out_proj_body, cast_rows=256),
        grid=(m // tm,),
        in_specs=[pl.BlockSpec((tm, k), lambda i: (i, 0)),
                  pl.BlockSpec((None, k, n), lambda i: (layer, 0, 0), pipeline_mode=pl.Buffered(1)),
                  pl.BlockSpec((tm, n), lambda i: (i, 0)),
                  pl.BlockSpec((1, n), lambda i: (0, 0))],
        out_specs=[pl.BlockSpec((tm, n), lambda i: (i, 0)),
                   pl.BlockSpec((tm, n), lambda i: (i, 0))],
        out_shape=[jax.ShapeDtypeStruct((m, n), F32), jax.ShapeDtypeStruct((m, n), BF16)],
        scratch_shapes=[pltpu.VMEM((k, n), BF16)],
        compiler_params=_cparams(1),
        name="out_proj",
    )(y, w, h, gain_next.reshape(1, n))


def _ffn_body(x_ref, h_ref, wg_ref, wu_ref, wd_ref, g_ref, ho_ref, *maybe_xo, n_t):
    t = pl.program_id(1)

    @pl.when(t == 0)
    def _():
        ho_ref[...] = h_ref[...]

    x = x_ref[...]
    gate = jnp.dot(x, wg_ref[...].astype(BF16), preferred_element_type=F32)
    up = jnp.dot(x, wu_ref[...].astype(BF16), preferred_element_type=F32)
    act = (gate * _sigmoid(gate) * up).astype(BF16)
    ho_ref[...] += jnp.dot(act, wd_ref[...].astype(BF16), preferred_element_type=F32)

    if maybe_xo:
        @pl.when(t == n_t - 1)
        def _():
            maybe_xo[0][...] = _rmsnorm_f32(ho_ref[...], g_ref[...]).astype(BF16)


def ffn(x, h, w_gu, w_down, layer, gain_next, *, tm, th):
    m, d = x.shape
    hidden = w_down.shape[1]
    n_t = hidden // th
    with_norm = gain_next is not None
    gain = gain_next if with_norm else jnp.ones((d,), F32)
    out_specs = [pl.BlockSpec((tm, d), lambda i, t: (i, 0))]
    out_shape = [jax.ShapeDtypeStruct((m, d), F32)]
    if with_norm:
        out_specs.append(pl.BlockSpec((tm, d), lambda i, t: (i, 0)))
        out_shape.append(jax.ShapeDtypeStruct((m, d), BF16))
    row_block = pl.BlockSpec((tm, d), lambda i, t: (i, 0), pipeline_mode=pl.Buffered(1))
    res = pl.pallas_call(
        functools.partial(_ffn_body, n_t=n_t),
        grid=(m // tm, n_t),
        in_specs=[row_block, row_block,
                  pl.BlockSpec((None, d, th), lambda i, t: (layer, 0, t)),
                  pl.BlockSpec((None, d, th), lambda i, t: (layer, 0, n_t + t)),
                  pl.BlockSpec((None, th, d), lambda i, t: (layer, t, 0)),
                  pl.BlockSpec((1, d), lambda i, t: (0, 0))],
        out_specs=out_specs,
        out_shape=out_shape,
        compiler_params=pltpu.CompilerParams(dimension_semantics=("arbitrary",) * 2,
                                             vmem_limit_bytes=FFN_VMEM_LIMIT_BYTES),
        name="ffn",
    )(x, h, w_gu, w_gu, w_down, gain.reshape(1, d))
    return (res[0], res[1]) if with_norm else (res[0], None)


def _spatial_body(u_ref, v_ref, wm_ref, bs_ref, vg_ref, p_ref, vn_ref, *, chunk, groups):
    v = v_ref[...].astype(F32)
    vn = _rmsnorm_f32(v, vg_ref[...])
    vn_ref[...] = vn.astype(vn_ref.dtype)
    vnb = vn.astype(BF16)
    rows, width = v.shape
    gw = width // groups
    for c in range(rows // chunk):
        r0 = c * chunk
        for g in range(groups):
            c0 = g * gw
            s = jnp.dot(wm_ref[g], vnb[r0:r0 + chunk, c0:c0 + gw], preferred_element_type=F32)
            s = s + bs_ref[g]
            u = u_ref[r0:r0 + chunk, c0:c0 + gw].astype(F32)
            p_ref[r0:r0 + chunk, c0:c0 + gw] = (u * s).astype(p_ref.dtype)


def spatial_mix(uv, wm, bs, v_gain, *, chunk, tm, vn_dtype):
    m, w2 = uv.shape
    width = w2 // 2
    groups = wm.shape[0]
    return pl.pallas_call(
        functools.partial(_spatial_body, chunk=chunk, groups=groups),
        grid=(m // tm,),
        in_specs=[pl.BlockSpec((tm, width), lambda i: (i, 0)),
                  pl.BlockSpec((tm, width), lambda i: (i, 1)),
                  pl.BlockSpec(wm.shape, lambda i: (0, 0, 0)),
                  pl.BlockSpec(bs.shape, lambda i: (0, 0, 0)),
                  pl.BlockSpec((1, width), lambda i: (0, 0))],
        out_specs=[pl.BlockSpec((tm, width), lambda i: (i, 0)),
                   pl.BlockSpec((tm, width), lambda i: (i, 0))],
        out_shape=[jax.ShapeDtypeStruct((m, width), BF16),
                   jax.ShapeDtypeStruct((m, width), vn_dtype)],
        compiler_params=_cparams(1),
        name="spatial_mix",
    )(uv, uv, wm, bs, v_gain.reshape(1, width))


def _diff_attn_body(qt_tab, kt_tab, q_ref, k_ref, v_ref, lam_ref, sub_ref, o_ref,
                    qs_ref, m_ref, l_ref, acc_ref, *, tq, out_scale):
    t = pl.program_id(2)
    qi = qt_tab[t]
    ki = kt_tab[t]

    @pl.when(ki == 0)
    def _():
        q = q_ref[...].astype(F32) * (B_HEAD_DIM ** -0.5)
        lane = lax.broadcasted_iota(jnp.int32, q.shape, 1)
        qs_ref[0:tq, :] = jnp.where(lane < B_HEAD_DIM, q, 0.0)
        qs_ref[tq:2 * tq, :] = jnp.where(lane >= B_HEAD_DIM, q, 0.0)
        m_ref[...] = jnp.full(m_ref.shape, -jnp.inf, F32)
        l_ref[...] = jnp.zeros(l_ref.shape, F32)
        acc_ref[...] = jnp.zeros(acc_ref.shape, F32)

    keys = k_ref[...]
    vals = v_ref[...]

    def strip(r0, masked):
        rs = pl.ds(r0, tq)
        s = lax.dot_general(qs_ref[rs, :], keys, (((1,), (1,)), ((), ())), preferred_element_type=F32)
        if masked:
            row = lax.broadcasted_iota(jnp.int32, s.shape, 0)
            col = lax.broadcasted_iota(jnp.int32, s.shape, 1)
            s = jnp.where(col <= row, s, -jnp.inf)
        m_prev = m_ref[rs, :]
        m_new = jnp.maximum(m_prev, jnp.max(s, axis=1, keepdims=True))
        alpha = jnp.exp(m_prev - m_new)
        p = jnp.exp(s - m_new[:, 0:1])
        l_ref[rs, :] = alpha * l_ref[rs, :] + jnp.sum(p, axis=1, keepdims=True)
        acc_ref[rs, :] = alpha * acc_ref[rs, :] + jnp.dot(p, vals, preferred_element_type=F32)
        m_ref[rs, :] = m_new

    @pl.when(ki < qi)
    def _():
        strip(0, False)
        strip(tq, False)

    @pl.when(ki == qi)
    def _():
        strip(0, True)
        strip(tq, True)
        o = acc_ref[...] / l_ref[...]
        d = o[0:tq] - lam_ref[0] * o[tq:2 * tq]
        o_ref[...] = (_rmsnorm_f32(d, sub_ref[...]) * out_scale).astype(o_ref.dtype)


def diff_attn_prompt(q, k, v, lam, subln, lam_init, *, batch, seq, heads, tq):
    nq = seq // tq
    tri = [(qi, ki) for qi in range(nq) for ki in range(qi + 1)]
    qt_tab = jnp.asarray([a for a, _ in tri], jnp.int32)
    kt_tab = jnp.asarray([b for _, b in tri], jnp.int32)
    hd = 2 * B_HEAD_DIM
    grid_spec = pltpu.PrefetchScalarGridSpec(
        num_scalar_prefetch=2,
        grid=(batch, heads, len(tri)),
        in_specs=[pl.BlockSpec((tq, hd), lambda b, h, t, qt, kt: (b * nq + qt[t], h)),
                  pl.BlockSpec((tq, hd), lambda b, h, t, qt, kt: (b * nq + kt[t], h)),
                  pl.BlockSpec((tq, B_V_DIM), lambda b, h, t, qt, kt: (b * nq + kt[t], h)),
                  pl.BlockSpec(memory_space=pltpu.SMEM),
                  pl.BlockSpec((1, B_V_DIM), lambda b, h, t, qt, kt: (0, 0))],
        out_specs=pl.BlockSpec((tq, B_V_DIM), lambda b, h, t, qt, kt: (b * nq + qt[t], h)),
        scratch_shapes=[pltpu.VMEM((2 * tq, hd), F32),
                        pltpu.VMEM((2 * tq, LANES), F32),
                        pltpu.VMEM((2 * tq, LANES), F32),
                        pltpu.VMEM((2 * tq, B_V_DIM), F32)],
    )
    return pl.pallas_call(
        functools.partial(_diff_attn_body, tq=tq, out_scale=1.0 - lam_init),
        grid_spec=grid_spec,
        out_shape=jax.ShapeDtypeStruct((batch * seq, heads * B_V_DIM), BF16),
        compiler_params=_cparams(3),
        name="diff_attn_prompt",
    )(qt_tab, kt_tab, q, k, v, lam.reshape(1), subln.reshape(1, B_V_DIM))


def _decode_attn_body(pt_ref, *refs, pages, heads, n_q, n_groups, out_scale):
    k_refs = refs[:pages]
    v_refs = refs[pages:2 * pages]
    kn_ref, vn_ref, qs_ref, lam_ref, sub_ref, o_ref, s_ref, m_ref, l_ref, acc_ref = refs[2 * pages:]
    g = pl.program_id(1)
    hc = 2 * n_q

    @pl.when(g == 0)
    def _():
        m_ref[...] = jnp.full(m_ref.shape, -jnp.inf, F32)
        l_ref[...] = jnp.zeros(l_ref.shape, F32)
        acc_ref[...] = jnp.zeros(acc_ref.shape, F32)

    def head_rows(ref, h):
        return ref[pl.ds(h, PAGE_SIZE, stride=heads), :]

    def process(page_k_refs, page_v_refs, mask):
        n_p = len(page_k_refs)
        width = n_p * PAGE_SIZE
        for h in range(heads):
            qh = qs_ref[h]
            for p_i in range(n_p):
                s_ref[h * hc:(h + 1) * hc, p_i * PAGE_SIZE:(p_i + 1) * PAGE_SIZE] = lax.dot_general(
                    qh, head_rows(page_k_refs[p_i], h), (((1,), (1,)), ((), ())),
                    preferred_element_type=F32)
        s = s_ref[:, 0:width]
        if mask is not None:
            s = jnp.where(mask, s, -jnp.inf)
        m_prev = m_ref[...]
        m_new = jnp.maximum(m_prev, jnp.max(s, axis=1, keepdims=True))
        alpha = jnp.exp(m_prev - m_new)
        p = jnp.exp(s - m_new)
        l_ref[...] = alpha * l_ref[...] + jnp.sum(p, axis=1, keepdims=True)
        m_ref[...] = m_new
        for h in range(heads):
            r0 = h * hc
            pv = jnp.zeros((hc, B_V_DIM), F32)
            for p_i in range(n_p):
                pv = pv + jnp.dot(p[r0:r0 + hc, p_i * PAGE_SIZE:(p_i + 1) * PAGE_SIZE],
                                  head_rows(page_v_refs[p_i], h), preferred_element_type=F32)
            acc_ref[r0:r0 + hc, :] = acc_ref[r0:r0 + hc, :] * alpha[r0:r0 + hc] + pv

    process(k_refs, v_refs, None)

    @pl.when(g == n_groups - 1)
    def _():
        pos = lax.broadcasted_iota(jnp.int32, (heads * hc, PAGE_SIZE), 1)
        qry = lax.broadcasted_iota(jnp.int32, (heads * hc, PAGE_SIZE), 0) % n_q
        process([kn_ref], [vn_ref], pos <= qry)
        o = acc_ref[...] / l_ref[...]
        o = o.reshape(heads, 2, n_q, B_V_DIM)
        d = o[:, 0] - lam_ref[0] * o[:, 1]
        d = d * lax.rsqrt(jnp.mean(d * d, axis=-1, keepdims=True) + EPS) * sub_ref[...]
        o_ref[...] = (d * out_scale).astype(o_ref.dtype)


def diff_attn_decode(qs, cache_k, cache_v, page_table, k_new, v_new, lam, subln, lam_init,
                     *, layer, heads, n_q, pages):
    dec_b, n_pages = page_table.shape
    hd = 2 * B_HEAD_DIM
    rows = PAGE_SIZE * heads
    hc = 2 * n_q
    n_groups = n_pages // pages
    page_spec = lambda p_i: pl.BlockSpec(
        (None, None, rows, hd),
        lambda b, g, pt, p_i=p_i: (layer, pt[b, g * pages + p_i], 0, 0))
    grid_spec = pltpu.PrefetchScalarGridSpec(
        num_scalar_prefetch=1,
        grid=(dec_b, n_groups),
        in_specs=([page_spec(p_i) for p_i in range(pages)] * 2
                  + [pl.BlockSpec((None, rows, hd), lambda b, g, pt: (b, 0, 0)),
                     pl.BlockSpec((None, rows, hd), lambda b, g, pt: (b, 0, 0)),
                     pl.BlockSpec((None, heads, hc, hd), lambda b, g, pt: (b, 0, 0, 0)),
                     pl.BlockSpec(memory_space=pltpu.SMEM),
                     pl.BlockSpec((1, B_V_DIM), lambda b, g, pt: (0, 0))]),
        out_specs=pl.BlockSpec((None, heads, n_q, B_V_DIM), lambda b, g, pt: (b, 0, 0, 0)),
        scratch_shapes=[pltpu.VMEM((heads * hc, pages * PAGE_SIZE), F32),
                        pltpu.VMEM((heads * hc, 1), F32),
                        pltpu.VMEM((heads * hc, 1), F32),
                        pltpu.VMEM((heads * hc, B_V_DIM), F32)],
    )
    return pl.pallas_call(
        functools.partial(_decode_attn_body, pages=pages, heads=heads, n_q=n_q,
                          n_groups=n_groups, out_scale=1.0 - lam_init),
        grid_spec=grid_spec,
        out_shape=jax.ShapeDtypeStruct((dec_b, heads, n_q, B_V_DIM), F32),
        compiler_params=_cparams(2),
        name="diff_attn_decode",
    )(page_table, *([cache_k] * pages), *([cache_v] * pages), k_new, v_new, qs,
      lam.reshape(1), subln.reshape(1, B_V_DIM))


def _cumsum_rows(x):
    c = x.shape[0]
    sub = lax.broadcasted_iota(jnp.int32, x.shape, 0) % HGRN_SUB
    d = 1
    while d < HGRN_SUB:
        x = x + jnp.where(sub >= d, pltpu.roll(x, d, axis=0), 0.0)
        d *= 2
    blocks = []
    carry = None
    for j in range(c // HGRN_SUB):
        blk = x[j * HGRN_SUB:(j + 1) * HGRN_SUB]
        if carry is not None:
            blk = blk + carry
        blocks.append(blk)
        carry = blk[HGRN_SUB - 1:HGRN_SUB]
    return jnp.concatenate(blocks, axis=0) if len(blocks) > 1 else blocks[0]


def _hgrn_chunk(q, k, lf, v, st):
    c = q.shape[0]
    nb = c // HGRN_SUB
    gcum = _cumsum_rows(lf)
    o = lax.dot_general(q * jnp.exp(gcum), st, (((1,), (1,)), ((), ())), preferred_element_type=F32)

    if nb > 1:
        row = lax.broadcasted_iota(jnp.int32, (c, C_HEAD_DIM), 0)
        t_idx = lax.broadcasted_iota(jnp.int32, (c, c), 0)
        s_idx = lax.broadcasted_iota(jnp.int32, (c, c), 1)
        a_off = None
        size = 2 * HGRN_SUB
        while size <= c:
            half = size // 2
            if size < c:
                g_mid = jnp.concatenate(
                    [jnp.broadcast_to(gcum[b0 + half - 1:b0 + half], (size, C_HEAD_DIM))
                     for b0 in range(0, c, size)], axis=0)
            else:
                g_mid = gcum[half - 1:half]
            upper = (row % size) >= half
            qd = q * jnp.exp(jnp.where(upper, gcum - g_mid, -jnp.inf))
            kd = k * jnp.exp(jnp.where(upper, -jnp.inf, g_mid - gcum))
            a = lax.dot_general(qd, kd, (((1,), (1,)), ((), ())), preferred_element_type=F32)
            if size < c:
                a = jnp.where((t_idx // size) == (s_idx // size), a, 0.0)
            a_off = a if a_off is None else a_off + a
            size *= 2
        o = o + jnp.dot(a_off, v, preferred_element_type=F32)

    sub_row = lax.broadcasted_iota(jnp.int32, (HGRN_SUB, C_HEAD_DIM), 0)
    o_blocks = []
    for i in range(nb):
        r0 = i * HGRN_SUB
        gi = gcum[r0:r0 + HGRN_SUB]
        qi = q[r0:r0 + HGRN_SUB]
        ki = k[r0:r0 + HGRN_SUB]
        vi = v[r0:r0 + HGRN_SUB]
        oi = o[r0:r0 + HGRN_SUB]
        for s in range(HGRN_SUB):
            dec = jnp.exp(jnp.where(sub_row >= s, gi - gi[s:s + 1], -jnp.inf))
            a_col = jnp.sum(qi * ki[s:s + 1] * dec, axis=-1, keepdims=True)
            oi = oi + a_col * vi[s:s + 1]
        o_blocks.append(oi)
    o = jnp.concatenate(o_blocks, axis=0) if nb > 1 else o_blocks[0]
    g_last = gcum[c - 1:c]
    kd = k * jnp.exp(g_last - gcum)
    if c < C_HEAD_DIM:
        pad = jnp.zeros((C_HEAD_DIM - c, C_HEAD_DIM), F32)
        kd = jnp.concatenate([kd, pad], axis=0)
        v = jnp.concatenate([v, pad], axis=0)
    st_new = st * jnp.exp(g_last) + jnp.dot(v.T, kd, preferred_element_type=F32)
    return o, st_new


def _hgrn_body(*refs, chunk, n_chunks, n_r, hb, with_state):
    if with_state:
        q_ref, k_ref, lf_ref, v_ref, g_ref, gg_ref, s0_ref, o_ref, so_ref, st_ref = refs
    else:
        q_ref, k_ref, lf_ref, v_ref, g_ref, gg_ref, o_ref, so_ref, st_ref = refs
    r = pl.program_id(2)
    hd = C_HEAD_DIM

    @pl.when(r == 0)
    def _():
        for h in range(hb):
            if with_state:
                st_ref[h] = s0_ref[h].astype(F32).T
            else:
                st_ref[h] = jnp.zeros((hd, hd), F32)

    def step(ci, carry):
        sl = pl.ds(pl.multiple_of(ci * chunk, chunk), chunk)
        for h in range(hb):
            cs = slice(h * hd, (h + 1) * hd)
            o, st_new = _hgrn_chunk(q_ref[sl, cs], k_ref[sl, cs], lf_ref[sl, cs], v_ref[sl, cs],
                                    st_ref[h])
            st_ref[h] = st_new
            gate = g_ref[sl, cs]
            o = _rmsnorm_f32(o, gg_ref[...]) * (gate * _sigmoid(gate))
            o_ref[sl, cs] = o.astype(o_ref.dtype)
        return carry

    lax.fori_loop(0, n_chunks, step, 0)

    @pl.when(r == n_r - 1)
    def _():
        for h in range(hb):
            so_ref[h] = st_ref[h].T.astype(so_ref.dtype)


def hgrn_recurrence(q, k, lf, v, g, g_gain, state0, *, batch, seq, heads, rows, chunk, hb, out_dtype):
    n_r = seq // rows
    hd = C_HEAD_DIM
    with_state = state0 is not None
    row_spec = pl.BlockSpec((rows, hb * hd), lambda b, h, r: (b * n_r + r, h))
    state_spec = pl.BlockSpec((None, hb, hd, hd), lambda b, h, r: (b, h, 0, 0))
    in_specs = [row_spec] * 5 + [pl.BlockSpec((1, hd), lambda b, h, r: (0, 0))]
    args = [q, k, lf, v, g, g_gain.reshape(1, hd)]
    if with_state:
        in_specs.append(state_spec)
        args.append(state0)
    return pl.pallas_call(
        functools.partial(_hgrn_body, chunk=chunk, n_chunks=rows // chunk, n_r=n_r, hb=hb,
                          with_state=with_state),
        grid=(batch, heads // hb, n_r),
        in_specs=in_specs,
        out_specs=[row_spec, state_spec],
        out_shape=[jax.ShapeDtypeStruct((batch * seq, heads * hd), out_dtype),
                   jax.ShapeDtypeStruct((batch, heads, hd, hd), F32)],
        scratch_shapes=[pltpu.VMEM((hb, hd, hd), F32)],
        compiler_params=_cparams(3),
        name="hgrn_recurrence",
    )(*args)


def _tiles(m):
    big = m >= 1024
    return dict(
        norm_tm=512 if big else m,
        proj_tm=1024 if big else m,
        proj_tn=256,
        out_tm=512 if big else m,
        ffn_tm=1024 if big else m,
        ffn_th=256,
        spatial_tm=256 if big else m,
        attn_tq=512,
        hgrn_rows=512,
        hgrn_heads=2,
    )


def _mixer_a(xn, w_in, layer, v_gain, w_s, b_s, *, chunk_len, n_seq, tiles, vn_dtype):
    width = w_in.shape[2] // 2
    (uv,) = seg_matmul(xn, w_in, layer, (0,), 2 * width, _gelu_epilogue, (BF16,),
                       tm=tiles["proj_tm"], tn=2 * tiles["proj_tn"], name="a_in_proj")
    causal = jnp.tril(jnp.ones((A_CHUNK, A_CHUNK), bool))
    w_masked = jnp.where(causal[None], w_s, 0.0)
    if chunk_len == A_CHUNK:
        wm, bs, chunk = w_masked, b_s, A_CHUNK
    else:
        eye = jnp.eye(n_seq, dtype=w_s.dtype)
        small = w_masked[:, :chunk_len, :chunk_len]
        wm = jnp.einsum("ab,gts->gatbs", eye, small).reshape(
            A_GROUPS, n_seq * chunk_len, n_seq * chunk_len)
        bs = jnp.tile(b_s[:, :chunk_len], (1, n_seq))
        chunk = n_seq * chunk_len
    p, vn = spatial_mix(uv, wm.astype(BF16), bs[:, :, None], v_gain, chunk=chunk,
                        tm=max(tiles["spatial_tm"], chunk) if chunk_len == A_CHUNK else chunk,
                        vn_dtype=vn_dtype)
    return p, vn


def _mixer_b_proj(xn, w_in, layer, q_gain, k_gain, *, heads, tiles):
    width = heads * 2 * B_HEAD_DIM
    tn = tiles["proj_tn"]
    lane_group = np.arange(tn) // B_HEAD_DIM
    group_ones = jnp.asarray(lane_group[:, None] == lane_group[None, :], BF16)
    reps = width // B_HEAD_DIM
    gq = jnp.tile(q_gain.astype(F32), reps).reshape(1, width)
    gk = jnp.tile(k_gain.astype(F32), reps).reshape(1, width)
    return seg_matmul(xn, w_in, layer, (0, width, 2 * width), width, _headnorm_epilogue,
                      (F32, F32, F32), vecs=(gq, gk), consts=(group_ones,),
                      tm=tiles["proj_tm"], tn=tn, name="b_in_proj")


def _mixer_c_proj(xn, w_in, layer, lower_bound, *, tiles):
    width = w_in.shape[2] // 4
    return seg_matmul(xn, w_in, layer, (0, width, 2 * width, 3 * width), width, _hgrn_gate_epilogue,
                      (F32,) * 5, vecs=(lower_bound.reshape(1, width),),
                      tm=tiles["proj_tm"], tn=tiles["proj_tn"], name="c_in_proj")


def kernel(x_prompt, x_sample, cache_k, cache_v, page_table, state_hgrn, norm_mix, norm_ffn, ffn_w_gu, ffn_w_down, a_w_in, a_v_norm, a_w_s, a_b_s, a_w_out, b_w_in, b_q_norm, b_k_norm, b_lambda_q1, b_lambda_k1, b_lambda_q2, b_lambda_k2, b_subln, b_w_out, c_w_in, c_g_norm, c_lower_bounds, c_w_out):
    batch, seq, d_model = x_prompt.shape
    dec_b, dec_seq, _ = x_sample.shape
    depth = norm_mix.shape[0]
    b_heads = d_model // (2 * B_HEAD_DIM)
    c_heads = d_model // C_HEAD_DIM
    mp, ms = batch * seq, dec_b * dec_seq
    tp, ts = _tiles(mp), _tiles(ms)

    probs = jax.nn.softmax(c_lower_bounds.astype(F32), axis=0)
    lower_bound = jnp.cumsum(probs, axis=0) - probs[0]

    h_p = x_prompt.reshape(mp, d_model)
    h_s = x_sample.reshape(ms, d_model)
    xn_p = norm_rows(h_p, norm_mix[0], tm=tp["norm_tm"])
    xn_s = norm_rows(h_s, norm_mix[0], tm=ts["norm_tm"])

    n_phys = cache_k.shape[1]
    cache_k2 = cache_k.reshape(cache_k.shape[0], n_phys, PAGE_SIZE * b_heads, 2 * B_HEAD_DIM)
    cache_v2 = cache_v.reshape(cache_v.shape[0], n_phys, PAGE_SIZE * b_heads, B_V_DIM)

    k_p_rows, v_p_rows, k_s_rows, v_s_rows = [], [], [], []
    hgrn_p, hgrn_s, chunk_v_s = [], [], []
    for i in range(depth):
        kind, j = i % 3, i // 3
        if kind == 0:
            y_p, _ = _mixer_a(xn_p, a_w_in, j, a_v_norm[j], a_w_s[j], a_b_s[j],
                              chunk_len=A_CHUNK, n_seq=batch, tiles=tp, vn_dtype=BF16)
            y_s, vn_s = _mixer_a(xn_s, a_w_in, j, a_v_norm[j], a_w_s[j], a_b_s[j],
                                 chunk_len=dec_seq, n_seq=dec_b, tiles=ts, vn_dtype=F32)
            chunk_v_s.append(vn_s.reshape(dec_b, dec_seq, -1))
            w_out = a_w_out
        elif kind == 1:
            lam_init = 0.8 - 0.6 * math.exp(-0.3 * i)
            lam = (jnp.exp(jnp.sum(b_lambda_q1[j].astype(F32) * b_lambda_k1[j].astype(F32)))
                   - jnp.exp(jnp.sum(b_lambda_q2[j].astype(F32) * b_lambda_k2[j].astype(F32)))
                   + lam_init)
            q_p, k_p, v_p = _mixer_b_proj(xn_p, b_w_in, j, b_q_norm[j], b_k_norm[j],
                                          heads=b_heads, tiles=tp)
            y_p = diff_attn_prompt(q_p, k_p, v_p, lam, b_subln[j], lam_init,
                                   batch=batch, seq=seq, heads=b_heads, tq=tp["attn_tq"])
            k_p_rows.append(k_p.reshape(batch, seq, b_heads, 2 * B_HEAD_DIM))
            v_p_rows.append(v_p.reshape(batch, seq, b_heads, B_V_DIM))

            q_s, k_s, v_s = _mixer_b_proj(xn_s, b_w_in, j, b_q_norm[j], b_k_norm[j],
                                          heads=b_heads, tiles=ts)
            q5 = (q_s * (B_HEAD_DIM ** -0.5)).reshape(dec_b, dec_seq, b_heads, 2, B_HEAD_DIM)
            qs = jnp.einsum("bthcd,ce->bhcted", q5, jnp.eye(2, dtype=F32))
            qs = qs.reshape(dec_b, b_heads, 2 * dec_seq, 2 * B_HEAD_DIM)
            pad = ((0, 0), (0, PAGE_SIZE - dec_seq), (0, 0))
            k_new = jnp.pad(k_s.reshape(dec_b, dec_seq, d_model), pad).reshape(
                dec_b, PAGE_SIZE * b_heads, 2 * B_HEAD_DIM)
            v_new = jnp.pad(v_s.reshape(dec_b, dec_seq, d_model), pad).reshape(
                dec_b, PAGE_SIZE * b_heads, B_V_DIM)
            o_s = diff_attn_decode(qs, cache_k2, cache_v2, page_table, k_new, v_new, lam,
                                   b_subln[j], lam_init, layer=j, heads=b_heads, n_q=dec_seq,
                                   pages=4)
            y_s = o_s.transpose(0, 2, 1, 3).reshape(ms, d_model)
            k_s_rows.append(k_s.reshape(dec_b, dec_seq, b_heads, 2 * B_HEAD_DIM))
            v_s_rows.append(v_s.reshape(dec_b, dec_seq, b_heads, B_V_DIM))
            w_out = b_w_out
        else:
            qkv_p = _mixer_c_proj(xn_p, c_w_in, j, lower_bound[i], tiles=tp)
            y_p, st_p = hgrn_recurrence(*qkv_p, c_g_norm[j], None, batch=batch, seq=seq,
                                        heads=c_heads, rows=tp["hgrn_rows"], chunk=C_HEAD_DIM,
                                        hb=tp["hgrn_heads"], out_dtype=BF16)
            qkv_s = _mixer_c_proj(xn_s, c_w_in, j, lower_bound[i], tiles=ts)
            y_s, st_s = hgrn_recurrence(*qkv_s, c_g_norm[j], state_hgrn[j], batch=dec_b,
                                        seq=dec_seq, heads=c_heads, rows=dec_seq, chunk=dec_seq,
                                        hb=ts["hgrn_heads"], out_dtype=F32)
            hgrn_p.append(st_p)
            hgrn_s.append(st_s)
            w_out = c_w_out
        h_p, xf_p = out_proj(y_p, w_out, j, h_p, norm_ffn[i], tm=tp["out_tm"])
        h_s, xf_s = out_proj(y_s, w_out, j, h_s, norm_ffn[i], tm=ts["out_tm"])
        gain_next = norm_mix[i + 1] if i + 1 < depth else None
        h_p, xn_p = ffn(xf_p, h_p, ffn_w_gu, ffn_w_down, i, gain_next, tm=tp["ffn_tm"], th=tp["ffn_th"])
        h_s, xn_s = ffn(xf_s, h_s, ffn_w_gu, ffn_w_down, i, gain_next, tm=ts["ffn_tm"], th=ts["ffn_th"])
    return (h_p.reshape(batch, seq, d_model), h_s.reshape(dec_b, dec_seq, d_model),
            jnp.stack(k_p_rows), jnp.stack(v_p_rows), jnp.stack(k_s_rows), jnp.stack(v_s_rows),
            jnp.stack(hgrn_p), jnp.stack(hgrn_s), jnp.stack(chunk_v_s))
```

```python
import functools
import math

import jax
import jax.numpy as jnp
import numpy as np
from jax import lax
from jax.experimental import pallas as pl
from jax.experimental.pallas import tpu as pltpu

F32 = jnp.float32
BF16 = jnp.bfloat16
EPS = 1e-6

LANES = 128
SUBLANES = 8
VMEM_LIMIT_BYTES = 56 << 20
FFN_VMEM_LIMIT_BYTES = 60 << 20

A_CHUNK = 128
A_GROUPS = 8
B_HEAD_DIM = 64
B_V_DIM = 2 * B_HEAD_DIM
C_HEAD_DIM = 128
PAGE_SIZE = 128
HGRN_SUB = SUBLANES


def _cparams(n_axes):
    return pltpu.CompilerParams(
        dimension_semantics=("arbitrary",) * n_axes,
        vmem_limit_bytes=VMEM_LIMIT_BYTES,
    )


def _rmsnorm_f32(x, gain):
    return x * lax.rsqrt(jnp.mean(x * x, axis=-1, keepdims=True) + EPS) * gain


def _sigmoid(x):
    return 1.0 / (1.0 + jnp.exp(-x))


def _norm_rows_body(x_ref, g_ref, o_ref):
    o_ref[...] = _rmsnorm_f32(x_ref[...], g_ref[...]).astype(o_ref.dtype)


def norm_rows(x, gain, *, tm):
    m, d = x.shape
    return pl.pallas_call(
        _norm_rows_body,
        grid=(m // tm,),
        in_specs=[pl.BlockSpec((tm, d), lambda i: (i, 0)),
                  pl.BlockSpec((1, d), lambda i: (0, 0))],
        out_specs=pl.BlockSpec((tm, d), lambda i: (i, 0)),
        out_shape=jax.ShapeDtypeStruct((m, d), BF16),
        compiler_params=_cparams(1),
        name="norm_rows",
    )(x, gain.reshape(1, d))


def _seg_matmul_body(*refs, ns, nv, nc, epilogue):
    x_ref = refs[0]
    w_refs = refs[1:1 + ns]
    vec_refs = refs[1 + ns:1 + ns + nv]
    const_refs = refs[1 + ns + nv:1 + ns + nv + nc]
    out_refs = refs[1 + ns + nv + nc:-1]
    wb_ref = refs[-1]

    @pl.when(pl.program_id(1) == 0)
    def _():
        for s in range(ns):
            wb_ref[s] = w_refs[s][...].astype(BF16)

    x = x_ref[...]
    accs = [jnp.dot(x, wb_ref[s], preferred_element_type=F32) for s in range(ns)]
    outs = epilogue(accs, [r[...] for r in vec_refs], [r[...] for r in const_refs])
    for r, o in zip(out_refs, outs):
        r[...] = o.astype(r.dtype)


def seg_matmul(x, w, layer, seg_starts, seg_width, epilogue, out_dtypes, vecs=(), consts=(),
               *, tm, tn, name):
    m, k = x.shape
    ns = len(seg_starts)
    in_specs = [pl.BlockSpec((tm, k), lambda j, i: (i, 0))]
    for st in seg_starts:
        in_specs.append(pl.BlockSpec((None, k, tn), lambda j, i, off=st // tn: (layer, 0, off + j)))
    for _ in vecs:
        in_specs.append(pl.BlockSpec((1, tn), lambda j, i: (0, j)))
    for c in consts:
        in_specs.append(pl.BlockSpec(c.shape, lambda j, i, nd=c.ndim: (0,) * nd))
    return pl.pallas_call(
        functools.partial(_seg_matmul_body, ns=ns, nv=len(vecs), nc=len(consts), epilogue=epilogue),
        grid=(seg_width // tn, m // tm),
        in_specs=in_specs,
        out_specs=[pl.BlockSpec((tm, tn), lambda j, i: (i, j)) for _ in out_dtypes],
        out_shape=[jax.ShapeDtypeStruct((m, seg_width), dt) for dt in out_dtypes],
        scratch_shapes=[pltpu.VMEM((ns, k, tn), BF16)],
        compiler_params=_cparams(2),
        name=name,
    )(x, *([w] * ns), *vecs, *consts)


def _erf_f32(x):
    x = jnp.clip(x, -4.0, 4.0)
    x2 = x * x
    alpha = (-2.72614225801306e-10, 2.77068142495902e-08, -2.10102402082508e-06,
             -5.69250639462346e-05, -7.34990630326855e-04, -2.95459980854025e-03,
             -1.60960333262415e-02)
    beta = (-1.45660718464996e-05, -2.13374055278905e-04, -1.68282697438203e-03,
            -7.37332916720468e-03, -1.42647390514189e-02)
    p = jnp.full_like(x2, alpha[0])
    for c in alpha[1:]:
        p = p * x2 + c
    q = jnp.full_like(x2, beta[0])
    for c in beta[1:]:
        q = q * x2 + c
    return x * p / q


def _gelu_epilogue(accs, vecs, consts):
    (a,) = accs
    return [0.5 * a * (1.0 + _erf_f32(a * (2.0 ** -0.5)))]


def _headnorm_epilogue(accs, vecs, consts):
    aq, ak, av = accs
    gq, gk = vecs
    (group_ones,) = consts

    def head_norm(a, g):
        sq = a * a
        hi = sq.astype(BF16)
        lo = (sq - hi.astype(F32)).astype(BF16)
        ms = (jnp.dot(hi, group_ones, preferred_element_type=F32)
              + jnp.dot(lo, group_ones, preferred_element_type=F32)) * (1.0 / B_HEAD_DIM)
        return a * lax.rsqrt(ms + EPS) * g

    return [head_norm(aq, gq), head_norm(ak, gk), av]


def _hgrn_gate_epilogue(accs, vecs, consts):
    aq, af, av, ag = accs
    (lb,) = vecs
    q = aq * _sigmoid(aq)
    f = lb + (1.0 - lb) * _sigmoid(af)
    k = (1.0 - lb) * _sigmoid(-af)
    return [q, k, jnp.log(f), av, ag]


def _out_proj_body(y_ref, w_ref, h_ref, g_ref, ho_ref, xo_ref, wb_ref, *, cast_rows):
    @pl.when(pl.program_id(0) == 0)
    def _():
        def cast(r, carry):
            sl = pl.ds(pl.multiple_of(r * cast_rows, cast_rows), cast_rows)
            wb_ref[sl, :] = w_ref[sl, :].astype(BF16)
            return carry
        lax.fori_loop(0, w_ref.shape[0] // cast_rows, cast, 0)

    hn = h_ref[...] + jnp.dot(y_ref[...].astype(BF16), wb_ref[...], preferred_element_type=F32)
    ho_ref[...] = hn
    xo_ref[...] = _rmsnorm_f32(hn, g_ref[...]).astype(xo_ref.dtype)


def out_proj(y, w, layer, h, gain_next, *, tm):
    m, k = y.shape
    n = w.shape[2]
    return pl.pallas_call(
        functools.partial(_out_proj_body, cast_rows=256),
        grid=(m // tm,),
        in_specs=[pl.BlockSpec((tm, k), lambda i: (i, 0)),
                  pl.BlockSpec((None, k, n), lambda i: (layer, 0, 0), pipeline_mode=pl.Buffered(1)),
                  pl.BlockSpec((tm, n), lambda i: (i, 0)),
                  pl.BlockSpec((1, n), lambda i: (0, 0))],
        out_specs=[pl.BlockSpec((tm, n), lambda i: (i, 0)),
                   pl.BlockSpec((tm, n), lambda i: (i, 0))],
        out_shape=[jax.ShapeDtypeStruct((m, n), F32), jax.ShapeDtypeStruct((m, n), BF16)],
        scratch_shapes=[pltpu.VMEM((k, n), BF16)],
        compiler_params=_cparams(1),
        name="out_proj",
    )(y, w, h, gain_next.reshape(1, n))


def _ffn_body(x_ref, h_ref, xs_ref, hs_ref, wg_ref, wu_ref, wd_ref, g_ref, *out_refs, n_t, with_norm):
    if with_norm:
        ho_ref, hso_ref, xo_ref, xso_ref = out_refs
    else:
        ho_ref, hso_ref = out_refs
    i = pl.program_id(0)
    t = pl.program_id(1)

    @pl.when(t == 0)
    def _():
        ho_ref[...] = h_ref[...]

    def swiglu(x):
        gate = jnp.dot(x, wg_ref[...].astype(BF16), preferred_element_type=F32)
        up = jnp.dot(x, wu_ref[...].astype(BF16), preferred_element_type=F32)
        act = (gate * _sigmoid(gate) * up).astype(BF16)
        return jnp.dot(act, wd_ref[...].astype(BF16), preferred_element_type=F32)

    ho_ref[...] += swiglu(x_ref[...])

    @pl.when((i == 0) & (t == 0))
    def _():
        hso_ref[...] = hs_ref[...]

    @pl.when(i == 0)
    def _():
        hso_ref[...] += swiglu(xs_ref[...])

    if with_norm:
        @pl.when(t == n_t - 1)
        def _():
            xo_ref[...] = _rmsnorm_f32(ho_ref[...], g_ref[...]).astype(BF16)

        @pl.when((i == 0) & (t == n_t - 1))
        def _():
            xso_ref[...] = _rmsnorm_f32(hso_ref[...], g_ref[...]).astype(BF16)


def ffn(x, h, xs, hs, w_gu, w_down, layer, gain_next, *, tm, th):
    m, d = x.shape
    ms = xs.shape[0]
    hidden = w_down.shape[1]
    n_t = hidden // th
    with_norm = gain_next is not None
    gain = gain_next if with_norm else jnp.ones((d,), F32)
    row_out = pl.BlockSpec((tm, d), lambda i, t: (i, 0))
    sample_block = pl.BlockSpec((ms, d), lambda i, t: (0, 0))
    out_specs = [row_out, sample_block]
    out_shape = [jax.ShapeDtypeStruct((m, d), F32), jax.ShapeDtypeStruct((ms, d), F32)]
    if with_norm:
        out_specs += [row_out, sample_block]
        out_shape += [jax.ShapeDtypeStruct((m, d), BF16), jax.ShapeDtypeStruct((ms, d), BF16)]
    row_block = pl.BlockSpec((tm, d), lambda i, t: (i, 0), pipeline_mode=pl.Buffered(1))
    res = pl.pallas_call(
        functools.partial(_ffn_body, n_t=n_t, with_norm=with_norm),
        grid=(m // tm, n_t),
        in_specs=[row_block, row_block, sample_block, sample_block,
                  pl.BlockSpec((None, d, th), lambda i, t: (layer, 0, t)),
                  pl.BlockSpec((None, d, th), lambda i, t: (layer, 0, n_t + t)),
                  pl.BlockSpec((None, th, d), lambda i, t: (layer, t, 0)),
                  pl.BlockSpec((1, d), lambda i, t: (0, 0))],
        out_specs=out_specs,
        out_shape=out_shape,
        compiler_params=pltpu.CompilerParams(dimension_semantics=("arbitrary",) * 2,
                                             vmem_limit_bytes=FFN_VMEM_LIMIT_BYTES),
        name="ffn",
    )(x, h, xs, hs, w_gu, w_gu, w_down, gain.reshape(1, d))
    return tuple(res) if with_norm else (res[0], res[1], None, None)


def _spatial_body(u_ref, v_ref, wm_ref, bs_ref, vg_ref, p_ref, vn_ref, *, chunk, groups):
    v = v_ref[...].astype(F32)
    vn = _rmsnorm_f32(v, vg_ref[...])
    vn_ref[...] = vn.astype(vn_ref.dtype)
    vnb = vn.astype(BF16)
    rows, width = v.shape
    gw = width // groups
    for c in range(rows // chunk):
        r0 = c * chunk
        for g in range(groups):
            c0 = g * gw
            s = jnp.dot(wm_ref[g], vnb[r0:r0 + chunk, c0:c0 + gw], preferred_element_type=F32)
            s = s + bs_ref[g]
            u = u_ref[r0:r0 + chunk, c0:c0 + gw].astype(F32)
            p_ref[r0:r0 + chunk, c0:c0 + gw] = (u * s).astype(p_ref.dtype)


def spatial_mix(uv, wm, bs, v_gain, *, chunk, tm, vn_dtype):
    m, w2 = uv.shape
    width = w2 // 2
    groups = wm.shape[0]
    return pl.pallas_call(
        functools.partial(_spatial_body, chunk=chunk, groups=groups),
        grid=(m // tm,),
        in_specs=[pl.BlockSpec((tm, width), lambda i: (i, 0)),
                  pl.BlockSpec((tm, width), lambda i: (i, 1)),
                  pl.BlockSpec(wm.shape, lambda i: (0, 0, 0)),
                  pl.BlockSpec(bs.shape, lambda i: (0, 0, 0)),
                  pl.BlockSpec((1, width), lambda i: (0, 0))],
        out_specs=[pl.BlockSpec((tm, width), lambda i: (i, 0)),
                   pl.BlockSpec((tm, width), lambda i: (i, 0))],
        out_shape=[jax.ShapeDtypeStruct((m, width), BF16),
                   jax.ShapeDtypeStruct((m, width), vn_dtype)],
        compiler_params=_cparams(1),
        name="spatial_mix",
    )(uv, uv, wm, bs, v_gain.reshape(1, width))


def _diff_attn_body(qt_tab, kt_tab, q_ref, k_ref, v_ref, lam_ref, sub_ref, o_ref,
                    qs_ref, m_ref, l_ref, acc_ref, *, tq, hb, out_scale):
    t = pl.program_id(2)
    qi = qt_tab[t]
    ki = kt_tab[t]
    hd = 2 * B_HEAD_DIM

    @pl.when(ki == 0)
    def _():
        for h in range(hb):
            q = q_ref[:, h * hd:(h + 1) * hd].astype(F32) * (B_HEAD_DIM ** -0.5)
            lane = lax.broadcasted_iota(jnp.int32, q.shape, 1)
            qs_ref[h, 0:tq, :] = jnp.where(lane < B_HEAD_DIM, q, 0.0)
            qs_ref[h, tq:2 * tq, :] = jnp.where(lane >= B_HEAD_DIM, q, 0.0)
        m_ref[...] = jnp.full(m_ref.shape, -jnp.inf, F32)
        l_ref[...] = jnp.zeros(l_ref.shape, F32)
        acc_ref[...] = jnp.zeros(acc_ref.shape, F32)

    def strip(h, r0, masked):
        rs = pl.ds(r0, tq)
        keys = k_ref[:, h * hd:(h + 1) * hd]
        vals = v_ref[:, h * B_V_DIM:(h + 1) * B_V_DIM]
        s = lax.dot_general(qs_ref[h, rs, :], keys, (((1,), (1,)), ((), ())),
                            preferred_element_type=F32)
        if masked:
            row = lax.broadcasted_iota(jnp.int32, s.shape, 0)
            col = lax.broadcasted_iota(jnp.int32, s.shape, 1)
            s = jnp.where(col <= row, s, -jnp.inf)
        m_prev = m_ref[h, rs, :]
        m_new = jnp.maximum(m_prev, jnp.max(s, axis=1, keepdims=True))
        alpha = jnp.exp(m_prev - m_new)
        p = jnp.exp(s - m_new[:, 0:1])
        l_ref[h, rs, :] = alpha * l_ref[h, rs, :] + jnp.sum(p, axis=1, keepdims=True)
        acc_ref[h, rs, :] = alpha * acc_ref[h, rs, :] + jnp.dot(p, vals, preferred_element_type=F32)
        m_ref[h, rs, :] = m_new

    @pl.when(ki < qi)
    def _():
        for h in range(hb):
            strip(h, 0, False)
            strip(h, tq, False)

    @pl.when(ki == qi)
    def _():
        for h in range(hb):
            strip(h, 0, True)
            strip(h, tq, True)
            o = acc_ref[h] / l_ref[h]
            d = o[0:tq] - lam_ref[0] * o[tq:2 * tq]
            o_ref[:, h * B_V_DIM:(h + 1) * B_V_DIM] = (
                _rmsnorm_f32(d, sub_ref[...]) * out_scale).astype(o_ref.dtype)


def diff_attn_prompt(q, k, v, lam, subln, lam_init, *, batch, seq, heads, tq, hb):
    nq = seq // tq
    tri = [(qi, ki) for qi in range(nq) for ki in range(qi + 1)]
    qt_tab = jnp.asarray([a for a, _ in tri], jnp.int32)
    kt_tab = jnp.asarray([b for _, b in tri], jnp.int32)
    hd = 2 * B_HEAD_DIM
    grid_spec = pltpu.PrefetchScalarGridSpec(
        num_scalar_prefetch=2,
        grid=(batch, heads // hb, len(tri)),
        in_specs=[pl.BlockSpec((tq, hb * hd), lambda b, h, t, qt, kt: (b * nq + qt[t], h)),
                  pl.BlockSpec((tq, hb * hd), lambda b, h, t, qt, kt: (b * nq + kt[t], h)),
                  pl.BlockSpec((tq, hb * B_V_DIM), lambda b, h, t, qt, kt: (b * nq + kt[t], h)),
                  pl.BlockSpec(memory_space=pltpu.SMEM),
                  pl.BlockSpec((1, B_V_DIM), lambda b, h, t, qt, kt: (0, 0))],
        out_specs=pl.BlockSpec((tq, hb * B_V_DIM), lambda b, h, t, qt, kt: (b * nq + qt[t], h)),
        scratch_shapes=[pltpu.VMEM((hb, 2 * tq, hd), F32),
                        pltpu.VMEM((hb, 2 * tq, LANES), F32),
                        pltpu.VMEM((hb, 2 * tq, LANES), F32),
                        pltpu.VMEM((hb, 2 * tq, B_V_DIM), F32)],
    )
    return pl.pallas_call(
        functools.partial(_diff_attn_body, tq=tq, hb=hb, out_scale=1.0 - lam_init),
        grid_spec=grid_spec,
        out_shape=jax.ShapeDtypeStruct((batch * seq, heads * B_V_DIM), BF16),
        compiler_params=_cparams(3),
        name="diff_attn_prompt",
    )(qt_tab, kt_tab, q, k, v, lam.reshape(1), subln.reshape(1, B_V_DIM))


def _decode_attn_body(pt_ref, *refs, pages, heads, n_q, n_groups, out_scale):
    k_refs = refs[:pages]
    v_refs = refs[pages:2 * pages]
    kn_ref, vn_ref, qt_ref, lam_ref, sub_ref, o_ref, m_ref, l_ref, acc_ref = refs[2 * pages:]
    g = pl.program_id(1)
    hg = heads // SUBLANES
    hc = 2 * n_q
    cols = SUBLANES * hc

    @pl.when(g == 0)
    def _():
        m_ref[...] = jnp.full(m_ref.shape, -jnp.inf, F32)
        l_ref[...] = jnp.zeros(l_ref.shape, F32)
        acc_ref[...] = jnp.zeros(acc_ref.shape, F32)

    sub = lax.broadcasted_iota(jnp.int32, (SUBLANES, cols), 0)
    lane = lax.broadcasted_iota(jnp.int32, (SUBLANES, cols), 1)
    own = sub == lane // hc

    def to_column(x8):
        r = jnp.sum(jnp.where(own, x8, 0.0), axis=0, keepdims=True)
        return jnp.broadcast_to(r, (LANES, cols)).T

    def group_rows(ref, j, n_pos):
        x = ref[0:n_pos * heads, :].reshape(n_pos, hg, SUBLANES, 2 * B_HEAD_DIM)
        return x[:, j].reshape(n_pos * SUBLANES, 2 * B_HEAD_DIM)

    def process(page_k_refs, page_v_refs, n_pos, new_tokens):
        for j in range(hg):
            scores = []
            for k_ref in page_k_refs:
                s = jnp.dot(group_rows(k_ref, j, n_pos), qt_ref[j], preferred_element_type=F32)
                s = s.reshape(n_pos, SUBLANES, cols)
                valid = own[None]
                if new_tokens:
                    pos = lax.broadcasted_iota(jnp.int32, s.shape, 0)
                    qry = lax.broadcasted_iota(jnp.int32, s.shape, 2) % n_q
                    valid = valid & (pos <= qry)
                scores.append(jnp.where(valid, s, -jnp.inf))
            m_prev = m_ref[j]
            m_new = m_prev
            for s in scores:
                m_new = jnp.maximum(m_new, jnp.max(s, axis=0))
            m_safe = jnp.where(own, m_new, 0.0)
            alpha = jnp.exp(m_prev - m_safe)
            l_new = alpha * l_ref[j]
            pv = jnp.zeros((cols, B_V_DIM), F32)
            for s, v_ref in zip(scores, page_v_refs):
                p = jnp.exp(s - m_safe[None])
                l_new = l_new + jnp.sum(p, axis=0)
                pv = pv + lax.dot_general(p.reshape(n_pos * SUBLANES, cols), group_rows(v_ref, j, n_pos),
                                          (((0,), (0,)), ((), ())), preferred_element_type=F32)
            acc_ref[j] = acc_ref[j] * to_column(alpha) + pv
            l_ref[j] = l_new
            m_ref[j] = m_new

    process(k_refs, v_refs, PAGE_SIZE, False)

    @pl.when(g == n_groups - 1)
    def _():
        process([kn_ref], [vn_ref], n_q, True)
        for j in range(hg):
            o = acc_ref[j] / to_column(l_ref[j])
            o = o.reshape(SUBLANES, 2, n_q, B_V_DIM)
            d = o[:, 0] - lam_ref[0] * o[:, 1]
            d = d * lax.rsqrt(jnp.mean(d * d, axis=-1, keepdims=True) + EPS) * sub_ref[...]
            o_ref[j * SUBLANES:(j + 1) * SUBLANES] = (d * out_scale).astype(o_ref.dtype)


def diff_attn_decode(qt, cache_k, cache_v, page_table, k_new, v_new, lam, subln, lam_init,
                     *, layer, heads, n_q, pages):
    dec_b, n_pages = page_table.shape
    hd = 2 * B_HEAD_DIM
    rows = PAGE_SIZE * heads
    hg = heads // SUBLANES
    cols = SUBLANES * 2 * n_q
    n_groups = n_pages // pages
    page_spec = lambda p_i: pl.BlockSpec(
        (None, None, rows, hd),
        lambda b, g, pt, p_i=p_i: (layer, pt[b, g * pages + p_i], 0, 0))
    grid_spec = pltpu.PrefetchScalarGridSpec(
        num_scalar_prefetch=1,
        grid=(dec_b, n_groups),
        in_specs=([page_spec(p_i) for p_i in range(pages)] * 2
                  + [pl.BlockSpec((None, n_q * heads, hd), lambda b, g, pt: (b, 0, 0)),
                     pl.BlockSpec((None, n_q * heads, hd), lambda b, g, pt: (b, 0, 0)),
                     pl.BlockSpec((None, hg, hd, cols), lambda b, g, pt: (b, 0, 0, 0)),
                     pl.BlockSpec(memory_space=pltpu.SMEM),
                     pl.BlockSpec((1, B_V_DIM), lambda b, g, pt: (0, 0))]),
        out_specs=pl.BlockSpec((None, heads, n_q, B_V_DIM), lambda b, g, pt: (b, 0, 0, 0)),
        scratch_shapes=[pltpu.VMEM((hg, SUBLANES, cols), F32),
                        pltpu.VMEM((hg, SUBLANES, cols), F32),
                        pltpu.VMEM((hg, cols, B_V_DIM), F32)],
    )
    return pl.pallas_call(
        functools.partial(_decode_attn_body, pages=pages, heads=heads, n_q=n_q,
                          n_groups=n_groups, out_scale=1.0 - lam_init),
        grid_spec=grid_spec,
        out_shape=jax.ShapeDtypeStruct((dec_b, heads, n_q, B_V_DIM), F32),
        compiler_params=_cparams(2),
        name="diff_attn_decode",
    )(page_table, *([cache_k] * pages), *([cache_v] * pages), k_new, v_new, qt,
      lam.reshape(1), subln.reshape(1, B_V_DIM))


def _cumsum_rows(x):
    c = x.shape[0]
    sub = lax.broadcasted_iota(jnp.int32, x.shape, 0) % HGRN_SUB
    d = 1
    while d < HGRN_SUB:
        x = x + jnp.where(sub >= d, pltpu.roll(x, d, axis=0), 0.0)
        d *= 2
    blocks = []
    carry = None
    for j in range(c // HGRN_SUB):
        blk = x[j * HGRN_SUB:(j + 1) * HGRN_SUB]
        if carry is not None:
            blk = blk + carry
        blocks.append(blk)
        carry = blk[HGRN_SUB - 1:HGRN_SUB]
    return jnp.concatenate(blocks, axis=0) if len(blocks) > 1 else blocks[0]


def _hgrn_chunk(q, k, lf, v, st):
    c = q.shape[0]
    nb = c // HGRN_SUB
    gcum = _cumsum_rows(lf)
    o = lax.dot_general(q * jnp.exp(gcum), st, (((1,), (1,)), ((), ())), preferred_element_type=F32)

    if nb > 1:
        row = lax.broadcasted_iota(jnp.int32, (c, C_HEAD_DIM), 0)
        t_idx = lax.broadcasted_iota(jnp.int32, (c, c), 0)
        s_idx = lax.broadcasted_iota(jnp.int32, (c, c), 1)
        a_off = None
        size = 2 * HGRN_SUB
        while size <= c:
            half = size // 2
            if size < c:
                g_mid = jnp.concatenate(
                    [jnp.broadcast_to(gcum[b0 + half - 1:b0 + half], (size, C_HEAD_DIM))
                     for b0 in range(0, c, size)], axis=0)
            else:
                g_mid = gcum[half - 1:half]
            upper = (row % size) >= half
            qd = q * jnp.exp(jnp.where(upper, gcum - g_mid, -jnp.inf))
            kd = k * jnp.exp(jnp.where(upper, -jnp.inf, g_mid - gcum))
            a = lax.dot_general(qd, kd, (((1,), (1,)), ((), ())), preferred_element_type=F32)
            if size < c:
                a = jnp.where((t_idx // size) == (s_idx // size), a, 0.0)
            a_off = a if a_off is None else a_off + a
            size *= 2
        o = o + jnp.dot(a_off, v, preferred_element_type=F32)

    sub_row = lax.broadcasted_iota(jnp.int32, (HGRN_SUB, C_HEAD_DIM), 0)
    o_blocks = []
    for i in range(nb):
        r0 = i * HGRN_SUB
        gi = gcum[r0:r0 + HGRN_SUB]
        qi = q[r0:r0 + HGRN_SUB]
        ki = k[r0:r0 + HGRN_SUB]
        vi = v[r0:r0 + HGRN_SUB]
        oi = o[r0:r0 + HGRN_SUB]
        for s in range(HGRN_SUB):
            dec = jnp.exp(jnp.where(sub_row >= s, gi - gi[s:s + 1], -jnp.inf))
            a_col = jnp.sum(qi * ki[s:s + 1] * dec, axis=-1, keepdims=True)
            oi = oi + a_col * vi[s:s + 1]
        o_blocks.append(oi)
    o = jnp.concatenate(o_blocks, axis=0) if nb > 1 else o_blocks[0]
    g_last = gcum[c - 1:c]
    kd = k * jnp.exp(g_last - gcum)
    if c < C_HEAD_DIM:
        pad = jnp.zeros((C_HEAD_DIM - c, C_HEAD_DIM), F32)
        kd = jnp.concatenate([kd, pad], axis=0)
        v = jnp.concatenate([v, pad], axis=0)
    st_new = st * jnp.exp(g_last) + jnp.dot(v.T, kd, preferred_element_type=F32)
    return o, st_new


def _hgrn_body(*refs, chunk, n_chunks, n_r, hb, with_state):
    if with_state:
        q_ref, k_ref, lf_ref, v_ref, g_ref, gg_ref, s0_ref, o_ref, so_ref, st_ref = refs
    else:
        q_ref, k_ref, lf_ref, v_ref, g_ref, gg_ref, o_ref, so_ref, st_ref = refs
    r = pl.program_id(2)
    hd = C_HEAD_DIM

    @pl.when(r == 0)
    def _():
        for h in range(hb):
            if with_state:
                st_ref[h] = s0_ref[h].astype(F32).T
            else:
                st_ref[h] = jnp.zeros((hd, hd), F32)

    def step(ci, carry):
        sl = pl.ds(pl.multiple_of(ci * chunk, chunk), chunk)
        for h in range(hb):
            cs = slice(h * hd, (h + 1) * hd)
            o, st_new = _hgrn_chunk(q_ref[sl, cs], k_ref[sl, cs], lf_ref[sl, cs], v_ref[sl, cs],
                                    st_ref[h])
            st_ref[h] = st_new
            gate = g_ref[sl, cs]
            o = _rmsnorm_f32(o, gg_ref[...]) * (gate * _sigmoid(gate))
            o_ref[sl, cs] = o.astype(o_ref.dtype)
        return carry

    lax.fori_loop(0, n_chunks, step, 0)

    @pl.when(r == n_r - 1)
    def _():
        for h in range(hb):
            so_ref[h] = st_ref[h].T.astype(so_ref.dtype)


def hgrn_recurrence(q, k, lf, v, g, g_gain, state0, *, batch, seq, heads, rows, chunk, hb, out_dtype):
    n_r = seq // rows
    hd = C_HEAD_DIM
    with_state = state0 is not None
    row_spec = pl.BlockSpec((rows, hb * hd), lambda b, h, r: (b * n_r + r, h))
    state_spec = pl.BlockSpec((None, hb, hd, hd), lambda b, h, r: (b, h, 0, 0))
    in_specs = [row_spec] * 5 + [pl.BlockSpec((1, hd), lambda b, h, r: (0, 0))]
    args = [q, k, lf, v, g, g_gain.reshape(1, hd)]
    if with_state:
        in_specs.append(state_spec)
        args.append(state0)
    return pl.pallas_call(
        functools.partial(_hgrn_body, chunk=chunk, n_chunks=rows // chunk, n_r=n_r, hb=hb,
                          with_state=with_state),
        grid=(batch, heads // hb, n_r),
        in_specs=in_specs,
        out_specs=[row_spec, state_spec],
        out_shape=[jax.ShapeDtypeStruct((batch * seq, heads * hd), out_dtype),
                   jax.ShapeDtypeStruct((batch, heads, hd, hd), F32)],
        scratch_shapes=[pltpu.VMEM((hb, hd, hd), F32)],
        compiler_params=_cparams(3),
        name="hgrn_recurrence",
    )(*args)


def _tiles(m):
    big = m >= 1024
    return dict(
        norm_tm=512 if big else m,
        proj_tm=1024 if big else m,
        proj_tn=256,
        out_tm=512 if big else m,
        ffn_tm=1024 if big else m,
        ffn_th=256,
        spatial_tm=256 if big else m,
        attn_tq=512,
        hgrn_rows=512,
        attn_heads=4,
        hgrn_heads=2 if big else 8,
    )


def _mixer_a(xn, w_in, layer, v_gain, w_s, b_s, *, chunk_len, n_seq, tiles, vn_dtype):
    width = w_in.shape[2] // 2
    (uv,) = seg_matmul(xn, w_in, layer, (0,), 2 * width, _gelu_epilogue, (BF16,),
                       tm=tiles["proj_tm"], tn=2 * tiles["proj_tn"], name="a_in_proj")
    causal = jnp.tril(jnp.ones((A_CHUNK, A_CHUNK), bool))
    w_masked = jnp.where(causal[None], w_s, 0.0)
    if chunk_len == A_CHUNK:
        wm, bs, chunk = w_masked, b_s, A_CHUNK
    else:
        eye = jnp.eye(n_seq, dtype=w_s.dtype)
        small = w_masked[:, :chunk_len, :chunk_len]
        wm = jnp.einsum("ab,gts->gatbs", eye, small).reshape(
            A_GROUPS, n_seq * chunk_len, n_seq * chunk_len)
        bs = jnp.tile(b_s[:, :chunk_len], (1, n_seq))
        chunk = n_seq * chunk_len
    p, vn = spatial_mix(uv, wm.astype(BF16), bs[:, :, None], v_gain, chunk=chunk,
                        tm=max(tiles["spatial_tm"], chunk) if chunk_len == A_CHUNK else chunk,
                        vn_dtype=vn_dtype)
    return p, vn


def _mixer_b_proj(xn, w_in, layer, q_gain, k_gain, *, heads, tiles):
    width = heads * 2 * B_HEAD_DIM
    tn = tiles["proj_tn"]
    lane_group = np.arange(tn) // B_HEAD_DIM
    group_ones = jnp.asarray(lane_group[:, None] == lane_group[None, :], BF16)
    reps = width // B_HEAD_DIM
    gq = jnp.tile(q_gain.astype(F32), reps).reshape(1, width)
    gk = jnp.tile(k_gain.astype(F32), reps).reshape(1, width)
    return seg_matmul(xn, w_in, layer, (0, width, 2 * width), width, _headnorm_epilogue,
                      (F32, F32, F32), vecs=(gq, gk), consts=(group_ones,),
                      tm=tiles["proj_tm"], tn=tn, name="b_in_proj")


def _mixer_c_proj(xn, w_in, layer, lower_bound, *, tiles):
    width = w_in.shape[2] // 4
    return seg_matmul(xn, w_in, layer, (0, width, 2 * width, 3 * width), width, _hgrn_gate_epilogue,
                      (F32,) * 5, vecs=(lower_bound.reshape(1, width),),
                      tm=tiles["proj_tm"], tn=tiles["proj_tn"], name="c_in_proj")


def kernel(x_prompt, x_sample, cache_k, cache_v, page_table, state_hgrn, norm_mix, norm_ffn, ffn_w_gu, ffn_w_down, a_w_in, a_v_norm, a_w_s, a_b_s, a_w_out, b_w_in, b_q_norm, b_k_norm, b_lambda_q1, b_lambda_k1, b_lambda_q2, b_lambda_k2, b_subln, b_w_out, c_w_in, c_g_norm, c_lower_bounds, c_w_out):
    batch, seq, d_model = x_prompt.shape
    dec_b, dec_seq, _ = x_sample.shape
    depth = norm_mix.shape[0]
    b_heads = d_model // (2 * B_HEAD_DIM)
    c_heads = d_model // C_HEAD_DIM
    mp, ms = batch * seq, dec_b * dec_seq
    tp, ts = _tiles(mp), _tiles(ms)

    probs = jax.nn.softmax(c_lower_bounds.astype(F32), axis=0)
    lower_bound = jnp.cumsum(probs, axis=0) - probs[0]

    h_p = x_prompt.reshape(mp, d_model)
    h_s = x_sample.reshape(ms, d_model)
    xn_p = norm_rows(h_p, norm_mix[0], tm=tp["norm_tm"])
    xn_s = norm_rows(h_s, norm_mix[0], tm=ts["norm_tm"])

    n_phys = cache_k.shape[1]
    cache_k2 = cache_k.reshape(cache_k.shape[0], n_phys, PAGE_SIZE * b_heads, 2 * B_HEAD_DIM)
    cache_v2 = cache_v.reshape(cache_v.shape[0], n_phys, PAGE_SIZE * b_heads, B_V_DIM)

    k_p_rows, v_p_rows, k_s_rows, v_s_rows = [], [], [], []
    hgrn_p, hgrn_s, chunk_v_s = [], [], []
    for i in range(depth):
        kind, j = i % 3, i // 3
        if kind == 0:
            y_p, _ = _mixer_a(xn_p, a_w_in, j, a_v_norm[j], a_w_s[j], a_b_s[j],
                              chunk_len=A_CHUNK, n_seq=batch, tiles=tp, vn_dtype=BF16)
            y_s, vn_s = _mixer_a(xn_s, a_w_in, j, a_v_norm[j], a_w_s[j], a_b_s[j],
                                 chunk_len=dec_seq, n_seq=dec_b, tiles=ts, vn_dtype=F32)
            chunk_v_s.append(vn_s.reshape(dec_b, dec_seq, -1))
            w_out = a_w_out
        elif kind == 1:
            lam_init = 0.8 - 0.6 * math.exp(-0.3 * i)
            lam = (jnp.exp(jnp.sum(b_lambda_q1[j].astype(F32) * b_lambda_k1[j].astype(F32)))
                   - jnp.exp(jnp.sum(b_lambda_q2[j].astype(F32) * b_lambda_k2[j].astype(F32)))
                   + lam_init)
            q_p, k_p, v_p = _mixer_b_proj(xn_p, b_w_in, j, b_q_norm[j], b_k_norm[j],
                                          heads=b_heads, tiles=tp)
            y_p = diff_attn_prompt(q_p, k_p, v_p, lam, b_subln[j], lam_init,
                                   batch=batch, seq=seq, heads=b_heads, tq=tp["attn_tq"],
                                   hb=tp["attn_heads"])
            k_p_rows.append(k_p.reshape(batch, seq, b_heads, 2 * B_HEAD_DIM))
            v_p_rows.append(v_p.reshape(batch, seq, b_heads, B_V_DIM))

            q_s, k_s, v_s = _mixer_b_proj(xn_s, b_w_in, j, b_q_norm[j], b_k_norm[j],
                                          heads=b_heads, tiles=ts)
            q5 = (q_s * (B_HEAD_DIM ** -0.5)).reshape(dec_b, dec_seq, b_heads, 2, B_HEAD_DIM)
            qt = jnp.einsum("bthcd,ce->bhcted", q5, jnp.eye(2, dtype=F32))
            qt = qt.reshape(dec_b, b_heads // SUBLANES, SUBLANES * 2 * dec_seq, 2 * B_HEAD_DIM)
            qt = qt.transpose(0, 1, 3, 2)
            k_new = k_s.reshape(dec_b, dec_seq * b_heads, 2 * B_HEAD_DIM)
            v_new = v_s.reshape(dec_b, dec_seq * b_heads, B_V_DIM)
            o_s = diff_attn_decode(qt, cache_k2, cache_v2, page_table, k_new, v_new, lam,
                                   b_subln[j], lam_init, layer=j, heads=b_heads, n_q=dec_seq,
                                   pages=4)
            y_s = o_s.transpose(0, 2, 1, 3).reshape(ms, d_model)
            k_s_rows.append(k_s.reshape(dec_b, dec_seq, b_heads, 2 * B_HEAD_DIM))
            v_s_rows.append(v_s.reshape(dec_b, dec_seq, b_heads, B_V_DIM))
            w_out = b_w_out
        else:
            qkv_p = _mixer_c_proj(xn_p, c_w_in, j, lower_bound[i], tiles=tp)
            y_p, st_p = hgrn_recurrence(*qkv_p, c_g_norm[j], None, batch=batch, seq=seq,
                                        heads=c_heads, rows=tp["hgrn_rows"], chunk=C_HEAD_DIM,
                                        hb=tp["hgrn_heads"], out_dtype=BF16)
            qkv_s = _mixer_c_proj(xn_s, c_w_in, j, lower_bound[i], tiles=ts)
            y_s, st_s = hgrn_recurrence(*qkv_s, c_g_norm[j], state_hgrn[j], batch=dec_b,
                                        seq=dec_seq, heads=c_heads, rows=dec_seq, chunk=dec_seq,
                                        hb=ts["hgrn_heads"], out_dtype=F32)
            hgrn_p.append(st_p)
            hgrn_s.append(st_s)
            w_out = c_w_out
        h_p, xf_p = out_proj(y_p, w_out, j, h_p, norm_ffn[i], tm=tp["out_tm"])
        h_s, xf_s = out_proj(y_s, w_out, j, h_s, norm_ffn[i], tm=ts["out_tm"])
        gain_next = norm_mix[i + 1] if i + 1 < depth else None
        h_p, h_s, xn_p, xn_s = ffn(xf_p, h_p, xf_s, h_s, ffn_w_gu, ffn_w_down, i, gain_next,
                                   tm=tp["ffn_tm"], th=tp["ffn_th"])
    return (h_p.reshape(batch, seq, d_model), h_s.reshape(dec_b, dec_seq, d_model),
            jnp.stack(k_p_rows), jnp.stack(v_p_rows), jnp.stack(k_s_rows), jnp.stack(v_s_rows),
            jnp.stack(hgrn_p), jnp.stack(hgrn_s), jnp.stack(chunk_v_s))
```

```python
import functools
import math

import jax
import jax.numpy as jnp
import numpy as np
from jax import lax
from jax.experimental import pallas as pl
from jax.experimental.pallas import tpu as pltpu

F32 = jnp.float32
BF16 = jnp.bfloat16
EPS = 1e-6

LANES = 128
SUBLANES = 8
VMEM_LIMIT_BYTES = 56 << 20
FFN_VMEM_LIMIT_BYTES = 60 << 20

A_CHUNK = 128
A_GROUPS = 8
B_HEAD_DIM = 64
B_V_DIM = 2 * B_HEAD_DIM
C_HEAD_DIM = 128
PAGE_SIZE = 128
HGRN_SUB = SUBLANES
DECODE_PAGES_PER_STEP = 8


def _cparams(n_axes):
    return pltpu.CompilerParams(
        dimension_semantics=("arbitrary",) * n_axes,
        vmem_limit_bytes=VMEM_LIMIT_BYTES,
    )


def _rmsnorm_f32(x, gain):
    return x * lax.rsqrt(jnp.mean(x * x, axis=-1, keepdims=True) + EPS) * gain


def _sigmoid(x):
    return 1.0 / (1.0 + jnp.exp(-x))


def _norm_rows_body(x_ref, g_ref, o_ref):
    o_ref[...] = _rmsnorm_f32(x_ref[...], g_ref[...]).astype(o_ref.dtype)


def norm_rows(x, gain, *, tm):
    m, d = x.shape
    return pl.pallas_call(
        _norm_rows_body,
        grid=(m // tm,),
        in_specs=[pl.BlockSpec((tm, d), lambda i: (i, 0)),
                  pl.BlockSpec((1, d), lambda i: (0, 0))],
        out_specs=pl.BlockSpec((tm, d), lambda i: (i, 0)),
        out_shape=jax.ShapeDtypeStruct((m, d), BF16),
        compiler_params=_cparams(1),
        name="norm_rows",
    )(x, gain.reshape(1, d))


def _seg_matmul_body(*refs, ns, nv, nc, no, epilogue):
    x_ref, xs_ref = refs[0:2]
    w_refs = refs[2:2 + ns]
    vec_refs = refs[2 + ns:2 + ns + nv]
    const_refs = refs[2 + ns + nv:2 + ns + nv + nc]
    out_refs = refs[2 + ns + nv + nc:2 + ns + nv + nc + no]
    sample_out_refs = refs[2 + ns + nv + nc + no:-1]
    wb_ref = refs[-1]
    first_row_tile = pl.program_id(1) == 0

    @pl.when(first_row_tile)
    def _():
        for s in range(ns):
            wb_ref[s] = w_refs[s][...].astype(BF16)

    def project(rows_ref, dst_refs):
        x = rows_ref[...]
        accs = [jnp.dot(x, wb_ref[s], preferred_element_type=F32) for s in range(ns)]
        outs = epilogue(accs, [r[...] for r in vec_refs], [r[...] for r in const_refs])
        for r, o in zip(dst_refs, outs):
            r[...] = o.astype(r.dtype)

    project(x_ref, out_refs)

    @pl.when(first_row_tile)
    def _():
        project(xs_ref, sample_out_refs)


def seg_matmul(x, xs, w, layer, seg_starts, seg_width, epilogue, out_dtypes, vecs=(), consts=(),
               *, tm, tn, name):
    m, k = x.shape
    ms = xs.shape[0]
    ns = len(seg_starts)
    no = len(out_dtypes)
    in_specs = [pl.BlockSpec((tm, k), lambda j, i: (i, 0)),
                pl.BlockSpec((ms, k), lambda j, i: (0, 0))]
    for st in seg_starts:
        in_specs.append(pl.BlockSpec((None, k, tn), lambda j, i, off=st // tn: (layer, 0, off + j)))
    for _ in vecs:
        in_specs.append(pl.BlockSpec((1, tn), lambda j, i: (0, j)))
    for c in consts:
        in_specs.append(pl.BlockSpec(c.shape, lambda j, i, nd=c.ndim: (0,) * nd))
    res = pl.pallas_call(
        functools.partial(_seg_matmul_body, ns=ns, nv=len(vecs), nc=len(consts), no=no,
                          epilogue=epilogue),
        grid=(seg_width // tn, m // tm),
        in_specs=in_specs,
        out_specs=([pl.BlockSpec((tm, tn), lambda j, i: (i, j)) for _ in out_dtypes]
                   + [pl.BlockSpec((ms, tn), lambda j, i: (0, j)) for _ in out_dtypes]),
        out_shape=([jax.ShapeDtypeStruct((m, seg_width), dt) for dt in out_dtypes]
                   + [jax.ShapeDtypeStruct((ms, seg_width), dt) for dt in out_dtypes]),
        scratch_shapes=[pltpu.VMEM((ns, k, tn), BF16)],
        compiler_params=_cparams(2),
        name=name,
    )(x, xs, *([w] * ns), *vecs, *consts)
    return res[:no], res[no:]


def _gelu_exact_f32(a):
    z = a * (2.0 ** -0.5)
    az = jnp.abs(z)
    t = 1.0 / (1.0 + 0.3275911 * az)
    poly = t * (0.254829592 + t * (-0.284496736 + t * (1.421413741
                                                       + t * (-1.453152027 + t * 1.061405429))))
    erfc_abs = poly * jnp.exp(-az * az)
    return 0.5 * a * jnp.where(z >= 0, 2.0 - erfc_abs, erfc_abs)


def _gelu_epilogue(accs, vecs, consts):
    (a,) = accs
    return [_gelu_exact_f32(a)]


def _headnorm_epilogue(accs, vecs, consts):
    aq, ak, av = accs
    gq, gk = vecs
    (group_ones,) = consts

    def head_norm(a, g):
        sq = a * a
        hi = sq.astype(BF16)
        lo = (sq - hi.astype(F32)).astype(BF16)
        ms = (jnp.dot(hi, group_ones, preferred_element_type=F32)
              + jnp.dot(lo, group_ones, preferred_element_type=F32)) * (1.0 / B_HEAD_DIM)
        return a * lax.rsqrt(ms + EPS) * g

    return [head_norm(aq, gq), head_norm(ak, gk), av]


def _hgrn_gate_epilogue(accs, vecs, consts):
    aq, af, av, ag = accs
    (lb,) = vecs
    q = aq * _sigmoid(aq)
    f = lb + (1.0 - lb) * _sigmoid(af)
    k = (1.0 - lb) * _sigmoid(-af)
    return [q, k, jnp.log(f), av, ag]


def _out_proj_body(y_ref, ys_ref, w_ref, h_ref, hs_ref, g_ref, ho_ref, xo_ref, hso_ref, xso_ref,
                   wb_ref, *, cast_rows):
    first_row_tile = pl.program_id(0) == 0

    @pl.when(first_row_tile)
    def _():
        def cast(r, carry):
            sl = pl.ds(pl.multiple_of(r * cast_rows, cast_rows), cast_rows)
            wb_ref[sl, :] = w_ref[sl, :].astype(BF16)
            return carry
        lax.fori_loop(0, w_ref.shape[0] // cast_rows, cast, 0)

    def project(rows_ref, res_ref, h_out_ref, x_out_ref):
        hn = res_ref[...] + jnp.dot(rows_ref[...].astype(BF16), wb_ref[...],
                                    preferred_element_type=F32)
        h_out_ref[...] = hn
        x_out_ref[...] = _rmsnorm_f32(hn, g_ref[...]).astype(x_out_ref.dtype)

    project(y_ref, h_ref, ho_ref, xo_ref)

    @pl.when(first_row_tile)
    def _():
        project(ys_ref, hs_ref, hso_ref, xso_ref)


def out_proj(y, ys, w, layer, h, hs, gain_next, *, tm):
    m, k = y.shape
    ms = ys.shape[0]
    n = w.shape[2]
    row = lambda cols: pl.BlockSpec((tm, cols), lambda i: (i, 0))
    sample = lambda cols: pl.BlockSpec((ms, cols), lambda i: (0, 0))
    return pl.pallas_call(
        functools.partial(_out_proj_body, cast_rows=256),
        grid=(m // tm,),
        in_specs=[row(k), sample(k),
                  pl.BlockSpec((None, k, n), lambda i: (layer, 0, 0), pipeline_mode=pl.Buffered(1)),
                  row(n), sample(n),
                  pl.BlockSpec((1, n), lambda i: (0, 0))],
        out_specs=[row(n), row(n), sample(n), sample(n)],
        out_shape=[jax.ShapeDtypeStruct((m, n), F32), jax.ShapeDtypeStruct((m, n), BF16),
                   jax.ShapeDtypeStruct((ms, n), F32), jax.ShapeDtypeStruct((ms, n), BF16)],
        scratch_shapes=[pltpu.VMEM((k, n), BF16)],
        compiler_params=_cparams(1),
        name="out_proj",
    )(y, ys, w, h, hs, gain_next.reshape(1, n))


def _ffn_body(x_ref, h_ref, xs_ref, hs_ref, wg_ref, wu_ref, wd_ref, g_ref, *out_refs, n_t, with_norm):
    if with_norm:
        ho_ref, hso_ref, xo_ref, xso_ref = out_refs
    else:
        ho_ref, hso_ref = out_refs
    i = pl.program_id(0)
    t = pl.program_id(1)

    @pl.when(t == 0)
    def _():
        ho_ref[...] = h_ref[...]

    def swiglu(x):
        gate = jnp.dot(x, wg_ref[...].astype(BF16), preferred_element_type=F32)
        up = jnp.dot(x, wu_ref[...].astype(BF16), preferred_element_type=F32)
        act = (gate * _sigmoid(gate) * up).astype(BF16)
        return jnp.dot(act, wd_ref[...].astype(BF16), preferred_element_type=F32)

    ho_ref[...] += swiglu(x_ref[...])

    @pl.when((i == 0) & (t == 0))
    def _():
        hso_ref[...] = hs_ref[...]

    @pl.when(i == 0)
    def _():
        hso_ref[...] += swiglu(xs_ref[...])

    if with_norm:
        @pl.when(t == n_t - 1)
        def _():
            xo_ref[...] = _rmsnorm_f32(ho_ref[...], g_ref[...]).astype(BF16)

        @pl.when((i == 0) & (t == n_t - 1))
        def _():
            xso_ref[...] = _rmsnorm_f32(hso_ref[...], g_ref[...]).astype(BF16)


def ffn(x, h, xs, hs, w_gu, w_down, layer, gain_next, *, tm, th):
    m, d = x.shape
    ms = xs.shape[0]
    hidden = w_down.shape[1]
    n_t = hidden // th
    with_norm = gain_next is not None
    gain = gain_next if with_norm else jnp.ones((d,), F32)
    row_out = pl.BlockSpec((tm, d), lambda i, t: (i, 0))
    sample_block = pl.BlockSpec((ms, d), lambda i, t: (0, 0))
    out_specs = [row_out, sample_block]
    out_shape = [jax.ShapeDtypeStruct((m, d), F32), jax.ShapeDtypeStruct((ms, d), F32)]
    if with_norm:
        out_specs += [row_out, sample_block]
        out_shape += [jax.ShapeDtypeStruct((m, d), BF16), jax.ShapeDtypeStruct((ms, d), BF16)]
    row_block = pl.BlockSpec((tm, d), lambda i, t: (i, 0), pipeline_mode=pl.Buffered(1))
    res = pl.pallas_call(
        functools.partial(_ffn_body, n_t=n_t, with_norm=with_norm),
        grid=(m // tm, n_t),
        in_specs=[row_block, row_block, sample_block, sample_block,
                  pl.BlockSpec((None, d, th), lambda i, t: (layer, 0, t)),
                  pl.BlockSpec((None, d, th), lambda i, t: (layer, 0, n_t + t)),
                  pl.BlockSpec((None, th, d), lambda i, t: (layer, t, 0)),
                  pl.BlockSpec((1, d), lambda i, t: (0, 0))],
        out_specs=out_specs,
        out_shape=out_shape,
        compiler_params=pltpu.CompilerParams(dimension_semantics=("arbitrary",) * 2,
                                             vmem_limit_bytes=FFN_VMEM_LIMIT_BYTES),
        name="ffn",
    )(x, h, xs, hs, w_gu, w_gu, w_down, gain.reshape(1, d))
    return tuple(res) if with_norm else (res[0], res[1], None, None)


def _spatial_body(u_ref, v_ref, wm_ref, bs_ref, vg_ref, p_ref, vn_ref, *, chunk, groups):
    v = v_ref[...].astype(F32)
    vn = _rmsnorm_f32(v, vg_ref[...])
    vn_ref[...] = vn.astype(vn_ref.dtype)
    vnb = vn.astype(BF16)
    rows, width = v.shape
    gw = width // groups
    for c in range(rows // chunk):
        r0 = c * chunk
        for g in range(groups):
            c0 = g * gw
            s = jnp.dot(wm_ref[g], vnb[r0:r0 + chunk, c0:c0 + gw], preferred_element_type=F32)
            s = s + bs_ref[g]
            u = u_ref[r0:r0 + chunk, c0:c0 + gw].astype(F32)
            p_ref[r0:r0 + chunk, c0:c0 + gw] = (u * s).astype(p_ref.dtype)


def spatial_mix(uv, wm, bs, v_gain, *, chunk, tm, vn_dtype):
    m, w2 = uv.shape
    width = w2 // 2
    groups = wm.shape[0]
    return pl.pallas_call(
        functools.partial(_spatial_body, chunk=chunk, groups=groups),
        grid=(m // tm,),
        in_specs=[pl.BlockSpec((tm, width), lambda i: (i, 0)),
                  pl.BlockSpec((tm, width), lambda i: (i, 1)),
                  pl.BlockSpec(wm.shape, lambda i: (0, 0, 0)),
                  pl.BlockSpec(bs.shape, lambda i: (0, 0, 0)),
                  pl.BlockSpec((1, width), lambda i: (0, 0))],
        out_specs=[pl.BlockSpec((tm, width), lambda i: (i, 0)),
                   pl.BlockSpec((tm, width), lambda i: (i, 0))],
        out_shape=[jax.ShapeDtypeStruct((m, width), BF16),
                   jax.ShapeDtypeStruct((m, width), vn_dtype)],
        compiler_params=_cparams(1),
        name="spatial_mix",
    )(uv, uv, wm, bs, v_gain.reshape(1, width))


def _diff_attn_body(qt_tab, kt_tab, q_ref, k_ref, v_ref, lam_ref, sub_ref, o_ref,
                    qs_ref, m_ref, l_ref, acc_ref, *, tq, hb, out_scale):
    t = pl.program_id(2)
    qi = qt_tab[t]
    ki = kt_tab[t]
    hd = 2 * B_HEAD_DIM

    @pl.when(ki == 0)
    def _():
        for h in range(hb):
            q = q_ref[:, h * hd:(h + 1) * hd].astype(F32) * (B_HEAD_DIM ** -0.5)
            lane = lax.broadcasted_iota(jnp.int32, q.shape, 1)
            qs_ref[h, 0:tq, :] = jnp.where(lane < B_HEAD_DIM, q, 0.0)
            qs_ref[h, tq:2 * tq, :] = jnp.where(lane >= B_HEAD_DIM, q, 0.0)
        m_ref[...] = jnp.full(m_ref.shape, -jnp.inf, F32)
        l_ref[...] = jnp.zeros(l_ref.shape, F32)
        acc_ref[...] = jnp.zeros(acc_ref.shape, F32)

    def strip(h, r0, masked):
        rs = pl.ds(r0, tq)
        keys = k_ref[:, h * hd:(h + 1) * hd]
        vals = v_ref[:, h * B_V_DIM:(h + 1) * B_V_DIM]
        s = lax.dot_general(qs_ref[h, rs, :], keys, (((1,), (1,)), ((), ())),
                            preferred_element_type=F32)
        if masked:
            row = lax.broadcasted_iota(jnp.int32, s.shape, 0)
            col = lax.broadcasted_iota(jnp.int32, s.shape, 1)
            s = jnp.where(col <= row, s, -jnp.inf)
        m_prev = m_ref[h, rs, :]
        m_new = jnp.maximum(m_prev, jnp.max(s, axis=1, keepdims=True))
        alpha = jnp.exp(m_prev - m_new)
        p = jnp.exp(s - m_new[:, 0:1])
        l_ref[h, rs, :] = alpha * l_ref[h, rs, :] + jnp.sum(p, axis=1, keepdims=True)
        acc_ref[h, rs, :] = alpha * acc_ref[h, rs, :] + jnp.dot(p, vals, preferred_element_type=F32)
        m_ref[h, rs, :] = m_new

    @pl.when(ki < qi)
    def _():
        for h in range(hb):
            strip(h, 0, False)
            strip(h, tq, False)

    @pl.when(ki == qi)
    def _():
        for h in range(hb):
            strip(h, 0, True)
            strip(h, tq, True)
            o = acc_ref[h] / l_ref[h]
            d = o[0:tq] - lam_ref[0] * o[tq:2 * tq]
            o_ref[:, h * B_V_DIM:(h + 1) * B_V_DIM] = (
                _rmsnorm_f32(d, sub_ref[...]) * out_scale).astype(o_ref.dtype)


def diff_attn_prompt(q, k, v, lam, subln, lam_init, *, batch, seq, heads, tq, hb):
    nq = seq // tq
    tri = [(qi, ki) for qi in range(nq) for ki in range(qi + 1)]
    qt_tab = jnp.asarray([a for a, _ in tri], jnp.int32)
    kt_tab = jnp.asarray([b for _, b in tri], jnp.int32)
    hd = 2 * B_HEAD_DIM
    grid_spec = pltpu.PrefetchScalarGridSpec(
        num_scalar_prefetch=2,
        grid=(batch, heads // hb, len(tri)),
        in_specs=[pl.BlockSpec((tq, hb * hd), lambda b, h, t, qt, kt: (b * nq + qt[t], h)),
                  pl.BlockSpec((tq, hb * hd), lambda b, h, t, qt, kt: (b * nq + kt[t], h)),
                  pl.BlockSpec((tq, hb * B_V_DIM), lambda b, h, t, qt, kt: (b * nq + kt[t], h)),
                  pl.BlockSpec(memory_space=pltpu.SMEM),
                  pl.BlockSpec((1, B_V_DIM), lambda b, h, t, qt, kt: (0, 0))],
        out_specs=pl.BlockSpec((tq, hb * B_V_DIM), lambda b, h, t, qt, kt: (b * nq + qt[t], h)),
        scratch_shapes=[pltpu.VMEM((hb, 2 * tq, hd), F32),
                        pltpu.VMEM((hb, 2 * tq, LANES), F32),
                        pltpu.VMEM((hb, 2 * tq, LANES), F32),
                        pltpu.VMEM((hb, 2 * tq, B_V_DIM), F32)],
    )
    return pl.pallas_call(
        functools.partial(_diff_attn_body, tq=tq, hb=hb, out_scale=1.0 - lam_init),
        grid_spec=grid_spec,
        out_shape=jax.ShapeDtypeStruct((batch * seq, heads * B_V_DIM), BF16),
        compiler_params=_cparams(3),
        name="diff_attn_prompt",
    )(qt_tab, kt_tab, q, k, v, lam.reshape(1), subln.reshape(1, B_V_DIM))


def _decode_attn_body(pt_ref, *refs, pages, heads, n_q, n_groups, out_scale):
    k_refs = refs[:pages]
    v_refs = refs[pages:2 * pages]
    kn_ref, vn_ref, qt_ref, lam_ref, sub_ref, o_ref, m_ref, l_ref, acc_ref = refs[2 * pages:]
    g = pl.program_id(1)
    hg = heads // SUBLANES
    hc = 2 * n_q
    cols = SUBLANES * hc

    @pl.when(g == 0)
    def _():
        m_ref[...] = jnp.full(m_ref.shape, -jnp.inf, F32)
        l_ref[...] = jnp.zeros(l_ref.shape, F32)
        acc_ref[...] = jnp.zeros(acc_ref.shape, F32)

    sub = lax.broadcasted_iota(jnp.int32, (SUBLANES, cols), 0)
    lane = lax.broadcasted_iota(jnp.int32, (SUBLANES, cols), 1)
    own = sub == lane // hc

    def to_column(x8):
        r = jnp.sum(jnp.where(own, x8, 0.0), axis=0, keepdims=True)
        return jnp.broadcast_to(r, (LANES, cols)).T

    def group_rows(ref, j, n_pos):
        x = ref[0:n_pos * heads, :].reshape(n_pos, hg, SUBLANES, 2 * B_HEAD_DIM)
        return x[:, j].reshape(n_pos * SUBLANES, 2 * B_HEAD_DIM)

    def process(page_k_refs, page_v_refs, n_pos, new_tokens):
        for j in range(hg):
            scores = []
            for k_ref in page_k_refs:
                s = jnp.dot(group_rows(k_ref, j, n_pos), qt_ref[j], preferred_element_type=F32)
                s = s.reshape(n_pos, SUBLANES, cols)
                valid = own[None]
                if new_tokens:
                    pos = lax.broadcasted_iota(jnp.int32, s.shape, 0)
                    qry = lax.broadcasted_iota(jnp.int32, s.shape, 2) % n_q
                    valid = valid & (pos <= qry)
                scores.append(jnp.where(valid, s, -jnp.inf))
            m_prev = m_ref[j]
            m_new = m_prev
            for s in scores:
                m_new = jnp.maximum(m_new, jnp.max(s, axis=0))
            m_safe = jnp.where(own, m_new, 0.0)
            alpha = jnp.exp(m_prev - m_safe)
            l_new = alpha * l_ref[j]
            pv = jnp.zeros((cols, B_V_DIM), F32)
            for s, v_ref in zip(scores, page_v_refs):
                p = jnp.exp(s - m_safe[None])
                l_new = l_new + jnp.sum(p, axis=0)
                pv = pv + lax.dot_general(p.reshape(n_pos * SUBLANES, cols), group_rows(v_ref, j, n_pos),
                                          (((0,), (0,)), ((), ())), preferred_element_type=F32)
            acc_ref[j] = acc_ref[j] * to_column(alpha) + pv
            l_ref[j] = l_new
            m_ref[j] = m_new

    process(k_refs, v_refs, PAGE_SIZE, False)

    @pl.when(g == n_groups - 1)
    def _():
        process([kn_ref], [vn_ref], n_q, True)
        for j in range(hg):
            o = acc_ref[j] / to_column(l_ref[j])
            o = o.reshape(SUBLANES, 2, n_q, B_V_DIM)
            d = o[:, 0] - lam_ref[0] * o[:, 1]
            d = d * lax.rsqrt(jnp.mean(d * d, axis=-1, keepdims=True) + EPS) * sub_ref[...]
            o_ref[j * SUBLANES:(j + 1) * SUBLANES] = (d * out_scale).astype(o_ref.dtype)


def diff_attn_decode(qt, cache_k, cache_v, page_table, k_new, v_new, lam, subln, lam_init,
                     *, layer, heads, n_q, pages):
    dec_b, n_pages = page_table.shape
    hd = 2 * B_HEAD_DIM
    rows = PAGE_SIZE * heads
    hg = heads // SUBLANES
    cols = SUBLANES * 2 * n_q
    n_groups = n_pages // pages
    page_spec = lambda p_i: pl.BlockSpec(
        (None, None, rows, hd),
        lambda b, g, pt, p_i=p_i: (layer, pt[b, g * pages + p_i], 0, 0))
    grid_spec = pltpu.PrefetchScalarGridSpec(
        num_scalar_prefetch=1,
        grid=(dec_b, n_groups),
        in_specs=([page_spec(p_i) for p_i in range(pages)] * 2
                  + [pl.BlockSpec((None, n_q * heads, hd), lambda b, g, pt: (b, 0, 0)),
                     pl.BlockSpec((None, n_q * heads, hd), lambda b, g, pt: (b, 0, 0)),
                     pl.BlockSpec((None, hg, hd, cols), lambda b, g, pt: (b, 0, 0, 0)),
                     pl.BlockSpec(memory_space=pltpu.SMEM),
                     pl.BlockSpec((1, B_V_DIM), lambda b, g, pt: (0, 0))]),
        out_specs=pl.BlockSpec((None, heads, n_q, B_V_DIM), lambda b, g, pt: (b, 0, 0, 0)),
        scratch_shapes=[pltpu.VMEM((hg, SUBLANES, cols), F32),
                        pltpu.VMEM((hg, SUBLANES, cols), F32),
                        pltpu.VMEM((hg, cols, B_V_DIM), F32)],
    )
    return pl.pallas_call(
        functools.partial(_decode_attn_body, pages=pages, heads=heads, n_q=n_q,
                          n_groups=n_groups, out_scale=1.0 - lam_init),
        grid_spec=grid_spec,
        out_shape=jax.ShapeDtypeStruct((dec_b, heads, n_q, B_V_DIM), F32),
        compiler_params=_cparams(2),
        name="diff_attn_decode",
    )(page_table, *([cache_k] * pages), *([cache_v] * pages), k_new, v_new, qt,
      lam.reshape(1), subln.reshape(1, B_V_DIM))


def _cumsum_rows(x):
    c = x.shape[0]
    sub = lax.broadcasted_iota(jnp.int32, x.shape, 0) % HGRN_SUB
    d = 1
    while d < HGRN_SUB:
        x = x + jnp.where(sub >= d, pltpu.roll(x, d, axis=0), 0.0)
        d *= 2
    blocks = []
    carry = None
    for j in range(c // HGRN_SUB):
        blk = x[j * HGRN_SUB:(j + 1) * HGRN_SUB]
        if carry is not None:
            blk = blk + carry
        blocks.append(blk)
        carry = blk[HGRN_SUB - 1:HGRN_SUB]
    return jnp.concatenate(blocks, axis=0) if len(blocks) > 1 else blocks[0]


def _hgrn_chunk(q, k, lf, v, st):
    c = q.shape[0]
    nb = c // HGRN_SUB
    gcum = _cumsum_rows(lf)
    o = lax.dot_general(q * jnp.exp(gcum), st, (((1,), (1,)), ((), ())), preferred_element_type=F32)

    if nb > 1:
        row = lax.broadcasted_iota(jnp.int32, (c, C_HEAD_DIM), 0)
        t_idx = lax.broadcasted_iota(jnp.int32, (c, c), 0)
        s_idx = lax.broadcasted_iota(jnp.int32, (c, c), 1)
        a_off = None
        size = 2 * HGRN_SUB
        while size <= c:
            half = size // 2
            if size < c:
                g_mid = jnp.concatenate(
                    [jnp.broadcast_to(gcum[b0 + half - 1:b0 + half], (size, C_HEAD_DIM))
                     for b0 in range(0, c, size)], axis=0)
            else:
                g_mid = gcum[half - 1:half]
            upper = (row % size) >= half
            qd = q * jnp.exp(jnp.where(upper, gcum - g_mid, -jnp.inf))
            kd = k * jnp.exp(jnp.where(upper, -jnp.inf, g_mid - gcum))
            a = lax.dot_general(qd, kd, (((1,), (1,)), ((), ())), preferred_element_type=F32)
            if size < c:
                a = jnp.where((t_idx // size) == (s_idx // size), a, 0.0)
            a_off = a if a_off is None else a_off + a
            size *= 2
        o = o + jnp.dot(a_off, v, preferred_element_type=F32)

    sub_row = lax.broadcasted_iota(jnp.int32, (HGRN_SUB, C_HEAD_DIM), 0)
    o_blocks = []
    for i in range(nb):
        r0 = i * HGRN_SUB
        gi = gcum[r0:r0 + HGRN_SUB]
        qi = q[r0:r0 + HGRN_SUB]
        ki = k[r0:r0 + HGRN_SUB]
        vi = v[r0:r0 + HGRN_SUB]
        oi = o[r0:r0 + HGRN_SUB]
        for s in range(HGRN_SUB):
            dec = jnp.exp(jnp.where(sub_row >= s, gi - gi[s:s + 1], -jnp.inf))
            a_col = jnp.sum(qi * ki[s:s + 1] * dec, axis=-1, keepdims=True)
            oi = oi + a_col * vi[s:s + 1]
        o_blocks.append(oi)
    o = jnp.concatenate(o_blocks, axis=0) if nb > 1 else o_blocks[0]
    g_last = gcum[c - 1:c]
    kd = k * jnp.exp(g_last - gcum)
    if c < C_HEAD_DIM:
        pad = jnp.zeros((C_HEAD_DIM - c, C_HEAD_DIM), F32)
        kd = jnp.concatenate([kd, pad], axis=0)
        v = jnp.concatenate([v, pad], axis=0)
    st_new = st * jnp.exp(g_last) + jnp.dot(v.T, kd, preferred_element_type=F32)
    return o, st_new


def _hgrn_body(*refs, chunk, n_chunks, n_r, hb, with_state):
    if with_state:
        q_ref, k_ref, lf_ref, v_ref, g_ref, gg_ref, s0_ref, o_ref, so_ref, st_ref = refs
    else:
        q_ref, k_ref, lf_ref, v_ref, g_ref, gg_ref, o_ref, so_ref, st_ref = refs
    r = pl.program_id(2)
    hd = C_HEAD_DIM

    @pl.when(r == 0)
    def _():
        for h in range(hb):
            if with_state:
                st_ref[h] = s0_ref[h].astype(F32).T
            else:
                st_ref[h] = jnp.zeros((hd, hd), F32)

    def step(ci, carry):
        sl = pl.ds(pl.multiple_of(ci * chunk, chunk), chunk)
        for h in range(hb):
            cs = slice(h * hd, (h + 1) * hd)
            o, st_new = _hgrn_chunk(q_ref[sl, cs], k_ref[sl, cs], lf_ref[sl, cs], v_ref[sl, cs],
                                    st_ref[h])
            st_ref[h] = st_new
            gate = g_ref[sl, cs]
            o = _rmsnorm_f32(o, gg_ref[...]) * (gate * _sigmoid(gate))
            o_ref[sl, cs] = o.astype(o_ref.dtype)
        return carry

    lax.fori_loop(0, n_chunks, step, 0)

    @pl.when(r == n_r - 1)
    def _():
        for h in range(hb):
            so_ref[h] = st_ref[h].T.astype(so_ref.dtype)


def hgrn_recurrence(q, k, lf, v, g, g_gain, state0, *, batch, seq, heads, rows, chunk, hb, out_dtype):
    n_r = seq // rows
    hd = C_HEAD_DIM
    with_state = state0 is not None
    row_spec = pl.BlockSpec((rows, hb * hd), lambda b, h, r: (b * n_r + r, h))
    state_spec = pl.BlockSpec((None, hb, hd, hd), lambda b, h, r: (b, h, 0, 0))
    in_specs = [row_spec] * 5 + [pl.BlockSpec((1, hd), lambda b, h, r: (0, 0))]
    args = [q, k, lf, v, g, g_gain.reshape(1, hd)]
    if with_state:
        in_specs.append(state_spec)
        args.append(state0)
    return pl.pallas_call(
        functools.partial(_hgrn_body, chunk=chunk, n_chunks=rows // chunk, n_r=n_r, hb=hb,
                          with_state=with_state),
        grid=(batch, heads // hb, n_r),
        in_specs=in_specs,
        out_specs=[row_spec, state_spec],
        out_shape=[jax.ShapeDtypeStruct((batch * seq, heads * hd), out_dtype),
                   jax.ShapeDtypeStruct((batch, heads, hd, hd), F32)],
        scratch_shapes=[pltpu.VMEM((hb, hd, hd), F32)],
        compiler_params=_cparams(3),
        name="hgrn_recurrence",
    )(*args)


def _tiles(m):
    big = m >= 1024
    return dict(
        norm_tm=512 if big else m,
        proj_tm=1024 if big else m,
        proj_tn=256,
        a_proj_tn=1024,
        out_tm=512 if big else m,
        ffn_tm=1024 if big else m,
        ffn_th=256,
        spatial_tm=256 if big else m,
        attn_tq=512,
        hgrn_rows=512,
        attn_heads=4,
        hgrn_heads=2 if big else 8,
    )


def _mixer_a_core(uv, v_gain, w_s, b_s, *, chunk_len, n_seq, tiles, vn_dtype):
    causal = jnp.tril(jnp.ones((A_CHUNK, A_CHUNK), bool))
    w_masked = jnp.where(causal[None], w_s, 0.0)
    if chunk_len == A_CHUNK:
        wm, bs, chunk = w_masked, b_s, A_CHUNK
    else:
        eye = jnp.eye(n_seq, dtype=w_s.dtype)
        small = w_masked[:, :chunk_len, :chunk_len]
        wm = jnp.einsum("ab,gts->gatbs", eye, small).reshape(
            A_GROUPS, n_seq * chunk_len, n_seq * chunk_len)
        bs = jnp.tile(b_s[:, :chunk_len], (1, n_seq))
        chunk = n_seq * chunk_len
    p, vn = spatial_mix(uv, wm.astype(BF16), bs[:, :, None], v_gain, chunk=chunk,
                        tm=max(tiles["spatial_tm"], chunk) if chunk_len == A_CHUNK else chunk,
                        vn_dtype=vn_dtype)
    return p, vn


def _mixer_a_proj(xn, xns, w_in, layer, *, tiles):
    width2 = w_in.shape[2]
    (uv,), (uv_s,) = seg_matmul(xn, xns, w_in, layer, (0,), width2, _gelu_epilogue, (BF16,),
                                tm=tiles["proj_tm"], tn=tiles["a_proj_tn"], name="a_in_proj")
    return uv, uv_s


def _mixer_b_proj(xn, xns, w_in, layer, q_gain, k_gain, *, heads, tiles):
    width = heads * 2 * B_HEAD_DIM
    tn = tiles["proj_tn"]
    lane_group = np.arange(tn) // B_HEAD_DIM
    group_ones = jnp.asarray(lane_group[:, None] == lane_group[None, :], BF16)
    reps = width // B_HEAD_DIM
    gq = jnp.tile(q_gain.astype(F32), reps).reshape(1, width)
    gk = jnp.tile(k_gain.astype(F32), reps).reshape(1, width)
    return seg_matmul(xn, xns, w_in, layer, (0, width, 2 * width), width, _headnorm_epilogue,
                      (F32, F32, F32), vecs=(gq, gk), consts=(group_ones,),
                      tm=tiles["proj_tm"], tn=tn, name="b_in_proj")


def _mixer_c_proj(xn, xns, w_in, layer, lower_bound, *, tiles):
    width = w_in.shape[2] // 4
    return seg_matmul(xn, xns, w_in, layer, (0, width, 2 * width, 3 * width), width, _hgrn_gate_epilogue,
                      (F32,) * 5, vecs=(lower_bound.reshape(1, width),),
                      tm=tiles["proj_tm"], tn=tiles["proj_tn"], name="c_in_proj")


def kernel(x_prompt, x_sample, cache_k, cache_v, page_table, state_hgrn, norm_mix, norm_ffn, ffn_w_gu, ffn_w_down, a_w_in, a_v_norm, a_w_s, a_b_s, a_w_out, b_w_in, b_q_norm, b_k_norm, b_lambda_q1, b_lambda_k1, b_lambda_q2, b_lambda_k2, b_subln, b_w_out, c_w_in, c_g_norm, c_lower_bounds, c_w_out):
    batch, seq, d_model = x_prompt.shape
    dec_b, dec_seq, _ = x_sample.shape
    depth = norm_mix.shape[0]
    b_heads = d_model // (2 * B_HEAD_DIM)
    c_heads = d_model // C_HEAD_DIM
    mp, ms = batch * seq, dec_b * dec_seq
    tp, ts = _tiles(mp), _tiles(ms)

    probs = jax.nn.softmax(c_lower_bounds.astype(F32), axis=0)
    lower_bound = jnp.cumsum(probs, axis=0) - probs[0]

    h_p = x_prompt.reshape(mp, d_model)
    h_s = x_sample.reshape(ms, d_model)
    xn_p = norm_rows(h_p, norm_mix[0], tm=tp["norm_tm"])
    xn_s = norm_rows(h_s, norm_mix[0], tm=ts["norm_tm"])

    n_phys = cache_k.shape[1]
    cache_k2 = cache_k.reshape(cache_k.shape[0], n_phys, PAGE_SIZE * b_heads, 2 * B_HEAD_DIM)
    cache_v2 = cache_v.reshape(cache_v.shape[0], n_phys, PAGE_SIZE * b_heads, B_V_DIM)

    k_p_rows, v_p_rows, k_s_rows, v_s_rows = [], [], [], []
    hgrn_p, hgrn_s, chunk_v_s = [], [], []
    for i in range(depth):
        kind, j = i % 3, i // 3
        if kind == 0:
            uv_p, uv_s = _mixer_a_proj(xn_p, xn_s, a_w_in, j, tiles=tp)
            y_p, _ = _mixer_a_core(uv_p, a_v_norm[j], a_w_s[j], a_b_s[j],
                                   chunk_len=A_CHUNK, n_seq=batch, tiles=tp, vn_dtype=BF16)
            y_s, vn_s = _mixer_a_core(uv_s, a_v_norm[j], a_w_s[j], a_b_s[j],
                                      chunk_len=dec_seq, n_seq=dec_b, tiles=ts, vn_dtype=F32)
            chunk_v_s.append(vn_s.reshape(dec_b, dec_seq, -1))
            w_out = a_w_out
        elif kind == 1:
            lam_init = 0.8 - 0.6 * math.exp(-0.3 * i)
            lam = (jnp.exp(jnp.sum(b_lambda_q1[j].astype(F32) * b_lambda_k1[j].astype(F32)))
                   - jnp.exp(jnp.sum(b_lambda_q2[j].astype(F32) * b_lambda_k2[j].astype(F32)))
                   + lam_init)
            (q_p, k_p, v_p), (q_s, k_s, v_s) = _mixer_b_proj(
                xn_p, xn_s, b_w_in, j, b_q_norm[j], b_k_norm[j], heads=b_heads, tiles=tp)
            y_p = diff_attn_prompt(q_p, k_p, v_p, lam, b_subln[j], lam_init,
                                   batch=batch, seq=seq, heads=b_heads, tq=tp["attn_tq"],
                                   hb=tp["attn_heads"])
            k_p_rows.append(k_p.reshape(batch, seq, b_heads, 2 * B_HEAD_DIM))
            v_p_rows.append(v_p.reshape(batch, seq, b_heads, B_V_DIM))

            q5 = (q_s *(B_HEAD_DIM ** -0.5)).reshape(dec_b, dec_seq, b_heads, 2, B_HEAD_DIM)
            qt = jnp.einsum("bthcd,ce->bhcted", q5, jnp.eye(2, dtype=F32))
            qt = qt.reshape(dec_b, b_heads // SUBLANES, SUBLANES * 2 * dec_seq, 2 * B_HEAD_DIM)
            qt = qt.transpose(0, 1, 3, 2)
            k_new = k_s.reshape(dec_b, dec_seq * b_heads, 2 * B_HEAD_DIM)
            v_new = v_s.reshape(dec_b, dec_seq * b_heads, B_V_DIM)
            o_s = diff_attn_decode(qt, cache_k2, cache_v2, page_table, k_new, v_new, lam,
                                   b_subln[j], lam_init, layer=j, heads=b_heads, n_q=dec_seq,
                                   pages=DECODE_PAGES_PER_STEP)
            y_s = o_s.transpose(0, 2, 1, 3).reshape(ms, d_model)
            k_s_rows.append(k_s.reshape(dec_b, dec_seq, b_heads, 2 * B_HEAD_DIM))
            v_s_rows.append(v_s.reshape(dec_b, dec_seq, b_heads, B_V_DIM))
            w_out = b_w_out
        else:
            qkv_p, qkv_s = _mixer_c_proj(xn_p, xn_s, c_w_in, j, lower_bound[i], tiles=tp)
            y_p, st_p = hgrn_recurrence(*qkv_p, c_g_norm[j], None, batch=batch, seq=seq,
                                        heads=c_heads, rows=tp["hgrn_rows"], chunk=C_HEAD_DIM,
                                        hb=tp["hgrn_heads"], out_dtype=BF16)
            y_s, st_s = hgrn_recurrence(*qkv_s, c_g_norm[j], state_hgrn[j], batch=dec_b,
                                        seq=dec_seq, heads=c_heads, rows=dec_seq, chunk=dec_seq,
                                        hb=ts["hgrn_heads"], out_dtype=F32)
            hgrn_p.append(st_p)
            hgrn_s.append(st_s)
            w_out = c_w_out
        h_p, xf_p, h_s, xf_s = out_proj(y_p, y_s, w_out, j, h_p, h_s, norm_ffn[i], tm=tp["out_tm"])
        gain_next = norm_mix[i + 1] if i + 1 < depth else None
        h_p, h_s, xn_p, xn_s = ffn(xf_p, h_p, xf_s, h_s, ffn_w_gu, ffn_w_down, i, gain_next,
                                   tm=tp["ffn_tm"], th=tp["ffn_th"])
    return (h_p.reshape(batch, seq, d_model), h_s.reshape(dec_b, dec_seq, d_model),
            jnp.stack(k_p_rows), jnp.stack(v_p_rows), jnp.stack(k_s_rows), jnp.stack(v_s_rows),
            jnp.stack(hgrn_p), jnp.stack(hgrn_s), jnp.stack(chunk_v_s))
```

```python
import functools
import math

import jax
import jax.numpy as jnp
import numpy as np
from jax import lax
from jax.experimental import pallas as pl
from jax.experimental.pallas import tpu as pltpu

F32 = jnp.float32
BF16 = jnp.bfloat16
EPS = 1e-6

LANES = 128
SUBLANES = 8
VMEM_LIMIT_BYTES = 56 << 20
FFN_VMEM_LIMIT_BYTES = 60 << 20
FFN_RESIDUAL_SLICES = 8

A_CHUNK = 128
A_GROUPS = 8
B_HEAD_DIM = 64
B_V_DIM = 2 * B_HEAD_DIM
C_HEAD_DIM = 128
PAGE_SIZE = 128
HGRN_SUB = SUBLANES
DECODE_PAGES_PER_STEP = 8


def _cparams(n_axes):
    return pltpu.CompilerParams(
        dimension_semantics=("arbitrary",) * n_axes,
        vmem_limit_bytes=VMEM_LIMIT_BYTES,
    )


def _rmsnorm_f32(x, gain):
    return x * lax.rsqrt(jnp.mean(x * x, axis=-1, keepdims=True) + EPS) * gain


def _sigmoid(x):
    return 1.0 / (1.0 + jnp.exp(-x))


def _norm_rows_body(x_ref, g_ref, o_ref):
    o_ref[...] = _rmsnorm_f32(x_ref[...], g_ref[...]).astype(o_ref.dtype)


def norm_rows(x, gain, *, tm):
    m, d = x.shape
    return pl.pallas_call(
        _norm_rows_body,
        grid=(m // tm,),
        in_specs=[pl.BlockSpec((tm, d), lambda i: (i, 0)),
                  pl.BlockSpec((1, d), lambda i: (0, 0))],
        out_specs=pl.BlockSpec((tm, d), lambda i: (i, 0)),
        out_shape=jax.ShapeDtypeStruct((m, d), BF16),
        compiler_params=_cparams(1),
        name="norm_rows",
    )(x, gain.reshape(1, d))


def _seg_matmul_body(*refs, ns, nv, nc, no, epilogue):
    x_ref, xs_ref = refs[0:2]
    w_refs = refs[2:2 + ns]
    vec_refs = refs[2 + ns:2 + ns + nv]
    const_refs = refs[2 + ns + nv:2 + ns + nv + nc]
    out_refs = refs[2 + ns + nv + nc:2 + ns + nv + nc + no]
    sample_out_refs = refs[2 + ns + nv + nc + no:-1]
    wb_ref = refs[-1]
    first_row_tile = pl.program_id(1) == 0

    @pl.when(first_row_tile)
    def _():
        for s in range(ns):
            wb_ref[s] = w_refs[s][...].astype(BF16)

    def project(rows_ref, dst_refs):
        x = rows_ref[...]
        accs = [jnp.dot(x, wb_ref[s], preferred_element_type=F32) for s in range(ns)]
        outs = epilogue(accs, [r[...] for r in vec_refs], [r[...] for r in const_refs])
        for r, o in zip(dst_refs, outs):
            r[...] = o.astype(r.dtype)

    project(x_ref, out_refs)

    @pl.when(first_row_tile)
    def _():
        project(xs_ref, sample_out_refs)


def seg_matmul(x, xs, w, layer, seg_starts, seg_width, epilogue, out_dtypes, vecs=(), consts=(),
               *, tm, tn, name):
    m, k = x.shape
    ms = xs.shape[0]
    ns = len(seg_starts)
    no = len(out_dtypes)
    in_specs = [pl.BlockSpec((tm, k), lambda j, i: (i, 0)),
                pl.BlockSpec((ms, k), lambda j, i: (0, 0))]
    for st in seg_starts:
        in_specs.append(pl.BlockSpec((None, k, tn), lambda j, i, off=st // tn: (layer, 0, off + j)))
    for _ in vecs:
        in_specs.append(pl.BlockSpec((1, tn), lambda j, i: (0, j)))
    for c in consts:
        in_specs.append(pl.BlockSpec(c.shape, lambda j, i, nd=c.ndim: (0,) * nd))
    res = pl.pallas_call(
        functools.partial(_seg_matmul_body, ns=ns, nv=len(vecs), nc=len(consts), no=no,
                          epilogue=epilogue),
        grid=(seg_width // tn, m // tm),
        in_specs=in_specs,
        out_specs=([pl.BlockSpec((tm, tn), lambda j, i: (i, j)) for _ in out_dtypes]
                   + [pl.BlockSpec((ms, tn), lambda j, i: (0, j)) for _ in out_dtypes]),
        out_shape=([jax.ShapeDtypeStruct((m, seg_width), dt) for dt in out_dtypes]
                   + [jax.ShapeDtypeStruct((ms, seg_width), dt) for dt in out_dtypes]),
        scratch_shapes=[pltpu.VMEM((ns, k, tn), BF16)],
        compiler_params=_cparams(2),
        name=name,
    )(x, xs, *([w] * ns), *vecs, *consts)
    return res[:no], res[no:]


def _gelu_exact_f32(a):
    z = a * (2.0 ** -0.5)
    az = jnp.abs(z)
    t = 1.0 / (1.0 + 0.3275911 * az)
    poly = t * (0.254829592 + t * (-0.284496736 + t * (1.421413741
                                                       + t * (-1.453152027 + t * 1.061405429))))
    erfc_abs = poly * jnp.exp(-az * az)
    return 0.5 * a * jnp.where(z >= 0, 2.0 - erfc_abs, erfc_abs)


def _gelu_epilogue(accs, vecs, consts):
    (a,) = accs
    return [_gelu_exact_f32(a)]


def _headnorm_epilogue(accs, vecs, consts):
    aq, ak, av = accs
    gq, gk = vecs
    (group_ones,) = consts

    def head_norm(a, g):
        sq = a * a
        hi = sq.astype(BF16)
        lo = (sq - hi.astype(F32)).astype(BF16)
        ms = (jnp.dot(hi, group_ones, preferred_element_type=F32)
              + jnp.dot(lo, group_ones, preferred_element_type=F32)) * (1.0 / B_HEAD_DIM)
        return a * lax.rsqrt(ms + EPS) * g

    return [head_norm(aq, gq), head_norm(ak, gk), av]


def _hgrn_gate_epilogue(accs, vecs, consts):
    aq, af, av, ag = accs
    (lb,) = vecs
    q = aq * _sigmoid(aq)
    f = lb + (1.0 - lb) * _sigmoid(af)
    k = (1.0 - lb) * _sigmoid(-af)
    return [q, k, jnp.log(f), av, ag]


def _out_proj_body(y_ref, ys_ref, w_ref, h_ref, hs_ref, g_ref, ho_ref, xo_ref, hso_ref, xso_ref,
                   wb_ref, *, cast_rows):
    first_row_tile = pl.program_id(0) == 0

    @pl.when(first_row_tile)
    def _():
        def cast(r, carry):
            sl = pl.ds(pl.multiple_of(r * cast_rows, cast_rows), cast_rows)
            wb_ref[sl, :] = w_ref[sl, :].astype(BF16)
            return carry
        lax.fori_loop(0, w_ref.shape[0] // cast_rows, cast, 0)

    def project(rows_ref, res_ref, h_out_ref, x_out_ref):
        hn = res_ref[...] + jnp.dot(rows_ref[...].astype(BF16), wb_ref[...],
                                    preferred_element_type=F32)
        h_out_ref[...] = hn
        x_out_ref[...] = _rmsnorm_f32(hn, g_ref[...]).astype(x_out_ref.dtype)

    project(y_ref, h_ref, ho_ref, xo_ref)

    @pl.when(first_row_tile)
    def _():
        project(ys_ref, hs_ref, hso_ref, xso_ref)


def out_proj(y, ys, w, layer, h, hs, gain_next, *, tm):
    m, k = y.shape
    ms = ys.shape[0]
    n = w.shape[2]
    row = lambda cols: pl.BlockSpec((tm, cols), lambda i: (i, 0))
    sample = lambda cols: pl.BlockSpec((ms, cols), lambda i: (0, 0))
    return pl.pallas_call(
        functools.partial(_out_proj_body, cast_rows=256),
        grid=(m // tm,),
        in_specs=[row(k), sample(k),
                  pl.BlockSpec((None, k, n), lambda i: (layer, 0, 0), pipeline_mode=pl.Buffered(1)),
                  row(n), sample(n),
                  pl.BlockSpec((1, n), lambda i: (0, 0))],
        out_specs=[row(n), row(n), sample(n), sample(n)],
        out_shape=[jax.ShapeDtypeStruct((m, n), F32), jax.ShapeDtypeStruct((m, n), BF16),
                   jax.ShapeDtypeStruct((ms, n), F32), jax.ShapeDtypeStruct((ms, n), BF16)],
        scratch_shapes=[pltpu.VMEM((k, n), BF16)],
        compiler_params=_cparams(1),
        name="out_proj",
    )(y, ys, w, h, hs, gain_next.reshape(1, n))


def _ffn_body(x_ref, h_ref, xs_ref, hs_ref, wg_ref, wu_ref, wd_ref, g_ref, *out_refs, n_t, with_norm):
    if with_norm:
        ho_ref, hso_ref, xo_ref, xso_ref = out_refs
    else:
        ho_ref, hso_ref = out_refs
    i = pl.program_id(0)
    t = pl.program_id(1)

    @pl.when(t == 0)
    def _():
        ho_ref[...] = jnp.zeros(ho_ref.shape, F32)

    slice_rows = h_ref.shape[0]
    for p in range(FFN_RESIDUAL_SLICES):
        @pl.when(t == p)
        def _(p=p):
            ho_ref[p * slice_rows:(p + 1) * slice_rows, :] += h_ref[...]

    def swiglu(x):
        gate = jnp.dot(x, wg_ref[...].astype(BF16), preferred_element_type=F32)
        up = jnp.dot(x, wu_ref[...].astype(BF16), preferred_element_type=F32)
        act = (gate * _sigmoid(gate) * up).astype(BF16)
        return jnp.dot(act, wd_ref[...].astype(BF16), preferred_element_type=F32)

    ho_ref[...] += swiglu(x_ref[...])

    @pl.when((i == 0) & (t == 0))
    def _():
        hso_ref[...] = hs_ref[...]

    @pl.when(i == 0)
    def _():
        hso_ref[...] += swiglu(xs_ref[...])

    if with_norm:
        @pl.when(t == n_t - 1)
        def _():
            xo_ref[...] = _rmsnorm_f32(ho_ref[...], g_ref[...]).astype(BF16)

        @pl.when((i == 0) & (t == n_t - 1))
        def _():
            xso_ref[...] = _rmsnorm_f32(hso_ref[...], g_ref[...]).astype(BF16)


def ffn(x, h, xs, hs, w_gu, w_down, layer, gain_next, *, tm, th):
    m, d = x.shape
    ms = xs.shape[0]
    hidden = w_down.shape[1]
    n_t = hidden // th
    with_norm = gain_next is not None
    gain = gain_next if with_norm else jnp.ones((d,), F32)
    row_out = pl.BlockSpec((tm, d), lambda i, t: (i, 0))
    sample_block = pl.BlockSpec((ms, d), lambda i, t: (0, 0))
    out_specs = [row_out, sample_block]
    out_shape = [jax.ShapeDtypeStruct((m, d), F32), jax.ShapeDtypeStruct((ms, d), F32)]
    if with_norm:
        out_specs += [row_out, sample_block]
        out_shape += [jax.ShapeDtypeStruct((m, d), BF16), jax.ShapeDtypeStruct((ms, d), BF16)]
    assert n_t >= FFN_RESIDUAL_SLICES and tm % (FFN_RESIDUAL_SLICES * SUBLANES) == 0
    row_block = pl.BlockSpec((tm, d), lambda i, t: (i, 0))
    last_slice = FFN_RESIDUAL_SLICES - 1
    residual_slice = pl.BlockSpec(
        (tm // FFN_RESIDUAL_SLICES, d),
        lambda i, t: (i * FFN_RESIDUAL_SLICES + jnp.minimum(t, last_slice), 0))
    res = pl.pallas_call(
        functools.partial(_ffn_body, n_t=n_t, with_norm=with_norm),
        grid=(m // tm, n_t),
        in_specs=[row_block, residual_slice, sample_block, sample_block,
                  pl.BlockSpec((None, d, th), lambda i, t: (layer, 0, t)),
                  pl.BlockSpec((None, d, th), lambda i, t: (layer, 0, n_t + t)),
                  pl.BlockSpec((None, th, d), lambda i, t: (layer, t, 0)),
                  pl.BlockSpec((1, d), lambda i, t: (0, 0))],
        out_specs=out_specs,
        out_shape=out_shape,
        compiler_params=pltpu.CompilerParams(dimension_semantics=("arbitrary",) * 2,
                                             vmem_limit_bytes=FFN_VMEM_LIMIT_BYTES),
        name="ffn",
    )(x, h, xs, hs, w_gu, w_gu, w_down, gain.reshape(1, d))
    return tuple(res) if with_norm else (res[0], res[1], None, None)


def _spatial_body(u_ref, v_ref, wm_ref, bs_ref, vg_ref, p_ref, vn_ref, *, chunk, groups):
    v = v_ref[...].astype(F32)
    vn = _rmsnorm_f32(v, vg_ref[...])
    vn_ref[...] = vn.astype(vn_ref.dtype)
    vnb = vn.astype(BF16)
    rows, width = v.shape
    gw = width // groups
    for c in range(rows // chunk):
        r0 = c * chunk
        for g in range(groups):
            c0 = g * gw
            s = jnp.dot(wm_ref[g], vnb[r0:r0 + chunk, c0:c0 + gw], preferred_element_type=F32)
            s = s + bs_ref[g]
            u = u_ref[r0:r0 + chunk, c0:c0 + gw].astype(F32)
            p_ref[r0:r0 + chunk, c0:c0 + gw] = (u * s).astype(p_ref.dtype)


def spatial_mix(uv, wm, bs, v_gain, *, chunk, tm, vn_dtype):
    m, w2 = uv.shape
    width = w2 // 2
    groups = wm.shape[0]
    return pl.pallas_call(
        functools.partial(_spatial_body, chunk=chunk, groups=groups),
        grid=(m // tm,),
        in_specs=[pl.BlockSpec((tm, width), lambda i: (i, 0)),
                  pl.BlockSpec((tm, width), lambda i: (i, 1)),
                  pl.BlockSpec(wm.shape, lambda i: (0, 0, 0)),
                  pl.BlockSpec(bs.shape, lambda i: (0, 0, 0)),
                  pl.BlockSpec((1, width), lambda i: (0, 0))],
        out_specs=[pl.BlockSpec((tm, width), lambda i: (i, 0)),
                   pl.BlockSpec((tm, width), lambda i: (i, 0))],
        out_shape=[jax.ShapeDtypeStruct((m, width), BF16),
                   jax.ShapeDtypeStruct((m, width), vn_dtype)],
        compiler_params=_cparams(1),
        name="spatial_mix",
    )(uv, uv, wm, bs, v_gain.reshape(1, width))


def _diff_attn_body(qt_tab, kt_tab, q_ref, k_ref, v_ref, lam_ref, sub_ref, o_ref,
                    qs_ref, m_ref, l_ref, acc_ref, *, tq, hb, out_scale):
    t = pl.program_id(2)
    qi = qt_tab[t]
    ki = kt_tab[t]
    hd = 2 * B_HEAD_DIM

    @pl.when(ki == 0)
    def _():
        for h in range(hb):
            q = q_ref[:, h * hd:(h + 1) * hd].astype(F32) * (B_HEAD_DIM ** -0.5)
            lane = lax.broadcasted_iota(jnp.int32, q.shape, 1)
            qs_ref[h, 0:tq, :] = jnp.where(lane < B_HEAD_DIM, q, 0.0)
            qs_ref[h, tq:2 * tq, :] = jnp.where(lane >= B_HEAD_DIM, q, 0.0)
        m_ref[...] = jnp.full(m_ref.shape, -jnp.inf, F32)
        l_ref[...] = jnp.zeros(l_ref.shape, F32)
        acc_ref[...] = jnp.zeros(acc_ref.shape, F32)

    def strip(h, r0, masked):
        rs = pl.ds(r0, tq)
        keys = k_ref[:, h * hd:(h + 1) * hd]
        vals = v_ref[:, h * B_V_DIM:(h + 1) * B_V_DIM]
        s = lax.dot_general(qs_ref[h, rs, :], keys, (((1,), (1,)), ((), ())),
                            preferred_element_type=F32)
        if masked:
            row = lax.broadcasted_iota(jnp.int32, s.shape, 0)
            col = lax.broadcasted_iota(jnp.int32, s.shape, 1)
            s = jnp.where(col <= row, s, -jnp.inf)
        m_prev = m_ref[h, rs, :]
        m_new = jnp.maximum(m_prev, jnp.max(s, axis=1, keepdims=True))
        alpha = jnp.exp(m_prev - m_new)
        p = jnp.exp(s - m_new[:, 0:1])
        l_ref[h, rs, :] = alpha * l_ref[h, rs, :] + jnp.sum(p, axis=1, keepdims=True)
        acc_ref[h, rs, :] = alpha * acc_ref[h, rs, :] + jnp.dot(p, vals, preferred_element_type=F32)
        m_ref[h, rs, :] = m_new

    @pl.when(ki < qi)
    def _():
        for h in range(hb):
            strip(h, 0, False)
            strip(h, tq, False)

    @pl.when(ki == qi)
    def _():
        for h in range(hb):
            strip(h, 0, True)
            strip(h, tq, True)
            o = acc_ref[h] / l_ref[h]
            d = o[0:tq] - lam_ref[0] * o[tq:2 * tq]
            o_ref[:, h * B_V_DIM:(h + 1) * B_V_DIM] = (
                _rmsnorm_f32(d, sub_ref[...]) * out_scale).astype(o_ref.dtype)


def diff_attn_prompt(q, k, v, lam, subln, lam_init, *, batch, seq, heads, tq, hb):
    nq = seq // tq
    tri = [(qi, ki) for qi in range(nq) for ki in range(qi + 1)]
    qt_tab = jnp.asarray([a for a, _ in tri], jnp.int32)
    kt_tab = jnp.asarray([b for _, b in tri], jnp.int32)
    hd = 2 * B_HEAD_DIM
    grid_spec = pltpu.PrefetchScalarGridSpec(
        num_scalar_prefetch=2,
        grid=(batch, heads // hb, len(tri)),
        in_specs=[pl.BlockSpec((tq, hb * hd), lambda b, h, t, qt, kt: (b * nq + qt[t], h)),
                  pl.BlockSpec((tq, hb * hd), lambda b, h, t, qt, kt: (b * nq + kt[t], h)),
                  pl.BlockSpec((tq, hb * B_V_DIM), lambda b, h, t, qt, kt: (b * nq + kt[t], h)),
                  pl.BlockSpec(memory_space=pltpu.SMEM),
                  pl.BlockSpec((1, B_V_DIM), lambda b, h, t, qt, kt: (0, 0))],
        out_specs=pl.BlockSpec((tq, hb * B_V_DIM), lambda b, h, t, qt, kt: (b * nq + qt[t], h)),
        scratch_shapes=[pltpu.VMEM((hb, 2 * tq, hd), F32),
                        pltpu.VMEM((hb, 2 * tq, LANES), F32),
                        pltpu.VMEM((hb, 2 * tq, LANES), F32),
                        pltpu.VMEM((hb, 2 * tq, B_V_DIM), F32)],
    )
    return pl.pallas_call(
        functools.partial(_diff_attn_body, tq=tq, hb=hb, out_scale=1.0 - lam_init),
        grid_spec=grid_spec,
        out_shape=jax.ShapeDtypeStruct((batch * seq, heads * B_V_DIM), BF16),
        compiler_params=_cparams(3),
        name="diff_attn_prompt",
    )(qt_tab, kt_tab, q, k, v, lam.reshape(1), subln.reshape(1, B_V_DIM))


def _decode_attn_body(pt_ref, *refs, pages, heads, n_q, n_groups, out_scale):
    k_refs = refs[:pages]
    v_refs = refs[pages:2 * pages]
    kn_ref, vn_ref, qt_ref, lam_ref, sub_ref, o_ref, m_ref, l_ref, acc_ref = refs[2 * pages:]
    g = pl.program_id(1)
    hg = heads // SUBLANES
    hc = 2 * n_q
    cols = SUBLANES * hc

    @pl.when(g == 0)
    def _():
        m_ref[...] = jnp.full(m_ref.shape, -jnp.inf, F32)
        l_ref[...] = jnp.zeros(l_ref.shape, F32)
        acc_ref[...] = jnp.zeros(acc_ref.shape, F32)

    sub = lax.broadcasted_iota(jnp.int32, (SUBLANES, cols), 0)
    lane = lax.broadcasted_iota(jnp.int32, (SUBLANES, cols), 1)
    own = sub == lane // hc

    def to_column(x8):
        r = jnp.sum(jnp.where(own, x8, 0.0), axis=0, keepdims=True)
        return jnp.broadcast_to(r, (LANES, cols)).T

    def group_rows(ref, j, n_pos):
        x = ref[0:n_pos * heads, :].reshape(n_pos, hg, SUBLANES, 2 * B_HEAD_DIM)
        return x[:, j].reshape(n_pos * SUBLANES, 2 * B_HEAD_DIM)

    def process(page_k_refs, page_v_refs, n_pos, new_tokens):
        for j in range(hg):
            scores = []
            for k_ref in page_k_refs:
                s = jnp.dot(group_rows(k_ref, j, n_pos), qt_ref[j], preferred_element_type=F32)
                s = s.reshape(n_pos, SUBLANES, cols)
                valid = own[None]
                if new_tokens:
                    pos = lax.broadcasted_iota(jnp.int32, s.shape, 0)
                    qry = lax.broadcasted_iota(jnp.int32, s.shape, 2) % n_q
                    valid = valid & (pos <= qry)
                scores.append(jnp.where(valid, s, -jnp.inf))
            m_prev = m_ref[j]
            m_new = m_prev
            for s in scores:
                m_new = jnp.maximum(m_new, jnp.max(s, axis=0))
            m_safe = jnp.where(own, m_new, 0.0)
            alpha = jnp.exp(m_prev - m_safe)
            l_new = alpha * l_ref[j]
            pv = jnp.zeros((cols, B_V_DIM), F32)
            for s, v_ref in zip(scores, page_v_refs):
                p = jnp.exp(s - m_safe[None])
                l_new = l_new + jnp.sum(p, axis=0)
                pv = pv + lax.dot_general(p.reshape(n_pos * SUBLANES, cols), group_rows(v_ref, j, n_pos),
                                          (((0,), (0,)), ((), ())), preferred_element_type=F32)
            acc_ref[j] = acc_ref[j] * to_column(alpha) + pv
            l_ref[j] = l_new
            m_ref[j] = m_new

    process(k_refs, v_refs, PAGE_SIZE, False)

    @pl.when(g == n_groups - 1)
    def _():
        process([kn_ref], [vn_ref], n_q, True)
        for j in range(hg):
            o = acc_ref[j] / to_column(l_ref[j])
            o = o.reshape(SUBLANES, 2, n_q, B_V_DIM)
            d = o[:, 0] - lam_ref[0] * o[:, 1]
            d = d * lax.rsqrt(jnp.mean(d * d, axis=-1, keepdims=True) + EPS) * sub_ref[...]
            o_ref[j * SUBLANES:(j + 1) * SUBLANES] = (d * out_scale).astype(o_ref.dtype)


def diff_attn_decode(qt, cache_k, cache_v, page_table, k_new, v_new, lam, subln, lam_init,
                     *, layer, heads, n_q, pages):
    dec_b, n_pages = page_table.shape
    hd = 2 * B_HEAD_DIM
    rows = PAGE_SIZE * heads
    hg = heads // SUBLANES
    cols = SUBLANES * 2 * n_q
    n_groups = n_pages // pages
    page_spec = lambda p_i: pl.BlockSpec(
        (None, None, rows, hd),
        lambda b, g, pt, p_i=p_i: (layer, pt[b, g * pages + p_i], 0, 0))
    grid_spec = pltpu.PrefetchScalarGridSpec(
        num_scalar_prefetch=1,
        grid=(dec_b, n_groups),
        in_specs=([page_spec(p_i) for p_i in range(pages)] * 2
                  + [pl.BlockSpec((None, n_q * heads, hd), lambda b, g, pt: (b, 0, 0)),
                     pl.BlockSpec((None, n_q * heads, hd), lambda b, g, pt: (b, 0, 0)),
                     pl.BlockSpec((None, hg, hd, cols), lambda b, g, pt: (b, 0, 0, 0)),
                     pl.BlockSpec(memory_space=pltpu.SMEM),
                     pl.BlockSpec((1, B_V_DIM), lambda b, g, pt: (0, 0))]),
        out_specs=pl.BlockSpec((None, heads, n_q, B_V_DIM), lambda b, g, pt: (b, 0, 0, 0)),
        scratch_shapes=[pltpu.VMEM((hg, SUBLANES, cols), F32),
                        pltpu.VMEM((hg, SUBLANES, cols), F32),
                        pltpu.VMEM((hg, cols, B_V_DIM), F32)],
    )
    return pl.pallas_call(
        functools.partial(_decode_attn_body, pages=pages, heads=heads, n_q=n_q,
                          n_groups=n_groups, out_scale=1.0 - lam_init),
        grid_spec=grid_spec,
        out_shape=jax.ShapeDtypeStruct((dec_b, heads, n_q, B_V_DIM), F32),
        compiler_params=_cparams(2),
        name="diff_attn_decode",
    )(page_table, *([cache_k] * pages), *([cache_v] * pages), k_new, v_new, qt,
      lam.reshape(1), subln.reshape(1, B_V_DIM))


def _cumsum_rows(x):
    c = x.shape[0]
    sub = lax.broadcasted_iota(jnp.int32, x.shape, 0) % HGRN_SUB
    d = 1
    while d < HGRN_SUB:
        x = x + jnp.where(sub >= d, pltpu.roll(x, d, axis=0), 0.0)
        d *= 2
    blocks = []
    carry = None
    for j in range(c // HGRN_SUB):
        blk = x[j * HGRN_SUB:(j + 1) * HGRN_SUB]
        if carry is not None:
            blk = blk + carry
        blocks.append(blk)
        carry = blk[HGRN_SUB - 1:HGRN_SUB]
    return jnp.concatenate(blocks, axis=0) if len(blocks) > 1 else blocks[0]


def _hgrn_chunk(q, k, lf, v, st):
    c = q.shape[0]
    nb = c // HGRN_SUB
    gcum = _cumsum_rows(lf)
    o = lax.dot_general(q * jnp.exp(gcum), st, (((1,), (1,)), ((), ())), preferred_element_type=F32)

    if nb > 1:
        row = lax.broadcasted_iota(jnp.int32, (c, C_HEAD_DIM), 0)
        t_idx = lax.broadcasted_iota(jnp.int32, (c, c), 0)
        s_idx = lax.broadcasted_iota(jnp.int32, (c, c), 1)
        a_off = None
        size = 2 * HGRN_SUB
        while size <= c:
            half = size // 2
            if size < c:
                g_mid = jnp.concatenate(
                    [jnp.broadcast_to(gcum[b0 + half - 1:b0 + half], (size, C_HEAD_DIM))
                     for b0 in range(0, c, size)], axis=0)
            else:
                g_mid = gcum[half - 1:half]
            upper = (row % size) >= half
            qd = q * jnp.exp(jnp.where(upper, gcum - g_mid, -jnp.inf))
            kd = k * jnp.exp(jnp.where(upper, -jnp.inf, g_mid - gcum))
            a = lax.dot_general(qd, kd, (((1,), (1,)), ((), ())), preferred_element_type=F32)
            if size < c:
                a = jnp.where((t_idx // size) == (s_idx // size), a, 0.0)
            a_off = a if a_off is None else a_off + a
            size *= 2
        o = o + jnp.dot(a_off, v, preferred_element_type=F32)

    sub_row = lax.broadcasted_iota(jnp.int32, (HGRN_SUB, C_HEAD_DIM), 0)
    o_blocks = []
    for i in range(nb):
        r0 = i * HGRN_SUB
        gi = gcum[r0:r0 + HGRN_SUB]
        qi = q[r0:r0 + HGRN_SUB]
        ki = k[r0:r0 + HGRN_SUB]
        vi = v[r0:r0 + HGRN_SUB]
        oi = o[r0:r0 + HGRN_SUB]
        for s in range(HGRN_SUB):
            dec = jnp.exp(jnp.where(sub_row >= s, gi - gi[s:s + 1], -jnp.inf))
            a_col = jnp.sum(qi * ki[s:s + 1] * dec, axis=-1, keepdims=True)
            oi = oi + a_col * vi[s:s + 1]
        o_blocks.append(oi)
    o = jnp.concatenate(o_blocks, axis=0) if nb > 1 else o_blocks[0]
    g_last = gcum[c - 1:c]
    kd = k * jnp.exp(g_last - gcum)
    if c < C_HEAD_DIM:
        pad = jnp.zeros((C_HEAD_DIM - c, C_HEAD_DIM), F32)
        kd = jnp.concatenate([kd, pad], axis=0)
        v = jnp.concatenate([v, pad], axis=0)
    st_new = st * jnp.exp(g_last) + jnp.dot(v.T, kd, preferred_element_type=F32)
    return o, st_new


def _hgrn_body(*refs, chunk, n_chunks, n_r, hb, with_state):
    if with_state:
        q_ref, k_ref, lf_ref, v_ref, g_ref, gg_ref, s0_ref, o_ref, so_ref, st_ref = refs
    else:
        q_ref, k_ref, lf_ref, v_ref, g_ref, gg_ref, o_ref, so_ref, st_ref = refs
    r = pl.program_id(2)
    hd = C_HEAD_DIM

    @pl.when(r == 0)
    def _():
        for h in range(hb):
            if with_state:
                st_ref[h] = s0_ref[h].astype(F32).T
            else:
                st_ref[h] = jnp.zeros((hd, hd), F32)

    def step(ci, carry):
        sl = pl.ds(pl.multiple_of(ci * chunk, chunk), chunk)
        for h in range(hb):
            cs = slice(h * hd, (h + 1) * hd)
            o, st_new = _hgrn_chunk(q_ref[sl, cs], k_ref[sl, cs], lf_ref[sl, cs], v_ref[sl, cs],
                                    st_ref[h])
            st_ref[h] = st_new
            gate = g_ref[sl, cs]
            o = _rmsnorm_f32(o, gg_ref[...]) * (gate * _sigmoid(gate))
            o_ref[sl, cs] = o.astype(o_ref.dtype)
        return carry

    lax.fori_loop(0, n_chunks, step, 0)

    @pl.when(r == n_r - 1)
    def _():
        for h in range(hb):
            so_ref[h] = st_ref[h].T.astype(so_ref.dtype)


def hgrn_recurrence(q, k, lf, v, g, g_gain, state0, *, batch, seq, heads, rows, chunk, hb, out_dtype):
    n_r = seq // rows
    hd = C_HEAD_DIM
    with_state = state0 is not None
    row_spec = pl.BlockSpec((rows, hb * hd), lambda b, h, r: (b * n_r + r, h))
    state_spec = pl.BlockSpec((None, hb, hd, hd), lambda b, h, r: (b, h, 0, 0))
    in_specs = [row_spec] * 5 + [pl.BlockSpec((1, hd), lambda b, h, r: (0, 0))]
    args = [q, k, lf, v, g, g_gain.reshape(1, hd)]
    if with_state:
        in_specs.append(state_spec)
        args.append(state0)
    return pl.pallas_call(
        functools.partial(_hgrn_body, chunk=chunk, n_chunks=rows // chunk, n_r=n_r, hb=hb,
                          with_state=with_state),
        grid=(batch, heads // hb, n_r),
        in_specs=in_specs,
        out_specs=[row_spec, state_spec],
        out_shape=[jax.ShapeDtypeStruct((batch * seq, heads * hd), out_dtype),
                   jax.ShapeDtypeStruct((batch, heads, hd, hd), F32)],
        scratch_shapes=[pltpu.VMEM((hb, hd, hd), F32)],
        compiler_params=_cparams(3),
        name="hgrn_recurrence",
    )(*args)


def _tiles(m):
    big = m >= 1024
    return dict(
        norm_tm=512 if big else m,
        proj_tm=1024 if big else m,
        proj_tn=256,
        a_proj_tn=1024,
        out_tm=512 if big else m,
        ffn_tm=1024 if big else m,
        ffn_th=256,
        spatial_tm=256 if big else m,
        attn_tq=512,
        hgrn_rows=512,
        attn_heads=8,
        hgrn_heads=2 if big else 8,
    )


def _mixer_a_core(uv, v_gain, w_s, b_s, *, chunk_len, n_seq, tiles, vn_dtype):
    causal = jnp.tril(jnp.ones((A_CHUNK, A_CHUNK), bool))
    w_masked = jnp.where(causal[None], w_s, 0.0)
    if chunk_len == A_CHUNK:
        wm, bs, chunk = w_masked, b_s, A_CHUNK
    else:
        eye = jnp.eye(n_seq, dtype=w_s.dtype)
        small = w_masked[:, :chunk_len, :chunk_len]
        wm = jnp.einsum("ab,gts->gatbs", eye, small).reshape(
            A_GROUPS, n_seq * chunk_len, n_seq * chunk_len)
        bs = jnp.tile(b_s[:, :chunk_len], (1, n_seq))
        chunk = n_seq * chunk_len
    p, vn = spatial_mix(uv, wm.astype(BF16), bs[:, :, None], v_gain, chunk=chunk,
                        tm=max(tiles["spatial_tm"], chunk) if chunk_len == A_CHUNK else chunk,
                        vn_dtype=vn_dtype)
    return p, vn


def _mixer_a_proj(xn, xns, w_in, layer, *, tiles):
    width2 = w_in.shape[2]
    (uv,), (uv_s,) = seg_matmul(xn, xns, w_in, layer, (0,), width2, _gelu_epilogue, (BF16,),
                                tm=tiles["proj_tm"], tn=tiles["a_proj_tn"], name="a_in_proj")
    return uv, uv_s


def _mixer_b_proj(xn, xns, w_in, layer, q_gain, k_gain, *, heads, tiles):
    width = heads * 2 * B_HEAD_DIM
    tn = tiles["proj_tn"]
    lane_group = np.arange(tn) // B_HEAD_DIM
    group_ones = jnp.asarray(lane_group[:, None] == lane_group[None, :], BF16)
    reps = width // B_HEAD_DIM
    gq = jnp.tile(q_gain.astype(F32), reps).reshape(1, width)
    gk = jnp.tile(k_gain.astype(F32), reps).reshape(1, width)
    return seg_matmul(xn, xns, w_in, layer, (0, width, 2 * width), width, _headnorm_epilogue,
                      (F32, F32, F32), vecs=(gq, gk), consts=(group_ones,),
                      tm=tiles["proj_tm"], tn=tn, name="b_in_proj")


def _mixer_c_proj(xn, xns, w_in, layer, lower_bound, *, tiles):
    width = w_in.shape[2] // 4
    return seg_matmul(xn, xns, w_in, layer, (0, width, 2 * width, 3 * width), width, _hgrn_gate_epilogue,
                      (F32,) * 5, vecs=(lower_bound.reshape(1, width),),
                      tm=tiles["proj_tm"], tn=tiles["proj_tn"], name="c_in_proj")


def kernel(x_prompt, x_sample, cache_k, cache_v, page_table, state_hgrn, norm_mix, norm_ffn, ffn_w_gu, ffn_w_down, a_w_in, a_v_norm, a_w_s, a_b_s, a_w_out, b_w_in, b_q_norm, b_k_norm, b_lambda_q1, b_lambda_k1, b_lambda_q2, b_lambda_k2, b_subln, b_w_out, c_w_in, c_g_norm, c_lower_bounds, c_w_out):
    batch, seq, d_model = x_prompt.shape
    dec_b, dec_seq, _ = x_sample.shape
    depth = norm_mix.shape[0]
    b_heads = d_model // (2 * B_HEAD_DIM)
    c_heads = d_model // C_HEAD_DIM
    mp, ms = batch * seq, dec_b * dec_seq
    tp, ts = _tiles(mp), _tiles(ms)

    probs = jax.nn.softmax(c_lower_bounds.astype(F32), axis=0)
    lower_bound = jnp.cumsum(probs, axis=0) - probs[0]

    h_p = x_prompt.reshape(mp, d_model)
    h_s = x_sample.reshape(ms, d_model)
    xn_p = norm_rows(h_p, norm_mix[0], tm=tp["norm_tm"])
    xn_s = norm_rows(h_s, norm_mix[0], tm=ts["norm_tm"])

    n_phys = cache_k.shape[1]
    cache_k2 = cache_k.reshape(cache_k.shape[0], n_phys, PAGE_SIZE * b_heads, 2 * B_HEAD_DIM)
    cache_v2 = cache_v.reshape(cache_v.shape[0], n_phys, PAGE_SIZE * b_heads, B_V_DIM)

    k_p_rows, v_p_rows, k_s_rows, v_s_rows = [], [], [], []
    hgrn_p, hgrn_s, chunk_v_s = [], [], []
    for i in range(depth):
        kind, j = i % 3, i // 3
        if kind == 0:
            uv_p, uv_s = _mixer_a_proj(xn_p, xn_s, a_w_in, j, tiles=tp)
            y_p, _ = _mixer_a_core(uv_p, a_v_norm[j], a_w_s[j], a_b_s[j],
                                   chunk_len=A_CHUNK, n_seq=batch, tiles=tp, vn_dtype=BF16)
            y_s, vn_s = _mixer_a_core(uv_s, a_v_norm[j], a_w_s[j], a_b_s[j],
                                      chunk_len=dec_seq, n_seq=dec_b, tiles=ts, vn_dtype=F32)
            chunk_v_s.append(vn_s.reshape(dec_b, dec_seq, -1))
            w_out = a_w_out
        elif kind == 1:
            lam_init = 0.8 - 0.6 * math.exp(-0.3 * i)
            lam = (jnp.exp(jnp.sum(b_lambda_q1[j].astype(F32) * b_lambda_k1[j].astype(F32)))
                   - jnp.exp(jnp.sum(b_lambda_q2[j].astype(F32) * b_lambda_k2[j].astype(F32)))
                   + lam_init)
            (q_p, k_p, v_p), (q_s, k_s, v_s) = _mixer_b_proj(
                xn_p, xn_s, b_w_in, j, b_q_norm[j], b_k_norm[j], heads=b_heads, tiles=tp)
            y_p = diff_attn_prompt(q_p, k_p, v_p, lam, b_subln[j], lam_init,
                                   batch=batch, seq=seq, heads=b_heads, tq=tp["attn_tq"],
                                   hb=tp["attn_heads"])
            k_p_rows.append(k_p.reshape(batch, seq, b_heads, 2 * B_HEAD_DIM))
            v_p_rows.append(v_p.reshape(batch, seq, b_heads, B_V_DIM))

            q5 = (q_s *(B_HEAD_DIM ** -0.5)).reshape(dec_b, dec_seq, b_heads, 2, B_HEAD_DIM)
            qt = jnp.einsum("bthcd,ce->bhcted", q5, jnp.eye(2, dtype=F32))
            qt = qt.reshape(dec_b, b_heads // SUBLANES, SUBLANES * 2 * dec_seq, 2 * B_HEAD_DIM)
            qt = qt.transpose(0, 1, 3, 2)
            k_new = k_s.reshape(dec_b, dec_seq * b_heads, 2 * B_HEAD_DIM)
            v_new = v_s.reshape(dec_b, dec_seq * b_heads, B_V_DIM)
            o_s = diff_attn_decode(qt, cache_k2, cache_v2, page_table, k_new, v_new, lam,
                                   b_subln[j], lam_init, layer=j, heads=b_heads, n_q=dec_seq,
                                   pages=DECODE_PAGES_PER_STEP)
            y_s = o_s.transpose(0, 2, 1, 3).reshape(ms, d_model)
            k_s_rows.append(k_s.reshape(dec_b, dec_seq, b_heads, 2 * B_HEAD_DIM))
            v_s_rows.append(v_s.reshape(dec_b, dec_seq, b_heads, B_V_DIM))
            w_out = b_w_out
        else:
            qkv_p, qkv_s = _mixer_c_proj(xn_p, xn_s, c_w_in, j, lower_bound[i], tiles=tp)
            y_p, st_p = hgrn_recurrence(*qkv_p, c_g_norm[j], None, batch=batch, seq=seq,
                                        heads=c_heads, rows=tp["hgrn_rows"], chunk=C_HEAD_DIM,
                                        hb=tp["hgrn_heads"], out_dtype=BF16)
            y_s, st_s = hgrn_recurrence(*qkv_s, c_g_norm[j], state_hgrn[j], batch=dec_b,
                                        seq=dec_seq, heads=c_heads, rows=dec_seq, chunk=dec_seq,
                                        hb=ts["hgrn_heads"], out_dtype=F32)
            hgrn_p.append(st_p)
            hgrn_s.append(st_s)
            w_out = c_w_out
        h_p, xf_p, h_s, xf_s = out_proj(y_p, y_s, w_out, j, h_p, h_s, norm_ffn[i], tm=tp["out_tm"])
        gain_next = norm_mix[i + 1] if i + 1 < depth else None
        h_p, h_s, xn_p, xn_s = ffn(xf_p, h_p, xf_s, h_s, ffn_w_gu, ffn_w_down, i, gain_next,
                                   tm=tp["ffn_tm"], th=tp["ffn_th"])
    return (h_p.reshape(batch, seq, d_model), h_s.reshape(dec_b, dec_seq, d_model),
            jnp.stack(k_p_rows), jnp.stack(v_p_rows), jnp.stack(k_s_rows), jnp.stack(v_s_rows),
            jnp.stack(hgrn_p), jnp.stack(hgrn_s), jnp.stack(chunk_v_s))
```

```python
import functools
import math

import jax
import jax.numpy as jnp
import numpy as np
from jax import lax
from jax.experimental import pallas as pl
from jax.experimental.pallas import tpu as pltpu

F32 = jnp.float32
BF16 = jnp.bfloat16
EPS = 1e-6

LANES = 128
SUBLANES = 8
VMEM_LIMIT_BYTES = 56 << 20
FFN_VMEM_LIMIT_BYTES = 60 << 20
FFN_RESIDUAL_SLICES = 8

A_CHUNK = 128
A_GROUPS = 8
B_HEAD_DIM = 64
B_V_DIM = 2 * B_HEAD_DIM
C_HEAD_DIM = 128
PAGE_SIZE = 128
HGRN_SUB = SUBLANES
DECODE_PAGES_PER_STEP = 8


def _cparams(n_axes):
    return pltpu.CompilerParams(
        dimension_semantics=("arbitrary",) * n_axes,
        vmem_limit_bytes=VMEM_LIMIT_BYTES,
    )


def _rmsnorm_f32(x, gain):
    return x * lax.rsqrt(jnp.mean(x * x, axis=-1, keepdims=True) + EPS) * gain


def _sigmoid(x):
    return 1.0 / (1.0 + jnp.exp(-x))


def _norm_rows_body(x_ref, g_ref, o_ref):
    o_ref[...] = _rmsnorm_f32(x_ref[...], g_ref[...]).astype(o_ref.dtype)


def norm_rows(x, gain, *, tm):
    m, d = x.shape
    return pl.pallas_call(
        _norm_rows_body,
        grid=(m // tm,),
        in_specs=[pl.BlockSpec((tm, d), lambda i: (i, 0)),
                  pl.BlockSpec((1, d), lambda i: (0, 0))],
        out_specs=pl.BlockSpec((tm, d), lambda i: (i, 0)),
        out_shape=jax.ShapeDtypeStruct((m, d), BF16),
        compiler_params=_cparams(1),
        name="norm_rows",
    )(x, gain.reshape(1, d))


def _seg_matmul_body(*refs, ns, nv, nc, no, epilogue):
    x_ref, xs_ref = refs[0:2]
    w_refs = refs[2:2 + ns]
    vec_refs = refs[2 + ns:2 + ns + nv]
    const_refs = refs[2 + ns + nv:2 + ns + nv + nc]
    out_refs = refs[2 + ns + nv + nc:2 + ns + nv + nc + no]
    sample_out_refs = refs[2 + ns + nv + nc + no:-1]
    wb_ref = refs[-1]
    first_row_tile = pl.program_id(1) == 0

    @pl.when(first_row_tile)
    def _():
        for s in range(ns):
            wb_ref[s] = w_refs[s][...].astype(BF16)

    def project(rows_ref, dst_refs):
        x = rows_ref[...]
        accs = [jnp.dot(x, wb_ref[s], preferred_element_type=F32) for s in range(ns)]
        outs = epilogue(accs, [r[...] for r in vec_refs], [r[...] for r in const_refs])
        for r, o in zip(dst_refs, outs):
            r[...] = o.astype(r.dtype)

    project(x_ref, out_refs)

    @pl.when(first_row_tile)
    def _():
        project(xs_ref, sample_out_refs)


def seg_matmul(x, xs, w, layer, seg_starts, seg_width, epilogue, out_dtypes, vecs=(), consts=(),
               *, tm, tn, name):
    m, k = x.shape
    ms = xs.shape[0]
    ns = len(seg_starts)
    no = len(out_dtypes)
    in_specs = [pl.BlockSpec((tm, k), lambda j, i: (i, 0)),
                pl.BlockSpec((ms, k), lambda j, i: (0, 0))]
    for st in seg_starts:
        in_specs.append(pl.BlockSpec((None, k, tn), lambda j, i, off=st // tn: (layer, 0, off + j)))
    for _ in vecs:
        in_specs.append(pl.BlockSpec((1, tn), lambda j, i: (0, j)))
    for c in consts:
        in_specs.append(pl.BlockSpec(c.shape, lambda j, i, nd=c.ndim: (0,) * nd))
    res = pl.pallas_call(
        functools.partial(_seg_matmul_body, ns=ns, nv=len(vecs), nc=len(consts), no=no,
                          epilogue=epilogue),
        grid=(seg_width // tn, m // tm),
        in_specs=in_specs,
        out_specs=([pl.BlockSpec((tm, tn), lambda j, i: (i, j)) for _ in out_dtypes]
                   + [pl.BlockSpec((ms, tn), lambda j, i: (0, j)) for _ in out_dtypes]),
        out_shape=([jax.ShapeDtypeStruct((m, seg_width), dt) for dt in out_dtypes]
                   + [jax.ShapeDtypeStruct((ms, seg_width), dt) for dt in out_dtypes]),
        scratch_shapes=[pltpu.VMEM((ns, k, tn), BF16)],
        compiler_params=_cparams(2),
        name=name,
    )(x, xs, *([w] * ns), *vecs, *consts)
    return res[:no], res[no:]


def _gelu_exact_f32(a):
    z = a * (2.0 ** -0.5)
    az = jnp.abs(z)
    t = 1.0 / (1.0 + 0.3275911 * az)
    poly = t * (0.254829592 + t * (-0.284496736 + t * (1.421413741
                                                       + t * (-1.453152027 + t * 1.061405429))))
    erfc_abs = poly * jnp.exp(-az * az)
    return 0.5 * a * jnp.where(z >= 0, 2.0 - erfc_abs, erfc_abs)


def _gelu_epilogue(accs, vecs, consts):
    (a,) = accs
    return [_gelu_exact_f32(a)]


def _headnorm_epilogue(accs, vecs, consts):
    aq, ak, av = accs
    gq, gk = vecs
    (group_ones,) = consts

    def head_norm(a, g):
        ms = jnp.dot((a * a).astype(BF16), group_ones, preferred_element_type=F32) * (1.0 / B_HEAD_DIM)
        return a * lax.rsqrt(ms + EPS) * g

    return [head_norm(aq, gq), head_norm(ak, gk), av]


def _hgrn_gate_epilogue(accs, vecs, consts):
    aq, af, av, ag = accs
    (lb,) = vecs
    q = aq * _sigmoid(aq)
    f = lb + (1.0 - lb) * _sigmoid(af)
    k = (1.0 - lb) * _sigmoid(-af)
    return [q, k, jnp.log(f), av, ag]


def _out_proj_body(y_ref, ys_ref, w_ref, h_ref, hs_ref, g_ref, ho_ref, xo_ref, hso_ref, xso_ref,
                   wb_ref, *, cast_rows):
    first_row_tile = pl.program_id(0) == 0

    @pl.when(first_row_tile)
    def _():
        def cast(r, carry):
            sl = pl.ds(pl.multiple_of(r * cast_rows, cast_rows), cast_rows)
            wb_ref[sl, :] = w_ref[sl, :].astype(BF16)
            return carry
        lax.fori_loop(0, w_ref.shape[0] // cast_rows, cast, 0)

    def project(rows_ref, res_ref, h_out_ref, x_out_ref):
        hn = res_ref[...] + jnp.dot(rows_ref[...].astype(BF16), wb_ref[...],
                                    preferred_element_type=F32)
        h_out_ref[...] = hn
        x_out_ref[...] = _rmsnorm_f32(hn, g_ref[...]).astype(x_out_ref.dtype)

    project(y_ref, h_ref, ho_ref, xo_ref)

    @pl.when(first_row_tile)
    def _():
        project(ys_ref, hs_ref, hso_ref, xso_ref)


def out_proj(y, ys, w, layer, h, hs, gain_next, *, tm):
    m, k = y.shape
    ms = ys.shape[0]
    n = w.shape[2]
    row = lambda cols: pl.BlockSpec((tm, cols), lambda i: (i, 0))
    sample = lambda cols: pl.BlockSpec((ms, cols), lambda i: (0, 0))
    return pl.pallas_call(
        functools.partial(_out_proj_body, cast_rows=256),
        grid=(m // tm,),
        in_specs=[row(k), sample(k),
                  pl.BlockSpec((None, k, n), lambda i: (layer, 0, 0), pipeline_mode=pl.Buffered(1)),
                  row(n), sample(n),
                  pl.BlockSpec((1, n), lambda i: (0, 0))],
        out_specs=[row(n), row(n), sample(n), sample(n)],
        out_shape=[jax.ShapeDtypeStruct((m, n), F32), jax.ShapeDtypeStruct((m, n), BF16),
                   jax.ShapeDtypeStruct((ms, n), F32), jax.ShapeDtypeStruct((ms, n), BF16)],
        scratch_shapes=[pltpu.VMEM((k, n), BF16)],
        compiler_params=_cparams(1),
        name="out_proj",
    )(y, ys, w, h, hs, gain_next.reshape(1, n))


def _ffn_body(x_ref, h_ref, xs_ref, hs_ref, wg_ref, wu_ref, wd_ref, g_ref, *out_refs, n_t, with_norm):
    if with_norm:
        ho_ref, hso_ref, xo_ref, xso_ref = out_refs
    else:
        ho_ref, hso_ref = out_refs
    i = pl.program_id(0)
    t = pl.program_id(1)

    @pl.when(t == 0)
    def _():
        ho_ref[...] = jnp.zeros(ho_ref.shape, F32)

    slice_rows = h_ref.shape[0]
    for p in range(FFN_RESIDUAL_SLICES):
        @pl.when(t == p)
        def _(p=p):
            ho_ref[p * slice_rows:(p + 1) * slice_rows, :] += h_ref[...]

    def swiglu(x):
        gate = jnp.dot(x, wg_ref[...].astype(BF16), preferred_element_type=F32)
        up = jnp.dot(x, wu_ref[...].astype(BF16), preferred_element_type=F32)
        act = (gate * _sigmoid(gate) * up).astype(BF16)
        return jnp.dot(act, wd_ref[...].astype(BF16), preferred_element_type=F32)

    ho_ref[...] += swiglu(x_ref[...])

    @pl.when((i == 0) & (t == 0))
    def _():
        hso_ref[...] = hs_ref[...]

    @pl.when(i == 0)
    def _():
        hso_ref[...] += swiglu(xs_ref[...])

    if with_norm:
        @pl.when(t == n_t - 1)
        def _():
            xo_ref[...] = _rmsnorm_f32(ho_ref[...], g_ref[...]).astype(BF16)

        @pl.when((i == 0) & (t == n_t - 1))
        def _():
            xso_ref[...] = _rmsnorm_f32(hso_ref[...], g_ref[...]).astype(BF16)


def ffn(x, h, xs, hs, w_gu, w_down, layer, gain_next, *, tm, th):
    m, d = x.shape
    ms = xs.shape[0]
    hidden = w_down.shape[1]
    n_t = hidden // th
    with_norm = gain_next is not None
    gain = gain_next if with_norm else jnp.ones((d,), F32)
    row_out = pl.BlockSpec((tm, d), lambda i, t: (i, 0))
    sample_block = pl.BlockSpec((ms, d), lambda i, t: (0, 0))
    out_specs = [row_out, sample_block]
    out_shape = [jax.ShapeDtypeStruct((m, d), F32), jax.ShapeDtypeStruct((ms, d), F32)]
    if with_norm:
        out_specs += [row_out, sample_block]
        out_shape += [jax.ShapeDtypeStruct((m, d), BF16), jax.ShapeDtypeStruct((ms, d), BF16)]
    assert n_t >= FFN_RESIDUAL_SLICES and tm % (FFN_RESIDUAL_SLICES * SUBLANES) == 0
    row_block = pl.BlockSpec((tm, d), lambda i, t: (i, 0))
    last_slice = FFN_RESIDUAL_SLICES - 1
    residual_slice = pl.BlockSpec(
        (tm // FFN_RESIDUAL_SLICES, d),
        lambda i, t: (i * FFN_RESIDUAL_SLICES + jnp.minimum(t, last_slice), 0))
    res = pl.pallas_call(
        functools.partial(_ffn_body, n_t=n_t, with_norm=with_norm),
        grid=(m // tm, n_t),
        in_specs=[row_block, residual_slice, sample_block, sample_block,
                  pl.BlockSpec((None, d, th), lambda i, t: (layer, 0, t)),
                  pl.BlockSpec((None, d, th), lambda i, t: (layer, 0, n_t + t)),
                  pl.BlockSpec((None, th, d), lambda i, t: (layer, t, 0)),
                  pl.BlockSpec((1, d), lambda i, t: (0, 0))],
        out_specs=out_specs,
        out_shape=out_shape,
        compiler_params=pltpu.CompilerParams(dimension_semantics=("arbitrary",) * 2,
                                             vmem_limit_bytes=FFN_VMEM_LIMIT_BYTES),
        name="ffn",
    )(x, h, xs, hs, w_gu, w_gu, w_down, gain.reshape(1, d))
    return tuple(res) if with_norm else (res[0], res[1], None, None)


def _spatial_body(u_ref, v_ref, wm_ref, bs_ref, vg_ref, p_ref, *maybe_vn_ref, chunk, groups):
    v = v_ref[...].astype(F32)
    vn = _rmsnorm_f32(v, vg_ref[...])
    if maybe_vn_ref:
        maybe_vn_ref[0][...] = vn
    vnb = vn.astype(BF16)
    rows, width = v.shape
    gw = width // groups
    for c in range(rows // chunk):
        r0 = c * chunk
        for g in range(groups):
            c0 = g * gw
            s = jnp.dot(wm_ref[g], vnb[r0:r0 + chunk, c0:c0 + gw], preferred_element_type=F32)
            s = s + bs_ref[g]
            u = u_ref[r0:r0 + chunk, c0:c0 + gw].astype(F32)
            p_ref[r0:r0 + chunk, c0:c0 + gw] = (u * s).astype(p_ref.dtype)


def spatial_mix(uv, wm, bs, v_gain, *, chunk, tm, with_vn):
    m, w2 = uv.shape
    width = w2 // 2
    groups = wm.shape[0]
    row_out = pl.BlockSpec((tm, width), lambda i: (i, 0))
    out_specs = [row_out]
    out_shape = [jax.ShapeDtypeStruct((m, width), BF16)]
    if with_vn:
        out_specs.append(row_out)
        out_shape.append(jax.ShapeDtypeStruct((m, width), F32))
    res = pl.pallas_call(
        functools.partial(_spatial_body, chunk=chunk, groups=groups),
        grid=(m // tm,),
        in_specs=[pl.BlockSpec((tm, width), lambda i: (i, 0)),
                  pl.BlockSpec((tm, width), lambda i: (i, 1)),
                  pl.BlockSpec(wm.shape, lambda i: (0, 0, 0)),
                  pl.BlockSpec(bs.shape, lambda i: (0, 0, 0)),
                  pl.BlockSpec((1, width), lambda i: (0, 0))],
        out_specs=out_specs,
        out_shape=out_shape,
        compiler_params=_cparams(1),
        name="spatial_mix",
    )(uv, uv, wm, bs, v_gain.reshape(1, width))
    return (res[0], res[1]) if with_vn else (res[0], None)


def _diff_attn_body(qt_tab, kt_tab, q_ref, k_ref, v_ref, lam_ref, sub_ref, o_ref,
                    qs_ref, m_ref, l_ref, acc_ref, *, tq, hb, out_scale):
    t = pl.program_id(2)
    qi = qt_tab[t]
    ki = kt_tab[t]
    hd = 2 * B_HEAD_DIM

    @pl.when(ki == 0)
    def _():
        for h in range(hb):
            q = q_ref[:, h * hd:(h + 1) * hd].astype(F32) * (B_HEAD_DIM ** -0.5)
            lane = lax.broadcasted_iota(jnp.int32, q.shape, 1)
            qs_ref[h, 0:tq, :] = jnp.where(lane < B_HEAD_DIM, q, 0.0)
            qs_ref[h, tq:2 * tq, :] = jnp.where(lane >= B_HEAD_DIM, q, 0.0)
        m_ref[...] = jnp.full(m_ref.shape, -jnp.inf, F32)
        l_ref[...] = jnp.zeros(l_ref.shape, F32)
        acc_ref[...] = jnp.zeros(acc_ref.shape, F32)

    def strip(h, r0, masked):
        rs = pl.ds(r0, tq)
        keys = k_ref[:, h * hd:(h + 1) * hd]
        vals = v_ref[:, h * B_V_DIM:(h + 1) * B_V_DIM]
        s = lax.dot_general(qs_ref[h, rs, :], keys, (((1,), (1,)), ((), ())),
                            preferred_element_type=F32)
        if masked:
            row = lax.broadcasted_iota(jnp.int32, s.shape, 0)
            col = lax.broadcasted_iota(jnp.int32, s.shape, 1)
            s = jnp.where(col <= row, s, -jnp.inf)
        m_prev = m_ref[h, rs, :]
        m_new = jnp.maximum(m_prev, jnp.max(s, axis=1, keepdims=True))
        alpha = jnp.exp(m_prev - m_new)
        p = jnp.exp(s - m_new[:, 0:1])
        l_ref[h, rs, :] = alpha * l_ref[h, rs, :] + jnp.sum(p, axis=1, keepdims=True)
        acc_ref[h, rs, :] = alpha * acc_ref[h, rs, :] + jnp.dot(p, vals, preferred_element_type=F32)
        m_ref[h, rs, :] = m_new

    @pl.when(ki < qi)
    def _():
        for h in range(hb):
            strip(h, 0, False)
            strip(h, tq, False)

    @pl.when(ki == qi)
    def _():
        for h in range(hb):
            strip(h, 0, True)
            strip(h, tq, True)
            o = acc_ref[h] / l_ref[h]
            d = o[0:tq] - lam_ref[0] * o[tq:2 * tq]
            o_ref[:, h * B_V_DIM:(h + 1) * B_V_DIM] = (
                _rmsnorm_f32(d, sub_ref[...]) * out_scale).astype(o_ref.dtype)


def diff_attn_prompt(q, k, v, lam, subln, lam_init, *, batch, seq, heads, tq, hb):
    nq = seq // tq
    tri = [(qi, ki) for qi in range(nq) for ki in range(qi + 1)]
    qt_tab = jnp.asarray([a for a, _ in tri], jnp.int32)
    kt_tab = jnp.asarray([b for _, b in tri], jnp.int32)
    hd = 2 * B_HEAD_DIM
    grid_spec = pltpu.PrefetchScalarGridSpec(
        num_scalar_prefetch=2,
        grid=(batch, heads // hb, len(tri)),
        in_specs=[pl.BlockSpec((tq, hb * hd), lambda b, h, t, qt, kt: (b * nq + qt[t], h)),
                  pl.BlockSpec((tq, hb * hd), lambda b, h, t, qt, kt: (b * nq + kt[t], h)),
                  pl.BlockSpec((tq, hb * B_V_DIM), lambda b, h, t, qt, kt: (b * nq + kt[t], h)),
                  pl.BlockSpec(memory_space=pltpu.SMEM),
                  pl.BlockSpec((1, B_V_DIM), lambda b, h, t, qt, kt: (0, 0))],
        out_specs=pl.BlockSpec((tq, hb * B_V_DIM), lambda b, h, t, qt, kt: (b * nq + qt[t], h)),
        scratch_shapes=[pltpu.VMEM((hb, 2 * tq, hd), F32),
                        pltpu.VMEM((hb, 2 * tq, LANES), F32),
                        pltpu.VMEM((hb, 2 * tq, LANES), F32),
                        pltpu.VMEM((hb, 2 * tq, B_V_DIM), F32)],
    )
    return pl.pallas_call(
        functools.partial(_diff_attn_body, tq=tq, hb=hb, out_scale=1.0 - lam_init),
        grid_spec=grid_spec,
        out_shape=jax.ShapeDtypeStruct((batch * seq, heads * B_V_DIM), BF16),
        compiler_params=_cparams(3),
        name="diff_attn_prompt",
    )(qt_tab, kt_tab, q, k, v, lam.reshape(1), subln.reshape(1, B_V_DIM))


def _decode_attn_body(pt_ref, *refs, pages, heads, n_q, n_groups, out_scale):
    k_refs = refs[:pages]
    v_refs = refs[pages:2 * pages]
    kn_ref, vn_ref, qt_ref, lam_ref, sub_ref, o_ref, m_ref, l_ref, acc_ref = refs[2 * pages:]
    g = pl.program_id(1)
    hg = heads // SUBLANES
    hc = 2 * n_q
    cols = SUBLANES * hc

    @pl.when(g == 0)
    def _():
        m_ref[...] = jnp.full(m_ref.shape, -jnp.inf, F32)
        l_ref[...] = jnp.zeros(l_ref.shape, F32)
        acc_ref[...] = jnp.zeros(acc_ref.shape, F32)

    sub = lax.broadcasted_iota(jnp.int32, (SUBLANES, cols), 0)
    lane = lax.broadcasted_iota(jnp.int32, (SUBLANES, cols), 1)
    own = sub == lane // hc

    def to_column(x8):
        r = jnp.sum(jnp.where(own, x8, 0.0), axis=0, keepdims=True)
        return jnp.broadcast_to(r, (LANES, cols)).T

    def group_rows(ref, j, n_pos):
        x = ref[0:n_pos * heads, :].reshape(n_pos, hg, SUBLANES, 2 * B_HEAD_DIM)
        return x[:, j].reshape(n_pos * SUBLANES, 2 * B_HEAD_DIM)

    def process(page_k_refs, page_v_refs, n_pos, new_tokens):
        for j in range(hg):
            scores = []
            for k_ref in page_k_refs:
                s = jnp.dot(group_rows(k_ref, j, n_pos), qt_ref[j], preferred_element_type=F32)
                s = s.reshape(n_pos, SUBLANES, cols)
                valid = own[None]
                if new_tokens:
                    pos = lax.broadcasted_iota(jnp.int32, s.shape, 0)
                    qry = lax.broadcasted_iota(jnp.int32, s.shape, 2) % n_q
                    valid = valid & (pos <= qry)
                scores.append(jnp.where(valid, s, -jnp.inf))
            m_prev = m_ref[j]
            m_new = m_prev
            for s in scores:
                m_new = jnp.maximum(m_new, jnp.max(s, axis=0))
            m_safe = jnp.where(own, m_new, 0.0)
            alpha = jnp.exp(m_prev - m_safe)
            l_new = alpha * l_ref[j]
            pv = jnp.zeros((cols, B_V_DIM), F32)
            for s, v_ref in zip(scores, page_v_refs):
                p = jnp.exp(s - m_safe[None])
                l_new = l_new + jnp.sum(p, axis=0)
                pv = pv + lax.dot_general(p.reshape(n_pos * SUBLANES, cols), group_rows(v_ref, j, n_pos),
                                          (((0,), (0,)), ((), ())), preferred_element_type=F32)
            acc_ref[j] = acc_ref[j] * to_column(alpha) + pv
            l_ref[j] = l_new
            m_ref[j] = m_new

    process(k_refs, v_refs, PAGE_SIZE, False)

    @pl.when(g == n_groups - 1)
    def _():
        process([kn_ref], [vn_ref], n_q, True)
        for j in range(hg):
            o = acc_ref[j] / to_column(l_ref[j])
            o = o.reshape(SUBLANES, 2, n_q, B_V_DIM)
            d = o[:, 0] - lam_ref[0] * o[:, 1]
            d = d * lax.rsqrt(jnp.mean(d * d, axis=-1, keepdims=True) + EPS) * sub_ref[...]
            o_ref[j * SUBLANES:(j + 1) * SUBLANES] = (d * out_scale).astype(o_ref.dtype)


def diff_attn_decode(qt, cache_k, cache_v, page_table, k_new, v_new, lam, subln, lam_init,
                     *, layer, heads, n_q, pages):
    dec_b, n_pages = page_table.shape
    hd = 2 * B_HEAD_DIM
    rows = PAGE_SIZE * heads
    hg = heads // SUBLANES
    cols = SUBLANES * 2 * n_q
    n_groups = n_pages // pages
    page_spec = lambda p_i: pl.BlockSpec(
        (None, None, rows, hd),
        lambda b, g, pt, p_i=p_i: (layer, pt[b, g * pages + p_i], 0, 0))
    grid_spec = pltpu.PrefetchScalarGridSpec(
        num_scalar_prefetch=1,
        grid=(dec_b, n_groups),
        in_specs=([page_spec(p_i) for p_i in range(pages)] * 2
                  + [pl.BlockSpec((None, n_q * heads, hd), lambda b, g, pt: (b, 0, 0)),
                     pl.BlockSpec((None, n_q * heads, hd), lambda b, g, pt: (b, 0, 0)),
                     pl.BlockSpec((None, hg, hd, cols), lambda b, g, pt: (b, 0, 0, 0)),
                     pl.BlockSpec(memory_space=pltpu.SMEM),
                     pl.BlockSpec((1, B_V_DIM), lambda b, g, pt: (0, 0))]),
        out_specs=pl.BlockSpec((None, heads, n_q, B_V_DIM), lambda b, g, pt: (b, 0, 0, 0)),
        scratch_shapes=[pltpu.VMEM((hg, SUBLANES, cols), F32),
                        pltpu.VMEM((hg, SUBLANES, cols), F32),
                        pltpu.VMEM((hg, cols, B_V_DIM), F32)],
    )
    return pl.pallas_call(
        functools.partial(_decode_attn_body, pages=pages, heads=heads, n_q=n_q,
                          n_groups=n_groups, out_scale=1.0 - lam_init),
        grid_spec=grid_spec,
        out_shape=jax.ShapeDtypeStruct((dec_b, heads, n_q, B_V_DIM), F32),
        compiler_params=_cparams(2),
        name="diff_attn_decode",
    )(page_table, *([cache_k] * pages), *([cache_v] * pages), k_new, v_new, qt,
      lam.reshape(1), subln.reshape(1, B_V_DIM))


def _cumsum_rows(x):
    c = x.shape[0]
    sub = lax.broadcasted_iota(jnp.int32, x.shape, 0) % HGRN_SUB
    d = 1
    while d < HGRN_SUB:
        x = x + jnp.where(sub >= d, pltpu.roll(x, d, axis=0), 0.0)
        d *= 2
    blocks = []
    carry = None
    for j in range(c // HGRN_SUB):
        blk = x[j * HGRN_SUB:(j + 1) * HGRN_SUB]
        if carry is not None:
            blk = blk + carry
        blocks.append(blk)
        carry = blk[HGRN_SUB - 1:HGRN_SUB]
    return jnp.concatenate(blocks, axis=0) if len(blocks) > 1 else blocks[0]


def _hgrn_chunk(q, k, lf, v, st, row_scr):
    c = q.shape[0]
    nb = c // HGRN_SUB
    gcum = _cumsum_rows(lf)
    row_scr[0] = gcum
    row_scr[1] = k
    row_scr[2] = v
    o = lax.dot_general(q * jnp.exp(gcum), st, (((1,), (1,)), ((), ())), preferred_element_type=F32)

    if nb > 1:
        row = lax.broadcasted_iota(jnp.int32, (c, C_HEAD_DIM), 0)
        t_idx = lax.broadcasted_iota(jnp.int32, (c, c), 0)
        s_idx = lax.broadcasted_iota(jnp.int32, (c, c), 1)
        a_off = None
        size = 2 * HGRN_SUB
        while size <= c:
            half = size // 2
            if size < c:
                g_mid = jnp.concatenate(
                    [jnp.broadcast_to(gcum[b0 + half - 1:b0 + half], (size, C_HEAD_DIM))
                     for b0 in range(0, c, size)], axis=0)
            else:
                g_mid = gcum[half - 1:half]
            upper = (row % size) >= half
            qd = q * jnp.exp(jnp.where(upper, gcum - g_mid, -jnp.inf))
            kd = k * jnp.exp(jnp.where(upper, -jnp.inf, g_mid - gcum))
            a = lax.dot_general(qd, kd, (((1,), (1,)), ((), ())), preferred_element_type=F32)
            if size < c:
                a = jnp.where((t_idx // size) == (s_idx // size), a, 0.0)
            a_off = a if a_off is None else a_off + a
            size *= 2
        o = o + jnp.dot(a_off, v, preferred_element_type=F32)

    sub_row = lax.broadcasted_iota(jnp.int32, (HGRN_SUB, C_HEAD_DIM), 0)
    o_blocks = []
    for i in range(nb):
        r0 = i * HGRN_SUB
        gi = gcum[r0:r0 + HGRN_SUB]
        qi = q[r0:r0 + HGRN_SUB]
        oi = o[r0:r0 + HGRN_SUB]
        for s in range(HGRN_SUB):
            r = r0 + s
            dec = jnp.exp(jnp.where(sub_row >= s, gi - row_scr[0, r:r + 1, :], -jnp.inf))
            a_col = jnp.sum(qi * row_scr[1, r:r + 1, :] * dec, axis=-1, keepdims=True)
            oi = oi + a_col * row_scr[2, r:r + 1, :]
        o_blocks.append(oi)
    o = jnp.concatenate(o_blocks, axis=0) if nb > 1 else o_blocks[0]
    g_last = gcum[c - 1:c]
    kd = k * jnp.exp(g_last - gcum)
    if c < C_HEAD_DIM:
        pad = jnp.zeros((C_HEAD_DIM - c, C_HEAD_DIM), F32)
        kd = jnp.concatenate([kd, pad], axis=0)
        v = jnp.concatenate([v, pad], axis=0)
    st_new = st * jnp.exp(g_last) + jnp.dot(v.T, kd, preferred_element_type=F32)
    return o, st_new


def _hgrn_body(*refs, chunk, n_chunks, n_r, hb, with_state):
    if with_state:
        q_ref, k_ref, lf_ref, v_ref, g_ref, gg_ref, s0_ref, o_ref, so_ref, st_ref, gs_ref = refs
    else:
        q_ref, k_ref, lf_ref, v_ref, g_ref, gg_ref, o_ref, so_ref, st_ref, gs_ref = refs
    r = pl.program_id(2)
    hd = C_HEAD_DIM

    @pl.when(r == 0)
    def _():
        for h in range(hb):
            if with_state:
                st_ref[h] = s0_ref[h].astype(F32).T
            else:
                st_ref[h] = jnp.zeros((hd, hd), F32)

    def step(ci, carry):
        base = pl.multiple_of(ci * chunk, chunk)
        sl = pl.ds(base, chunk)
        for h in range(hb):
            cs = slice(h * hd, (h + 1) * hd)
            o, st_new = _hgrn_chunk(q_ref[sl, cs], k_ref[sl, cs], lf_ref[sl, cs], v_ref[sl, cs],
                                    st_ref[h], gs_ref.at[h])
            st_ref[h] = st_new
            gate = g_ref[sl, cs]
            o = _rmsnorm_f32(o, gg_ref[...]) * (gate * _sigmoid(gate))
            o_ref[sl, cs] = o.astype(o_ref.dtype)
        return carry

    lax.fori_loop(0, n_chunks, step, 0)

    @pl.when(r == n_r - 1)
    def _():
        for h in range(hb):
            so_ref[h] = st_ref[h].T.astype(so_ref.dtype)


def hgrn_recurrence(q, k, lf, v, g, g_gain, state0, *, batch, seq, heads, rows, chunk, hb, out_dtype):
    n_r = seq // rows
    hd = C_HEAD_DIM
    with_state = state0 is not None
    row_spec = pl.BlockSpec((rows, hb * hd), lambda b, h, r: (b * n_r + r, h))
    state_spec = pl.BlockSpec((None, hb, hd, hd), lambda b, h, r: (b, h, 0, 0))
    in_specs = [row_spec] * 5 + [pl.BlockSpec((1, hd), lambda b, h, r: (0, 0))]
    args = [q, k, lf, v, g, g_gain.reshape(1, hd)]
    if with_state:
        in_specs.append(state_spec)
        args.append(state0)
    return pl.pallas_call(
        functools.partial(_hgrn_body, chunk=chunk, n_chunks=rows // chunk, n_r=n_r, hb=hb,
                          with_state=with_state),
        grid=(batch, heads // hb, n_r),
        in_specs=in_specs,
        out_specs=[row_spec, state_spec],
        out_shape=[jax.ShapeDtypeStruct((batch * seq, heads * hd), out_dtype),
                   jax.ShapeDtypeStruct((batch, heads, hd, hd), F32)],
        scratch_shapes=[pltpu.VMEM((hb, hd, hd), F32), pltpu.VMEM((hb, 3, chunk, hd), F32)],
        compiler_params=_cparams(3),
        name="hgrn_recurrence",
    )(*args)


def _tiles(m):
    big = m >= 1024
    return dict(
        norm_tm=512 if big else m,
        proj_tm=1024 if big else m,
        proj_tn=256,
        a_proj_tn=1024,
        out_tm=512 if big else m,
        ffn_tm=1024 if big else m,
        ffn_th=256,
        spatial_tm=256 if big else m,
        attn_tq=512,
        hgrn_rows=512,
        attn_heads=8,
        hgrn_heads=2 if big else 8,
    )


def _mixer_a_core(uv, v_gain, w_s, b_s, *, chunk_len, n_seq, tiles, with_vn):
    causal = jnp.tril(jnp.ones((A_CHUNK, A_CHUNK), bool))
    w_masked = jnp.where(causal[None], w_s, 0.0)
    if chunk_len == A_CHUNK:
        wm, bs, chunk = w_masked, b_s, A_CHUNK
    else:
        eye = jnp.eye(n_seq, dtype=w_s.dtype)
        small = w_masked[:, :chunk_len, :chunk_len]
        wm = jnp.einsum("ab,gts->gatbs", eye, small).reshape(
            A_GROUPS, n_seq * chunk_len, n_seq * chunk_len)
        bs = jnp.tile(b_s[:, :chunk_len], (1, n_seq))
        chunk = n_seq * chunk_len
    p, vn = spatial_mix(uv, wm.astype(BF16), bs[:, :, None], v_gain, chunk=chunk,
                        tm=max(tiles["spatial_tm"], chunk) if chunk_len == A_CHUNK else chunk,
                        with_vn=with_vn)
    return p, vn


def _mixer_a_proj(xn, xns, w_in, layer, *, tiles):
    width2 = w_in.shape[2]
    (uv,), (uv_s,) = seg_matmul(xn, xns, w_in, layer, (0,), width2, _gelu_epilogue, (BF16,),
                                tm=tiles["proj_tm"], tn=tiles["a_proj_tn"], name="a_in_proj")
    return uv, uv_s


def _mixer_b_proj(xn, xns, w_in, layer, q_gain, k_gain, *, heads, tiles):
    width = heads * 2 * B_HEAD_DIM
    tn = tiles["proj_tn"]
    lane_group = np.arange(tn) // B_HEAD_DIM
    group_ones = jnp.asarray(lane_group[:, None] == lane_group[None, :], BF16)
    reps = width // B_HEAD_DIM
    gq = jnp.tile(q_gain.astype(F32), reps).reshape(1, width)
    gk = jnp.tile(k_gain.astype(F32), reps).reshape(1, width)
    return seg_matmul(xn, xns, w_in, layer, (0, width, 2 * width), width, _headnorm_epilogue,
                      (F32, F32, F32), vecs=(gq, gk), consts=(group_ones,),
                      tm=tiles["proj_tm"], tn=tn, name="b_in_proj")


def _mixer_c_proj(xn, xns, w_in, layer, lower_bound, *, tiles):
    width = w_in.shape[2] // 4
    return seg_matmul(xn, xns, w_in, layer, (0, width, 2 * width, 3 * width), width, _hgrn_gate_epilogue,
                      (F32,) * 5, vecs=(lower_bound.reshape(1, width),),
                      tm=tiles["proj_tm"], tn=tiles["proj_tn"], name="c_in_proj")


def kernel(x_prompt, x_sample, cache_k, cache_v, page_table, state_hgrn, norm_mix, norm_ffn, ffn_w_gu, ffn_w_down, a_w_in, a_v_norm, a_w_s, a_b_s, a_w_out, b_w_in, b_q_norm, b_k_norm, b_lambda_q1, b_lambda_k1, b_lambda_q2, b_lambda_k2, b_subln, b_w_out, c_w_in, c_g_norm, c_lower_bounds, c_w_out):
    batch, seq, d_model = x_prompt.shape
    dec_b, dec_seq, _ = x_sample.shape
    depth = norm_mix.shape[0]
    b_heads = d_model // (2 * B_HEAD_DIM)
    c_heads = d_model // C_HEAD_DIM
    mp, ms = batch * seq, dec_b * dec_seq
    tp, ts = _tiles(mp), _tiles(ms)

    probs = jax.nn.softmax(c_lower_bounds.astype(F32), axis=0)
    lower_bound = jnp.cumsum(probs, axis=0) - probs[0]

    h_p = x_prompt.reshape(mp, d_model)
    h_s = x_sample.reshape(ms, d_model)
    xn_p = norm_rows(h_p, norm_mix[0], tm=tp["norm_tm"])
    xn_s = norm_rows(h_s, norm_mix[0], tm=ts["norm_tm"])

    n_phys = cache_k.shape[1]
    cache_k2 = cache_k.reshape(cache_k.shape[0], n_phys, PAGE_SIZE * b_heads, 2 * B_HEAD_DIM)
    cache_v2 = cache_v.reshape(cache_v.shape[0], n_phys, PAGE_SIZE * b_heads, B_V_DIM)

    k_p_rows, v_p_rows, k_s_rows, v_s_rows = [], [], [], []
    hgrn_p, hgrn_s, chunk_v_s = [], [], []
    for i in range(depth):
        kind, j = i % 3, i // 3
        if kind == 0:
            uv_p, uv_s = _mixer_a_proj(xn_p, xn_s, a_w_in, j, tiles=tp)
            y_p, _ = _mixer_a_core(uv_p, a_v_norm[j], a_w_s[j], a_b_s[j],
                                   chunk_len=A_CHUNK, n_seq=batch, tiles=tp, with_vn=False)
            y_s, vn_s = _mixer_a_core(uv_s, a_v_norm[j], a_w_s[j], a_b_s[j],
                                      chunk_len=dec_seq, n_seq=dec_b, tiles=ts, with_vn=True)
            chunk_v_s.append(vn_s.reshape(dec_b, dec_seq, -1))
            w_out = a_w_out
        elif kind == 1:
            lam_init = 0.8 - 0.6 * math.exp(-0.3 * i)
            lam = (jnp.exp(jnp.sum(b_lambda_q1[j].astype(F32) * b_lambda_k1[j].astype(F32)))
                   - jnp.exp(jnp.sum(b_lambda_q2[j].astype(F32) * b_lambda_k2[j].astype(F32)))
                   + lam_init)
            (q_p, k_p, v_p), (q_s, k_s, v_s) = _mixer_b_proj(
                xn_p, xn_s, b_w_in, j, b_q_norm[j], b_k_norm[j], heads=b_heads, tiles=tp)
            y_p = diff_attn_prompt(q_p, k_p, v_p, lam, b_subln[j], lam_init,
                                   batch=batch, seq=seq, heads=b_heads, tq=tp["attn_tq"],
                                   hb=tp["attn_heads"])
            k_p_rows.append(k_p.reshape(batch, seq, b_heads, 2 * B_HEAD_DIM))
            v_p_rows.append(v_p.reshape(batch, seq, b_heads, B_V_DIM))

            q5 = (q_s *(B_HEAD_DIM ** -0.5)).reshape(dec_b, dec_seq, b_heads, 2, B_HEAD_DIM)
            qt = jnp.einsum("bthcd,ce->bhcted", q5, jnp.eye(2, dtype=F32))
            qt = qt.reshape(dec_b, b_heads // SUBLANES, SUBLANES * 2 * dec_seq, 2 * B_HEAD_DIM)
            qt = qt.transpose(0, 1, 3, 2)
            k_new = k_s.reshape(dec_b, dec_seq * b_heads, 2 * B_HEAD_DIM)
            v_new = v_s.reshape(dec_b, dec_seq * b_heads, B_V_DIM)
            o_s = diff_attn_decode(qt, cache_k2, cache_v2, page_table, k_new, v_new, lam,
                                   b_subln[j], lam_init, layer=j, heads=b_heads, n_q=dec_seq,
                                   pages=DECODE_PAGES_PER_STEP)
            y_s = o_s.transpose(0, 2, 1, 3).reshape(ms, d_model)
            k_s_rows.append(k_s.reshape(dec_b, dec_seq, b_heads, 2 * B_HEAD_DIM))
            v_s_rows.append(v_s.reshape(dec_b, dec_seq, b_heads, B_V_DIM))
            w_out = b_w_out
        else:
            qkv_p, qkv_s = _mixer_c_proj(xn_p, xn_s, c_w_in, j, lower_bound[i], tiles=tp)
            y_p, st_p = hgrn_recurrence(*qkv_p, c_g_norm[j], None, batch=batch, seq=seq,
                                        heads=c_heads, rows=tp["hgrn_rows"], chunk=C_HEAD_DIM,
                                        hb=tp["hgrn_heads"], out_dtype=BF16)
            y_s, st_s = hgrn_recurrence(*qkv_s, c_g_norm[j], state_hgrn[j], batch=dec_b,
                                        seq=dec_seq, heads=c_heads, rows=dec_seq, chunk=dec_seq,
                                        hb=ts["hgrn_heads"], out_dtype=F32)
            hgrn_p.append(st_p)
            hgrn_s.append(st_s)
            w_out = c_w_out
        h_p, xf_p, h_s, xf_s = out_proj(y_p, y_s, w_out, j, h_p, h_s, norm_ffn[i], tm=tp["out_tm"])
        gain_next = norm_mix[i + 1] if i + 1 < depth else None
        h_p, h_s, xn_p, xn_s = ffn(xf_p, h_p, xf_s, h_s, ffn_w_gu, ffn_w_down, i, gain_next,
                                   tm=tp["ffn_tm"], th=tp["ffn_th"])
    return (h_p.reshape(batch, seq, d_model), h_s.reshape(dec_b, dec_seq, d_model),
            jnp.stack(k_p_rows), jnp.stack(v_p_rows), jnp.stack(k_s_rows), jnp.stack(v_s_rows),
            jnp.stack(hgrn_p), jnp.stack(hgrn_s), jnp.stack(chunk_v_s))
```

```python
import functools
import math

import jax
import jax.numpy as jnp
import numpy as np
from jax import lax
from jax.experimental import pallas as pl
from jax.experimental.pallas import tpu as pltpu

F32 = jnp.float32
BF16 = jnp.bfloat16
EPS = 1e-6

LANES = 128
SUBLANES = 8
VMEM_LIMIT_BYTES = 56 << 20
FFN_VMEM_LIMIT_BYTES = 60 << 20
FFN_RESIDUAL_SLICES = 8

A_CHUNK = 128
A_GROUPS = 8
B_HEAD_DIM = 64
B_V_DIM = 2 * B_HEAD_DIM
C_HEAD_DIM = 128
PAGE_SIZE = 128
HGRN_SUB = SUBLANES
DECODE_PAGES_PER_STEP = 8
ATTN_FIXED_SHIFT_MAX_BOUND = 30.0


def _cparams(n_axes):
    return pltpu.CompilerParams(
        dimension_semantics=("arbitrary",) * n_axes,
        vmem_limit_bytes=VMEM_LIMIT_BYTES,
    )


def _rmsnorm_f32(x, gain):
    return x * lax.rsqrt(jnp.mean(x * x, axis=-1, keepdims=True) + EPS) * gain


def _sigmoid(x):
    return 1.0 / (1.0 + jnp.exp(-x))


def _norm_rows_body(x_ref, g_ref, o_ref):
    o_ref[...] = _rmsnorm_f32(x_ref[...], g_ref[...]).astype(o_ref.dtype)


def norm_rows(x, gain, *, tm):
    m, d = x.shape
    return pl.pallas_call(
        _norm_rows_body,
        grid=(m // tm,),
        in_specs=[pl.BlockSpec((tm, d), lambda i: (i, 0)),
                  pl.BlockSpec((1, d), lambda i: (0, 0))],
        out_specs=pl.BlockSpec((tm, d), lambda i: (i, 0)),
        out_shape=jax.ShapeDtypeStruct((m, d), BF16),
        compiler_params=_cparams(1),
        name="norm_rows",
    )(x, gain.reshape(1, d))


def _seg_matmul_body(*refs, ns, nv, nc, no, epilogue):
    x_ref, xs_ref = refs[0:2]
    w_refs = refs[2:2 + ns]
    vec_refs = refs[2 + ns:2 + ns + nv]
    const_refs = refs[2 + ns + nv:2 + ns + nv + nc]
    out_refs = refs[2 + ns + nv + nc:2 + ns + nv + nc + no]
    sample_out_refs = refs[2 + ns + nv + nc + no:-1]
    wb_ref = refs[-1]
    first_row_tile = pl.program_id(1) == 0

    @pl.when(first_row_tile)
    def _():
        for s in range(ns):
            wb_ref[s] = w_refs[s][...].astype(BF16)

    def project(rows_ref, dst_refs):
        x = rows_ref[...]
        accs = [jnp.dot(x, wb_ref[s], preferred_element_type=F32) for s in range(ns)]
        outs = epilogue(accs, [r[...] for r in vec_refs], [r[...] for r in const_refs])
        for r, o in zip(dst_refs, outs):
            r[...] = o.astype(r.dtype)

    project(x_ref, out_refs)

    @pl.when(first_row_tile)
    def _():
        project(xs_ref, sample_out_refs)


def seg_matmul(x, xs, w, layer, seg_starts, seg_width, epilogue, out_dtypes, vecs=(), consts=(),
               *, tm, tn, name):
    m, k = x.shape
    ms = xs.shape[0]
    ns = len(seg_starts)
    no = len(out_dtypes)
    in_specs = [pl.BlockSpec((tm, k), lambda j, i: (i, 0)),
                pl.BlockSpec((ms, k), lambda j, i: (0, 0))]
    for st in seg_starts:
        in_specs.append(pl.BlockSpec((None, k, tn), lambda j, i, off=st // tn: (layer, 0, off + j)))
    for _ in vecs:
        in_specs.append(pl.BlockSpec((1, tn), lambda j, i: (0, j)))
    for c in consts:
        in_specs.append(pl.BlockSpec(c.shape, lambda j, i, nd=c.ndim: (0,) * nd))
    res = pl.pallas_call(
        functools.partial(_seg_matmul_body, ns=ns, nv=len(vecs), nc=len(consts), no=no,
                          epilogue=epilogue),
        grid=(seg_width // tn, m // tm),
        in_specs=in_specs,
        out_specs=([pl.BlockSpec((tm, tn), lambda j, i: (i, j)) for _ in out_dtypes]
                   + [pl.BlockSpec((ms, tn), lambda j, i: (0, j)) for _ in out_dtypes]),
        out_shape=([jax.ShapeDtypeStruct((m, seg_width), dt) for dt in out_dtypes]
                   + [jax.ShapeDtypeStruct((ms, seg_width), dt) for dt in out_dtypes]),
        scratch_shapes=[pltpu.VMEM((ns, k, tn), BF16)],
        compiler_params=_cparams(2),
        name=name,
    )(x, xs, *([w] * ns), *vecs, *consts)
    return res[:no], res[no:]


def _gelu_exact_f32(a):
    z = a * (2.0 ** -0.5)
    az = jnp.abs(z)
    t = 1.0 / (1.0 + 0.3275911 * az)
    poly = t * (0.254829592 + t * (-0.284496736 + t * (1.421413741
                                                       + t * (-1.453152027 + t * 1.061405429))))
    erfc_abs = poly * jnp.exp(-az * az)
    return 0.5 * a * jnp.where(z >= 0, 2.0 - erfc_abs, erfc_abs)


def _gelu_epilogue(accs, vecs, consts):
    (a,) = accs
    return [_gelu_exact_f32(a)]


def _headnorm_epilogue(accs, vecs, consts):
    aq, ak, av = accs
    gq, gk = vecs
    (group_ones,) = consts

    def head_norm(a, g):
        ms = jnp.dot((a * a).astype(BF16), group_ones, preferred_element_type=F32) * (1.0 / B_HEAD_DIM)
        return a * lax.rsqrt(ms + EPS) * g

    return [head_norm(aq, gq), head_norm(ak, gk), av]


def _hgrn_gate_epilogue(accs, vecs, consts):
    aq, af, av, ag = accs
    (lb,) = vecs
    q = aq * _sigmoid(aq)
    f = lb + (1.0 - lb) * _sigmoid(af)
    k = (1.0 - lb) * _sigmoid(-af)
    return [q, k, jnp.log(f), av, ag]


def _out_proj_body(y_ref, ys_ref, w_ref, h_ref, hs_ref, g_ref, ho_ref, xo_ref, hso_ref, xso_ref,
                   wb_ref, *, cast_rows):
    first_row_tile = pl.program_id(0) == 0

    @pl.when(first_row_tile)
    def _():
        def cast(r, carry):
            sl = pl.ds(pl.multiple_of(r * cast_rows, cast_rows), cast_rows)
            wb_ref[sl, :] = w_ref[sl, :].astype(BF16)
            return carry
        lax.fori_loop(0, w_ref.shape[0] // cast_rows, cast, 0)

    def project(rows_ref, res_ref, h_out_ref, x_out_ref):
        hn = res_ref[...] + jnp.dot(rows_ref[...].astype(BF16), wb_ref[...],
                                    preferred_element_type=F32)
        h_out_ref[...] = hn
        x_out_ref[...] = _rmsnorm_f32(hn, g_ref[...]).astype(x_out_ref.dtype)

    project(y_ref, h_ref, ho_ref, xo_ref)

    @pl.when(first_row_tile)
    def _():
        project(ys_ref, hs_ref, hso_ref, xso_ref)


def out_proj(y, ys, w, layer, h, hs, gain_next, *, tm):
    m, k = y.shape
    ms = ys.shape[0]
    n = w.shape[2]
    row = lambda cols: pl.BlockSpec((tm, cols), lambda i: (i, 0))
    sample = lambda cols: pl.BlockSpec((ms, cols), lambda i: (0, 0))
    return pl.pallas_call(
        functools.partial(_out_proj_body, cast_rows=256),
        grid=(m // tm,),
        in_specs=[row(k), sample(k),
                  pl.BlockSpec((None, k, n), lambda i: (layer, 0, 0), pipeline_mode=pl.Buffered(1)),
                  row(n), sample(n),
                  pl.BlockSpec((1, n), lambda i: (0, 0))],
        out_specs=[row(n), row(n), sample(n), sample(n)],
        out_shape=[jax.ShapeDtypeStruct((m, n), F32), jax.ShapeDtypeStruct((m, n), BF16),
                   jax.ShapeDtypeStruct((ms, n), F32), jax.ShapeDtypeStruct((ms, n), BF16)],
        scratch_shapes=[pltpu.VMEM((k, n), BF16)],
        compiler_params=_cparams(1),
        name="out_proj",
    )(y, ys, w, h, hs, gain_next.reshape(1, n))


def _ffn_body(x_ref, h_ref, xs_ref, hs_ref, wg_ref, wu_ref, wd_ref, g_ref, *out_refs, n_t, with_norm):
    if with_norm:
        ho_ref, hso_ref, xo_ref, xso_ref = out_refs
    else:
        ho_ref, hso_ref = out_refs
    i = pl.program_id(0)
    t = pl.program_id(1)

    @pl.when(t == 0)
    def _():
        ho_ref[...] = jnp.zeros(ho_ref.shape, F32)

    slice_rows = h_ref.shape[0]
    for p in range(FFN_RESIDUAL_SLICES):
        @pl.when(t == p)
        def _(p=p):
            ho_ref[p * slice_rows:(p + 1) * slice_rows, :] += h_ref[...]

    def swiglu(x):
        gate = jnp.dot(x, wg_ref[...].astype(BF16), preferred_element_type=F32)
        up = jnp.dot(x, wu_ref[...].astype(BF16), preferred_element_type=F32)
        act = (gate * _sigmoid(gate) * up).astype(BF16)
        return jnp.dot(act, wd_ref[...].astype(BF16), preferred_element_type=F32)

    ho_ref[...] += swiglu(x_ref[...])

    @pl.when((i == 0) & (t == 0))
    def _():
        hso_ref[...] = hs_ref[...]

    @pl.when(i == 0)
    def _():
        hso_ref[...] += swiglu(xs_ref[...])

    if with_norm:
        @pl.when(t == n_t - 1)
        def _():
            xo_ref[...] = _rmsnorm_f32(ho_ref[...], g_ref[...]).astype(BF16)

        @pl.when((i == 0) & (t == n_t - 1))
        def _():
            xso_ref[...] = _rmsnorm_f32(hso_ref[...], g_ref[...]).astype(BF16)


def ffn(x, h, xs, hs, w_gu, w_down, layer, gain_next, *, tm, th):
    m, d = x.shape
    ms = xs.shape[0]
    hidden = w_down.shape[1]
    n_t = hidden // th
    with_norm = gain_next is not None
    gain = gain_next if with_norm else jnp.ones((d,), F32)
    row_out = pl.BlockSpec((tm, d), lambda i, t: (i, 0))
    sample_block = pl.BlockSpec((ms, d), lambda i, t: (0, 0))
    out_specs = [row_out, sample_block]
    out_shape = [jax.ShapeDtypeStruct((m, d), F32), jax.ShapeDtypeStruct((ms, d), F32)]
    if with_norm:
        out_specs += [row_out, sample_block]
        out_shape += [jax.ShapeDtypeStruct((m, d), BF16), jax.ShapeDtypeStruct((ms, d), BF16)]
    assert n_t >= FFN_RESIDUAL_SLICES and tm % (FFN_RESIDUAL_SLICES * SUBLANES) == 0
    row_block = pl.BlockSpec((tm, d), lambda i, t: (i, 0))
    last_slice = FFN_RESIDUAL_SLICES - 1
    residual_slice = pl.BlockSpec(
        (tm // FFN_RESIDUAL_SLICES, d),
        lambda i, t: (i * FFN_RESIDUAL_SLICES + jnp.minimum(t, last_slice), 0))
    res = pl.pallas_call(
        functools.partial(_ffn_body, n_t=n_t, with_norm=with_norm),
        grid=(m // tm, n_t),
        in_specs=[row_block, residual_slice, sample_block, sample_block,
                  pl.BlockSpec((None, d, th), lambda i, t: (layer, 0, t)),
                  pl.BlockSpec((None, d, th), lambda i, t: (layer, 0, n_t + t)),
                  pl.BlockSpec((None, th, d), lambda i, t: (layer, t, 0)),
                  pl.BlockSpec((1, d), lambda i, t: (0, 0))],
        out_specs=out_specs,
        out_shape=out_shape,
        compiler_params=pltpu.CompilerParams(dimension_semantics=("arbitrary",) * 2,
                                             vmem_limit_bytes=FFN_VMEM_LIMIT_BYTES),
        name="ffn",
    )(x, h, xs, hs, w_gu, w_gu, w_down, gain.reshape(1, d))
    return tuple(res) if with_norm else (res[0], res[1], None, None)


def _spatial_body(u_ref, v_ref, wm_ref, bs_ref, vg_ref, p_ref, *maybe_vn_ref, chunk, groups):
    v = v_ref[...].astype(F32)
    vn = _rmsnorm_f32(v, vg_ref[...])
    if maybe_vn_ref:
        maybe_vn_ref[0][...] = vn
    vnb = vn.astype(BF16)
    rows, width = v.shape
    gw = width // groups
    for c in range(rows // chunk):
        r0 = c * chunk
        for g in range(groups):
            c0 = g * gw
            s = jnp.dot(wm_ref[g], vnb[r0:r0 + chunk, c0:c0 + gw], preferred_element_type=F32)
            s = s + bs_ref[g]
            u = u_ref[r0:r0 + chunk, c0:c0 + gw].astype(F32)
            p_ref[r0:r0 + chunk, c0:c0 + gw] = (u * s).astype(p_ref.dtype)


def spatial_mix(uv, wm, bs, v_gain, *, chunk, tm, with_vn):
    m, w2 = uv.shape
    width = w2 // 2
    groups = wm.shape[0]
    row_out = pl.BlockSpec((tm, width), lambda i: (i, 0))
    out_specs = [row_out]
    out_shape = [jax.ShapeDtypeStruct((m, width), BF16)]
    if with_vn:
        out_specs.append(row_out)
        out_shape.append(jax.ShapeDtypeStruct((m, width), F32))
    res = pl.pallas_call(
        functools.partial(_spatial_body, chunk=chunk, groups=groups),
        grid=(m // tm,),
        in_specs=[pl.BlockSpec((tm, width), lambda i: (i, 0)),
                  pl.BlockSpec((tm, width), lambda i: (i, 1)),
                  pl.BlockSpec(wm.shape, lambda i: (0, 0, 0)),
                  pl.BlockSpec(bs.shape, lambda i: (0, 0, 0)),
                  pl.BlockSpec((1, width), lambda i: (0, 0))],
        out_specs=out_specs,
        out_shape=out_shape,
        compiler_params=_cparams(1),
        name="spatial_mix",
    )(uv, uv, wm, bs, v_gain.reshape(1, width))
    return (res[0], res[1]) if with_vn else (res[0], None)


def _diff_attn_body(qt_tab, kt_tab, q_ref, k_ref, v_ref, sc_ref, sub_ref, o_ref,
                    qs_ref, m_ref, l_ref, acc_ref, *, tq, hb, out_scale, fixed_shift):
    t = pl.program_id(2)
    qi = qt_tab[t]
    ki = kt_tab[t]
    hd = 2 * B_HEAD_DIM

    @pl.when(ki == 0)
    def _():
        for h in range(hb):
            q = q_ref[:, h * hd:(h + 1) * hd].astype(F32) * (B_HEAD_DIM ** -0.5)
            lane = lax.broadcasted_iota(jnp.int32, q.shape, 1)
            qs_ref[h, 0:tq, :] = jnp.where(lane < B_HEAD_DIM, q, 0.0)
            qs_ref[h, tq:2 * tq, :] = jnp.where(lane >= B_HEAD_DIM, q, 0.0)
        if not fixed_shift:
            m_ref[...] = jnp.full(m_ref.shape, -jnp.inf, F32)
        l_ref[...] = jnp.zeros(l_ref.shape, F32)
        acc_ref[...] = jnp.zeros(acc_ref.shape, F32)

    def strip(h, r0, masked):
        rs = pl.ds(r0, tq)
        keys = k_ref[:, h * hd:(h + 1) * hd]
        vals = v_ref[:, h * B_V_DIM:(h + 1) * B_V_DIM]
        s = lax.dot_general(qs_ref[h, rs, :], keys, (((1,), (1,)), ((), ())),
                            preferred_element_type=F32)
        if masked:
            row = lax.broadcasted_iota(jnp.int32, s.shape, 0)
            col = lax.broadcasted_iota(jnp.int32, s.shape, 1)
            s = jnp.where(col <= row, s, -jnp.inf)
        if fixed_shift:
            p = jnp.exp(s - sc_ref[1])
            part = p[:, 0:LANES]
            for c0 in range(LANES, p.shape[1], LANES):
                part = part + p[:, c0:c0 + LANES]
            l_ref[h, rs, :] += part
            acc_ref[h, rs, :] += jnp.dot(p, vals, preferred_element_type=F32)
        else:
            m_prev = m_ref[h, rs, :]
            m_new = jnp.maximum(m_prev, jnp.max(s, axis=1, keepdims=True))
            alpha = jnp.exp(m_prev - m_new)
            p = jnp.exp(s - m_new[:, 0:1])
            l_ref[h, rs, :] = alpha * l_ref[h, rs, :] + jnp.sum(p, axis=1, keepdims=True)
            acc_ref[h, rs, :] = alpha * acc_ref[h, rs, :] + jnp.dot(p, vals, preferred_element_type=F32)
            m_ref[h, rs, :] = m_new

    @pl.when(ki < qi)
    def _():
        for h in range(hb):
            strip(h, 0, False)
            strip(h, tq, False)

    @pl.when(ki == qi)
    def _():
        for h in range(hb):
            strip(h, 0, True)
            strip(h, tq, True)
            l = l_ref[h]
            if fixed_shift:
                l = jnp.sum(l, axis=1, keepdims=True)
            o = acc_ref[h] / l
            d = o[0:tq] - sc_ref[0] * o[tq:2 * tq]
            o_ref[:, h * B_V_DIM:(h + 1) * B_V_DIM] = (
                _rmsnorm_f32(d, sub_ref[...]) * out_scale).astype(o_ref.dtype)


def diff_attn_prompt(q, k, v, lam, score_bound, subln, lam_init, *, batch, seq, heads, tq, hb,
                     fixed_shift):
    nq = seq // tq
    tri = [(qi, ki) for qi in range(nq) for ki in range(qi + 1)]
    qt_tab = jnp.asarray([a for a, _ in tri], jnp.int32)
    kt_tab = jnp.asarray([b for _, b in tri], jnp.int32)
    hd = 2 * B_HEAD_DIM
    grid_spec = pltpu.PrefetchScalarGridSpec(
        num_scalar_prefetch=2,
        grid=(batch, heads // hb, len(tri)),
        in_specs=[pl.BlockSpec((tq, hb * hd), lambda b, h, t, qt, kt: (b * nq + qt[t], h)),
                  pl.BlockSpec((tq, hb * hd), lambda b, h, t, qt, kt: (b * nq + kt[t], h)),
                  pl.BlockSpec((tq, hb * B_V_DIM), lambda b, h, t, qt, kt: (b * nq + kt[t], h)),
                  pl.BlockSpec(memory_space=pltpu.SMEM),
                  pl.BlockSpec((1, B_V_DIM), lambda b, h, t, qt, kt: (0, 0))],
        out_specs=pl.BlockSpec((tq, hb * B_V_DIM), lambda b, h, t, qt, kt: (b * nq + qt[t], h)),
        scratch_shapes=[pltpu.VMEM((hb, 2 * tq, hd), F32),
                        pltpu.VMEM((hb, 2 * tq, LANES), F32),
                        pltpu.VMEM((hb, 2 * tq, LANES), F32),
                        pltpu.VMEM((hb, 2 * tq, B_V_DIM), F32)],
    )
    return pl.pallas_call(
        functools.partial(_diff_attn_body, tq=tq, hb=hb, out_scale=1.0 - lam_init,
                          fixed_shift=fixed_shift),
        grid_spec=grid_spec,
        out_shape=jax.ShapeDtypeStruct((batch * seq, heads * B_V_DIM), BF16),
        compiler_params=_cparams(3),
        name="diff_attn_prompt_fixed_shift" if fixed_shift else "diff_attn_prompt",
    )(qt_tab, kt_tab, q, k, v, jnp.stack([lam, score_bound]).astype(F32),
      subln.reshape(1, B_V_DIM))


def _decode_attn_body(pt_ref, *refs, pages, heads, n_q, n_groups, out_scale):
    k_refs = refs[:pages]
    v_refs = refs[pages:2 * pages]
    kn_ref, vn_ref, qt_ref, lam_ref, sub_ref, o_ref, m_ref, l_ref, acc_ref = refs[2 * pages:]
    g = pl.program_id(1)
    hg = heads // SUBLANES
    hc = 2 * n_q
    cols = SUBLANES * hc

    @pl.when(g == 0)
    def _():
        m_ref[...] = jnp.full(m_ref.shape, -jnp.inf, F32)
        l_ref[...] = jnp.zeros(l_ref.shape, F32)
        acc_ref[...] = jnp.zeros(acc_ref.shape, F32)

    sub = lax.broadcasted_iota(jnp.int32, (SUBLANES, cols), 0)
    lane = lax.broadcasted_iota(jnp.int32, (SUBLANES, cols), 1)
    own = sub == lane // hc

    def to_column(x8):
        r = jnp.sum(jnp.where(own, x8, 0.0), axis=0, keepdims=True)
        return jnp.broadcast_to(r, (LANES, cols)).T

    def group_rows(ref, j, n_pos):
        x = ref[0:n_pos * heads, :].reshape(n_pos, hg, SUBLANES, 2 * B_HEAD_DIM)
        return x[:, j].reshape(n_pos * SUBLANES, 2 * B_HEAD_DIM)

    def process(page_k_refs, page_v_refs, n_pos, new_tokens):
        for j in range(hg):
            scores = []
            for k_ref in page_k_refs:
                s = jnp.dot(group_rows(k_ref, j, n_pos), qt_ref[j], preferred_element_type=F32)
                s = s.reshape(n_pos, SUBLANES, cols)
                valid = own[None]
                if new_tokens:
                    pos = lax.broadcasted_iota(jnp.int32, s.shape, 0)
                    qry = lax.broadcasted_iota(jnp.int32, s.shape, 2) % n_q
                    valid = valid & (pos <= qry)
                scores.append(jnp.where(valid, s, -jnp.inf))
            m_prev = m_ref[j]
            m_new = m_prev
            for s in scores:
                m_new = jnp.maximum(m_new, jnp.max(s, axis=0))
            m_safe = jnp.where(own, m_new, 0.0)
            alpha = jnp.exp(m_prev - m_safe)
            l_new = alpha * l_ref[j]
            pv = jnp.zeros((cols, B_V_DIM), F32)
            for s, v_ref in zip(scores, page_v_refs):
                p = jnp.exp(s - m_safe[None])
                l_new = l_new + jnp.sum(p, axis=0)
                pv = pv + lax.dot_general(p.reshape(n_pos * SUBLANES, cols), group_rows(v_ref, j, n_pos),
                                          (((0,), (0,)), ((), ())), preferred_element_type=F32)
            acc_ref[j] = acc_ref[j] * to_column(alpha) + pv
            l_ref[j] = l_new
            m_ref[j] = m_new

    process(k_refs, v_refs, PAGE_SIZE, False)

    @pl.when(g == n_groups - 1)
    def _():
        process([kn_ref], [vn_ref], n_q, True)
        for j in range(hg):
            o = acc_ref[j] / to_column(l_ref[j])
            o = o.reshape(SUBLANES, 2, n_q, B_V_DIM)
            d = o[:, 0] - lam_ref[0] * o[:, 1]
            d = d * lax.rsqrt(jnp.mean(d * d, axis=-1, keepdims=True) + EPS) * sub_ref[...]
            o_ref[j * SUBLANES:(j + 1) * SUBLANES] = (d * out_scale).astype(o_ref.dtype)


def diff_attn_decode(qt, cache_k, cache_v, page_table, k_new, v_new, lam, subln, lam_init,
                     *, layer, heads, n_q, pages):
    dec_b, n_pages = page_table.shape
    hd = 2 * B_HEAD_DIM
    rows = PAGE_SIZE * heads
    hg = heads // SUBLANES
    cols = SUBLANES * 2 * n_q
    n_groups = n_pages // pages
    page_spec = lambda p_i: pl.BlockSpec(
        (None, None, rows, hd),
        lambda b, g, pt, p_i=p_i: (layer, pt[b, g * pages + p_i], 0, 0))
    grid_spec = pltpu.PrefetchScalarGridSpec(
        num_scalar_prefetch=1,
        grid=(dec_b, n_groups),
        in_specs=([page_spec(p_i) for p_i in range(pages)] * 2
                  + [pl.BlockSpec((None, n_q * heads, hd), lambda b, g, pt: (b, 0, 0)),
                     pl.BlockSpec((None, n_q * heads, hd), lambda b, g, pt: (b, 0, 0)),
                     pl.BlockSpec((None, hg, hd, cols), lambda b, g, pt: (b, 0, 0, 0)),
                     pl.BlockSpec(memory_space=pltpu.SMEM),
                     pl.BlockSpec((1, B_V_DIM), lambda b, g, pt: (0, 0))]),
        out_specs=pl.BlockSpec((None, heads, n_q, B_V_DIM), lambda b, g, pt: (b, 0, 0, 0)),
        scratch_shapes=[pltpu.VMEM((hg, SUBLANES, cols), F32),
                        pltpu.VMEM((hg, SUBLANES, cols), F32),
                        pltpu.VMEM((hg, cols, B_V_DIM), F32)],
    )
    return pl.pallas_call(
        functools.partial(_decode_attn_body, pages=pages, heads=heads, n_q=n_q,
                          n_groups=n_groups, out_scale=1.0 - lam_init),
        grid_spec=grid_spec,
        out_shape=jax.ShapeDtypeStruct((dec_b, heads, n_q, B_V_DIM), F32),
        compiler_params=_cparams(2),
        name="diff_attn_decode",
    )(page_table, *([cache_k] * pages), *([cache_v] * pages), k_new, v_new, qt,
      lam.reshape(1), subln.reshape(1, B_V_DIM))


def _cumsum_rows(x):
    c = x.shape[0]
    sub = lax.broadcasted_iota(jnp.int32, x.shape, 0) % HGRN_SUB
    d = 1
    while d < HGRN_SUB:
        x = x + jnp.where(sub >= d, pltpu.roll(x, d, axis=0), 0.0)
        d *= 2
    blocks = []
    carry = None
    for j in range(c // HGRN_SUB):
        blk = x[j * HGRN_SUB:(j + 1) * HGRN_SUB]
        if carry is not None:
            blk = blk + carry
        blocks.append(blk)
        carry = blk[HGRN_SUB - 1:HGRN_SUB]
    return jnp.concatenate(blocks, axis=0) if len(blocks) > 1 else blocks[0]


def _hgrn_chunk(q, k, lf, v, st, row_scr):
    c = q.shape[0]
    nb = c // HGRN_SUB
    gcum = _cumsum_rows(lf)
    row_scr[0] = gcum
    row_scr[1] = k
    row_scr[2] = v
    o = lax.dot_general(q * jnp.exp(gcum), st, (((1,), (1,)), ((), ())), preferred_element_type=F32)

    if nb > 1:
        row = lax.broadcasted_iota(jnp.int32, (c, C_HEAD_DIM), 0)
        t_idx = lax.broadcasted_iota(jnp.int32, (c, c), 0)
        s_idx = lax.broadcasted_iota(jnp.int32, (c, c), 1)
        a_off = None
        size = 2 * HGRN_SUB
        while size <= c:
            half = size // 2
            if size < c:
                g_mid = jnp.concatenate(
                    [jnp.broadcast_to(gcum[b0 + half - 1:b0 + half], (size, C_HEAD_DIM))
                     for b0 in range(0, c, size)], axis=0)
            else:
                g_mid = gcum[half - 1:half]
            upper = (row % size) >= half
            qd = q * jnp.exp(jnp.where(upper, gcum - g_mid, -jnp.inf))
            kd = k * jnp.exp(jnp.where(upper, -jnp.inf, g_mid - gcum))
            a = lax.dot_general(qd, kd, (((1,), (1,)), ((), ())), preferred_element_type=F32)
            if size < c:
                a = jnp.where((t_idx // size) == (s_idx // size), a, 0.0)
            a_off = a if a_off is None else a_off + a
            size *= 2
        o = o + jnp.dot(a_off, v, preferred_element_type=F32)

    sub_row = lax.broadcasted_iota(jnp.int32, (HGRN_SUB, C_HEAD_DIM), 0)
    o_blocks = []
    for i in range(nb):
        r0 = i * HGRN_SUB
        gi = gcum[r0:r0 + HGRN_SUB]
        qi = q[r0:r0 + HGRN_SUB]
        oi = o[r0:r0 + HGRN_SUB]
        for s in range(HGRN_SUB):
            r = r0 + s
            dec = jnp.exp(jnp.where(sub_row >= s, gi - row_scr[0, r:r + 1, :], -jnp.inf))
            a_col = jnp.sum(qi * row_scr[1, r:r + 1, :] * dec, axis=-1, keepdims=True)
            oi = oi + a_col * row_scr[2, r:r + 1, :]
        o_blocks.append(oi)
    o = jnp.concatenate(o_blocks, axis=0) if nb > 1 else o_blocks[0]
    g_last = gcum[c - 1:c]
    kd = k * jnp.exp(g_last - gcum)
    if c < C_HEAD_DIM:
        pad = jnp.zeros((C_HEAD_DIM - c, C_HEAD_DIM), F32)
        kd = jnp.concatenate([kd, pad], axis=0)
        v = jnp.concatenate([v, pad], axis=0)
    st_new = st * jnp.exp(g_last) + jnp.dot(v.T, kd, preferred_element_type=F32)
    return o, st_new


def _hgrn_body(*refs, chunk, n_chunks, n_r, hb, with_state):
    if with_state:
        q_ref, k_ref, lf_ref, v_ref, g_ref, gg_ref, s0_ref, o_ref, so_ref, st_ref, gs_ref = refs
    else:
        q_ref, k_ref, lf_ref, v_ref, g_ref, gg_ref, o_ref, so_ref, st_ref, gs_ref = refs
    r = pl.program_id(2)
    hd = C_HEAD_DIM

    @pl.when(r == 0)
    def _():
        for h in range(hb):
            if with_state:
                st_ref[h] = s0_ref[h].astype(F32).T
            else:
                st_ref[h] = jnp.zeros((hd, hd), F32)

    def step(ci, carry):
        base = pl.multiple_of(ci * chunk, chunk)
        sl = pl.ds(base, chunk)
        for h in range(hb):
            cs = slice(h * hd, (h + 1) * hd)
            o, st_new = _hgrn_chunk(q_ref[sl, cs], k_ref[sl, cs], lf_ref[sl, cs], v_ref[sl, cs],
                                    st_ref[h], gs_ref.at[h])
            st_ref[h] = st_new
            gate = g_ref[sl, cs]
            o = _rmsnorm_f32(o, gg_ref[...]) * (gate * _sigmoid(gate))
            o_ref[sl, cs] = o.astype(o_ref.dtype)
        return carry

    lax.fori_loop(0, n_chunks, step, 0)

    @pl.when(r == n_r - 1)
    def _():
        for h in range(hb):
            so_ref[h] = st_ref[h].T.astype(so_ref.dtype)


def hgrn_recurrence(q, k, lf, v, g, g_gain, state0, *, batch, seq, heads, rows, chunk, hb, out_dtype):
    n_r = seq // rows
    hd = C_HEAD_DIM
    with_state = state0 is not None
    row_spec = pl.BlockSpec((rows, hb * hd), lambda b, h, r: (b * n_r + r, h))
    state_spec = pl.BlockSpec((None, hb, hd, hd), lambda b, h, r: (b, h, 0, 0))
    in_specs = [row_spec] * 5 + [pl.BlockSpec((1, hd), lambda b, h, r: (0, 0))]
    args = [q, k, lf, v, g, g_gain.reshape(1, hd)]
    if with_state:
        in_specs.append(state_spec)
        args.append(state0)
    return pl.pallas_call(
        functools.partial(_hgrn_body, chunk=chunk, n_chunks=rows // chunk, n_r=n_r, hb=hb,
                          with_state=with_state),
        grid=(batch, heads // hb, n_r),
        in_specs=in_specs,
        out_specs=[row_spec, state_spec],
        out_shape=[jax.ShapeDtypeStruct((batch * seq, heads * hd), out_dtype),
                   jax.ShapeDtypeStruct((batch, heads, hd, hd), F32)],
        scratch_shapes=[pltpu.VMEM((hb, hd, hd), F32), pltpu.VMEM((hb, 3, chunk, hd), F32)],
        compiler_params=_cparams(3),
        name="hgrn_recurrence",
    )(*args)


def _tiles(m):
    big = m >= 1024
    return dict(
        norm_tm=512 if big else m,
        proj_tm=1024 if big else m,
        proj_tn=256,
        a_proj_tn=1024,
        out_tm=512 if big else m,
        ffn_tm=1024 if big else m,
        ffn_th=256,
        spatial_tm=256 if big else m,
        attn_tq=512,
        hgrn_rows=512,
        attn_heads=8,
        hgrn_heads=2 if big else 8,
    )


def _mixer_a_core(uv, v_gain, w_s, b_s, *, chunk_len, n_seq, tiles, with_vn):
    causal = jnp.tril(jnp.ones((A_CHUNK, A_CHUNK), bool))
    w_masked = jnp.where(causal[None], w_s, 0.0)
    if chunk_len == A_CHUNK:
        wm, bs, chunk = w_masked, b_s, A_CHUNK
    else:
        eye = jnp.eye(n_seq, dtype=w_s.dtype)
        small = w_masked[:, :chunk_len, :chunk_len]
        wm = jnp.einsum("ab,gts->gatbs", eye, small).reshape(
            A_GROUPS, n_seq * chunk_len, n_seq * chunk_len)
        bs = jnp.tile(b_s[:, :chunk_len], (1, n_seq))
        chunk = n_seq * chunk_len
    p, vn = spatial_mix(uv, wm.astype(BF16), bs[:, :, None], v_gain, chunk=chunk,
                        tm=max(tiles["spatial_tm"], chunk) if chunk_len == A_CHUNK else chunk,
                        with_vn=with_vn)
    return p, vn


def _mixer_a_proj(xn, xns, w_in, layer, *, tiles):
    width2 = w_in.shape[2]
    (uv,), (uv_s,) = seg_matmul(xn, xns, w_in, layer, (0,), width2, _gelu_epilogue, (BF16,),
                                tm=tiles["proj_tm"], tn=tiles["a_proj_tn"], name="a_in_proj")
    return uv, uv_s


def _mixer_b_proj(xn, xns, w_in, layer, q_gain, k_gain, *, heads, tiles):
    width = heads * 2 * B_HEAD_DIM
    tn = tiles["proj_tn"]
    lane_group = np.arange(tn) // B_HEAD_DIM
    group_ones = jnp.asarray(lane_group[:, None] == lane_group[None, :], BF16)
    reps = width // B_HEAD_DIM
    gq = jnp.tile(q_gain.astype(F32), reps).reshape(1, width)
    gk = jnp.tile(k_gain.astype(F32), reps).reshape(1, width)
    return seg_matmul(xn, xns, w_in, layer, (0, width, 2 * width), width, _headnorm_epilogue,
                      (F32, F32, F32), vecs=(gq, gk), consts=(group_ones,),
                      tm=tiles["proj_tm"], tn=tn, name="b_in_proj")


def _mixer_c_proj(xn, xns, w_in, layer, lower_bound, *, tiles):
    width = w_in.shape[2] // 4
    return seg_matmul(xn, xns, w_in, layer, (0, width, 2 * width, 3 * width), width, _hgrn_gate_epilogue,
                      (F32,) * 5, vecs=(lower_bound.reshape(1, width),),
                      tm=tiles["proj_tm"], tn=tiles["proj_tn"], name="c_in_proj")


def kernel(x_prompt, x_sample, cache_k, cache_v, page_table, state_hgrn, norm_mix, norm_ffn, ffn_w_gu, ffn_w_down, a_w_in, a_v_norm, a_w_s, a_b_s, a_w_out, b_w_in, b_q_norm, b_k_norm, b_lambda_q1, b_lambda_k1, b_lambda_q2, b_lambda_k2, b_subln, b_w_out, c_w_in, c_g_norm, c_lower_bounds, c_w_out):
    batch, seq, d_model = x_prompt.shape
    dec_b, dec_seq, _ = x_sample.shape
    depth = norm_mix.shape[0]
    b_heads = d_model // (2 * B_HEAD_DIM)
    c_heads = d_model // C_HEAD_DIM
    mp, ms = batch * seq, dec_b * dec_seq
    tp, ts = _tiles(mp), _tiles(ms)

    probs = jax.nn.softmax(c_lower_bounds.astype(F32), axis=0)
    lower_bound = jnp.cumsum(probs, axis=0) - probs[0]

    h_p = x_prompt.reshape(mp, d_model)
    h_s = x_sample.reshape(ms, d_model)
    xn_p = norm_rows(h_p, norm_mix[0], tm=tp["norm_tm"])
    xn_s = norm_rows(h_s, norm_mix[0], tm=ts["norm_tm"])

    n_phys = cache_k.shape[1]
    cache_k2 = cache_k.reshape(cache_k.shape[0], n_phys, PAGE_SIZE * b_heads, 2 * B_HEAD_DIM)
    cache_v2 = cache_v.reshape(cache_v.shape[0], n_phys, PAGE_SIZE * b_heads, B_V_DIM)

    k_p_rows, v_p_rows, k_s_rows, v_s_rows = [], [], [], []
    hgrn_p, hgrn_s, chunk_v_s = [], [], []
    for i in range(depth):
        kind, j = i % 3, i // 3
        if kind == 0:
            uv_p, uv_s = _mixer_a_proj(xn_p, xn_s, a_w_in, j, tiles=tp)
            y_p, _ = _mixer_a_core(uv_p, a_v_norm[j], a_w_s[j], a_b_s[j],
                                   chunk_len=A_CHUNK, n_seq=batch, tiles=tp, with_vn=False)
            y_s, vn_s = _mixer_a_core(uv_s, a_v_norm[j], a_w_s[j], a_b_s[j],
                                      chunk_len=dec_seq, n_seq=dec_b, tiles=ts, with_vn=True)
            chunk_v_s.append(vn_s.reshape(dec_b, dec_seq, -1))
            w_out = a_w_out
        elif kind == 1:
            lam_init = 0.8 - 0.6 * math.exp(-0.3 * i)
            lam = (jnp.exp(jnp.sum(b_lambda_q1[j].astype(F32) * b_lambda_k1[j].astype(F32)))
                   - jnp.exp(jnp.sum(b_lambda_q2[j].astype(F32) * b_lambda_k2[j].astype(F32)))
                   + lam_init)
            (q_p, k_p, v_p), (q_s, k_s, v_s) = _mixer_b_proj(
                xn_p, xn_s, b_w_in, j, b_q_norm[j], b_k_norm[j], heads=b_heads, tiles=tp)
            score_bound = (B_HEAD_DIM ** 0.5) * jnp.max(jnp.abs(b_q_norm[j].astype(F32))) * jnp.max(
                jnp.abs(b_k_norm[j].astype(F32)))
            attn = functools.partial(diff_attn_prompt, batch=batch, seq=seq, heads=b_heads,
                                     tq=tp["attn_tq"], hb=tp["attn_heads"])
            y_p = lax.cond(
                score_bound <= ATTN_FIXED_SHIFT_MAX_BOUND,
                lambda *a: attn(*a, lam_init, fixed_shift=True),
                lambda *a: attn(*a, lam_init, fixed_shift=False),
                q_p, k_p, v_p, lam, score_bound, b_subln[j])
            k_p_rows.append(k_p.reshape(batch, seq, b_heads, 2 * B_HEAD_DIM))
            v_p_rows.append(v_p.reshape(batch, seq, b_heads, B_V_DIM))

            q5 = (q_s *(B_HEAD_DIM ** -0.5)).reshape(dec_b, dec_seq, b_heads, 2, B_HEAD_DIM)
            qt = jnp.einsum("bthcd,ce->bhcted", q5, jnp.eye(2, dtype=F32))
            qt = qt.reshape(dec_b, b_heads // SUBLANES, SUBLANES * 2 * dec_seq, 2 * B_HEAD_DIM)
            qt = qt.transpose(0, 1, 3, 2)
            k_new = k_s.reshape(dec_b, dec_seq * b_heads, 2 * B_HEAD_DIM)
            v_new = v_s.reshape(dec_b, dec_seq * b_heads, B_V_DIM)
            o_s = diff_attn_decode(qt, cache_k2, cache_v2, page_table, k_new, v_new, lam,
                                   b_subln[j], lam_init, layer=j, heads=b_heads, n_q=dec_seq,
                                   pages=DECODE_PAGES_PER_STEP)
            y_s = o_s.transpose(0, 2, 1, 3).reshape(ms, d_model)
            k_s_rows.append(k_s.reshape(dec_b, dec_seq, b_heads, 2 * B_HEAD_DIM))
            v_s_rows.append(v_s.reshape(dec_b, dec_seq, b_heads, B_V_DIM))
            w_out = b_w_out
        else:
            qkv_p, qkv_s = _mixer_c_proj(xn_p, xn_s, c_w_in, j, lower_bound[i], tiles=tp)
            y_p, st_p = hgrn_recurrence(*qkv_p, c_g_norm[j], None, batch=batch, seq=seq,
                                        heads=c_heads, rows=tp["hgrn_rows"], chunk=C_HEAD_DIM,
                                        hb=tp["hgrn_heads"], out_dtype=BF16)
            y_s, st_s = hgrn_recurrence(*qkv_s, c_g_norm[j], state_hgrn[j], batch=dec_b,
                                        seq=dec_seq, heads=c_heads, rows=dec_seq, chunk=dec_seq,
                                        hb=ts["hgrn_heads"], out_dtype=F32)
            hgrn_p.append(st_p)
            hgrn_s.append(st_s)
            w_out = c_w_out
        h_p, xf_p, h_s, xf_s = out_proj(y_p, y_s, w_out, j, h_p, h_s, norm_ffn[i], tm=tp["out_tm"])
        gain_next = norm_mix[i + 1] if i + 1 < depth else None
        h_p, h_s, xn_p, xn_s = ffn(xf_p, h_p, xf_s, h_s, ffn_w_gu, ffn_w_down, i, gain_next,
                                   tm=tp["ffn_tm"], th=tp["ffn_th"])
    return (h_p.reshape(batch, seq, d_model), h_s.reshape(dec_b, dec_seq, d_model),
            jnp.stack(k_p_rows), jnp.stack(v_p_rows), jnp.stack(k_s_rows), jnp.stack(v_s_rows),
            jnp.stack(hgrn_p), jnp.stack(hgrn_s), jnp.stack(chunk_v_s))
```

```python
import functools
import math

import jax
import jax.numpy as jnp
import numpy as np
from jax import lax
from jax.experimental import pallas as pl
from jax.experimental.pallas import tpu as pltpu

F32 = jnp.float32
BF16 = jnp.bfloat16
EPS = 1e-6

LANES = 128
SUBLANES = 8
VMEM_LIMIT_BYTES = 56 << 20
FFN_VMEM_LIMIT_BYTES = 60 << 20
FFN_RESIDUAL_SLICES = 8

A_CHUNK = 128
A_GROUPS = 8
B_HEAD_DIM = 64
B_V_DIM = 2 * B_HEAD_DIM
C_HEAD_DIM = 128
PAGE_SIZE = 128
HGRN_SUB = SUBLANES
DECODE_PAGES_PER_STEP = 8
ATTN_FIXED_SHIFT_MAX_BOUND = 30.0


def _cparams(n_axes):
    return pltpu.CompilerParams(
        dimension_semantics=("arbitrary",) * n_axes,
        vmem_limit_bytes=VMEM_LIMIT_BYTES,
    )


def _rmsnorm_f32(x, gain):
    return x * lax.rsqrt(jnp.mean(x * x, axis=-1, keepdims=True) + EPS) * gain


def _sigmoid(x):
    return 1.0 / (1.0 + jnp.exp(-x))


def _norm_rows_body(x_ref, g_ref, o_ref):
    o_ref[...] = _rmsnorm_f32(x_ref[...], g_ref[...]).astype(o_ref.dtype)


def norm_rows(x, gain, *, tm):
    m, d = x.shape
    return pl.pallas_call(
        _norm_rows_body,
        grid=(m // tm,),
        in_specs=[pl.BlockSpec((tm, d), lambda i: (i, 0)),
                  pl.BlockSpec((1, d), lambda i: (0, 0))],
        out_specs=pl.BlockSpec((tm, d), lambda i: (i, 0)),
        out_shape=jax.ShapeDtypeStruct((m, d), BF16),
        compiler_params=_cparams(1),
        name="norm_rows",
    )(x, gain.reshape(1, d))


def _seg_matmul_body(*refs, ns, nv, nc, no, epilogue):
    x_ref, xs_ref = refs[0:2]
    w_refs = refs[2:2 + ns]
    vec_refs = refs[2 + ns:2 + ns + nv]
    const_refs = refs[2 + ns + nv:2 + ns + nv + nc]
    out_refs = refs[2 + ns + nv + nc:2 + ns + nv + nc + no]
    sample_out_refs = refs[2 + ns + nv + nc + no:-1]
    wb_ref = refs[-1]
    first_row_tile = pl.program_id(1) == 0

    @pl.when(first_row_tile)
    def _():
        for s in range(ns):
            wb_ref[s] = w_refs[s][...].astype(BF16)

    def project(rows_ref, dst_refs):
        x = rows_ref[...]
        accs = [jnp.dot(x, wb_ref[s], preferred_element_type=F32) for s in range(ns)]
        outs = epilogue(accs, [r[...] for r in vec_refs], [r[...] for r in const_refs])
        for r, o in zip(dst_refs, outs):
            r[...] = o.astype(r.dtype)

    project(x_ref, out_refs)

    @pl.when(first_row_tile)
    def _():
        project(xs_ref, sample_out_refs)


def seg_matmul(x, xs, w, layer, seg_starts, seg_width, epilogue, out_dtypes, vecs=(), consts=(),
               *, tm, tn, name):
    m, k = x.shape
    ms = xs.shape[0]
    ns = len(seg_starts)
    no = len(out_dtypes)
    in_specs = [pl.BlockSpec((tm, k), lambda j, i: (i, 0)),
                pl.BlockSpec((ms, k), lambda j, i: (0, 0))]
    for st in seg_starts:
        in_specs.append(pl.BlockSpec((None, k, tn), lambda j, i, off=st // tn: (layer, 0, off + j)))
    for _ in vecs:
        in_specs.append(pl.BlockSpec((1, tn), lambda j, i: (0, j)))
    for c in consts:
        in_specs.append(pl.BlockSpec(c.shape, lambda j, i, nd=c.ndim: (0,) * nd))
    res = pl.pallas_call(
        functools.partial(_seg_matmul_body, ns=ns, nv=len(vecs), nc=len(consts), no=no,
                          epilogue=epilogue),
        grid=(seg_width // tn, m // tm),
        in_specs=in_specs,
        out_specs=([pl.BlockSpec((tm, tn), lambda j, i: (i, j)) for _ in out_dtypes]
                   + [pl.BlockSpec((ms, tn), lambda j, i: (0, j)) for _ in out_dtypes]),
        out_shape=([jax.ShapeDtypeStruct((m, seg_width), dt) for dt in out_dtypes]
                   + [jax.ShapeDtypeStruct((ms, seg_width), dt) for dt in out_dtypes]),
        scratch_shapes=[pltpu.VMEM((ns, k, tn), BF16)],
        compiler_params=_cparams(2),
        name=name,
    )(x, xs, *([w] * ns), *vecs, *consts)
    return res[:no], res[no:]


def _gelu_exact_f32(a):
    z = a * (2.0 ** -0.5)
    az = jnp.abs(z)
    t = 1.0 / (1.0 + 0.3275911 * az)
    poly = t * (0.254829592 + t * (-0.284496736 + t * (1.421413741
                                                       + t * (-1.453152027 + t * 1.061405429))))
    erfc_abs = poly * jnp.exp(-az * az)
    return 0.5 * a * jnp.where(z >= 0, 2.0 - erfc_abs, erfc_abs)


def _gelu_epilogue(accs, vecs, consts):
    (a,) = accs
    return [_gelu_exact_f32(a)]


def _headnorm_epilogue(accs, vecs, consts):
    aq, ak, av = accs
    gq, gk = vecs
    (group_ones,) = consts

    def head_norm(a, g):
        ms = jnp.dot((a * a).astype(BF16), group_ones, preferred_element_type=F32) * (1.0 / B_HEAD_DIM)
        return a * lax.rsqrt(ms + EPS) * g

    return [head_norm(aq, gq), head_norm(ak, gk), av]


def _hgrn_gate_epilogue(accs, vecs, consts):
    aq, af, av, ag = accs
    (lb,) = vecs
    q = aq * _sigmoid(aq)
    sig = _sigmoid(af)
    f = lb + (1.0 - lb) * sig
    k = (1.0 - lb) * (1.0 - sig)
    return [q, k, jnp.log(f), av, ag]


def _out_proj_body(y_ref, ys_ref, w_ref, h_ref, hs_ref, g_ref, ho_ref, xo_ref, hso_ref, xso_ref,
                   wb_ref, *, cast_rows):
    first_row_tile = pl.program_id(0) == 0

    @pl.when(first_row_tile)
    def _():
        def cast(r, carry):
            sl = pl.ds(pl.multiple_of(r * cast_rows, cast_rows), cast_rows)
            wb_ref[sl, :] = w_ref[sl, :].astype(BF16)
            return carry
        lax.fori_loop(0, w_ref.shape[0] // cast_rows, cast, 0)

    def project(rows_ref, res_ref, h_out_ref, x_out_ref):
        hn = res_ref[...] + jnp.dot(rows_ref[...].astype(BF16), wb_ref[...],
                                    preferred_element_type=F32)
        h_out_ref[...] = hn
        x_out_ref[...] = _rmsnorm_f32(hn, g_ref[...]).astype(x_out_ref.dtype)

    project(y_ref, h_ref, ho_ref, xo_ref)

    @pl.when(first_row_tile)
    def _():
        project(ys_ref, hs_ref, hso_ref, xso_ref)


def out_proj(y, ys, w, layer, h, hs, gain_next, *, tm):
    m, k = y.shape
    ms = ys.shape[0]
    n = w.shape[2]
    row = lambda cols: pl.BlockSpec((tm, cols), lambda i: (i, 0))
    sample = lambda cols: pl.BlockSpec((ms, cols), lambda i: (0, 0))
    return pl.pallas_call(
        functools.partial(_out_proj_body, cast_rows=256),
        grid=(m // tm,),
        in_specs=[row(k), sample(k),
                  pl.BlockSpec((None, k, n), lambda i: (layer, 0, 0), pipeline_mode=pl.Buffered(1)),
                  row(n), sample(n),
                  pl.BlockSpec((1, n), lambda i: (0, 0))],
        out_specs=[row(n), row(n), sample(n), sample(n)],
        out_shape=[jax.ShapeDtypeStruct((m, n), F32), jax.ShapeDtypeStruct((m, n), BF16),
                   jax.ShapeDtypeStruct((ms, n), F32), jax.ShapeDtypeStruct((ms, n), BF16)],
        scratch_shapes=[pltpu.VMEM((k, n), BF16)],
        compiler_params=_cparams(1),
        name="out_proj",
    )(y, ys, w, h, hs, gain_next.reshape(1, n))


def _ffn_body(x_ref, h_ref, xs_ref, hs_ref, wg_ref, wu_ref, wd_ref, g_ref, *out_refs, n_t, with_norm):
    if with_norm:
        ho_ref, hso_ref, xo_ref, xso_ref = out_refs
    else:
        ho_ref, hso_ref = out_refs
    i = pl.program_id(0)
    t = pl.program_id(1)

    @pl.when(t == 0)
    def _():
        ho_ref[...] = jnp.zeros(ho_ref.shape, F32)

    slice_rows = h_ref.shape[0]
    for p in range(FFN_RESIDUAL_SLICES):
        @pl.when(t == p)
        def _(p=p):
            ho_ref[p * slice_rows:(p + 1) * slice_rows, :] += h_ref[...]

    def swiglu(x):
        gate = jnp.dot(x, wg_ref[...].astype(BF16), preferred_element_type=F32)
        up = jnp.dot(x, wu_ref[...].astype(BF16), preferred_element_type=F32)
        act = (gate * _sigmoid(gate) * up).astype(BF16)
        return jnp.dot(act, wd_ref[...].astype(BF16), preferred_element_type=F32)

    ho_ref[...] += swiglu(x_ref[...])

    @pl.when((i == 0) & (t == 0))
    def _():
        hso_ref[...] = hs_ref[...]

    @pl.when(i == 0)
    def _():
        hso_ref[...] += swiglu(xs_ref[...])

    if with_norm:
        @pl.when(t == n_t - 1)
        def _():
            xo_ref[...] = _rmsnorm_f32(ho_ref[...], g_ref[...]).astype(BF16)

        @pl.when((i == 0) & (t == n_t - 1))
        def _():
            xso_ref[...] = _rmsnorm_f32(hso_ref[...], g_ref[...]).astype(BF16)


def ffn(x, h, xs, hs, w_gu, w_down, layer, gain_next, *, tm, th):
    m, d = x.shape
    ms = xs.shape[0]
    hidden = w_down.shape[1]
    n_t = hidden // th
    with_norm = gain_next is not None
    gain = gain_next if with_norm else jnp.ones((d,), F32)
    row_out = pl.BlockSpec((tm, d), lambda i, t: (i, 0))
    sample_block = pl.BlockSpec((ms, d), lambda i, t: (0, 0))
    out_specs = [row_out, sample_block]
    out_shape = [jax.ShapeDtypeStruct((m, d), F32), jax.ShapeDtypeStruct((ms, d), F32)]
    if with_norm:
        out_specs += [row_out, sample_block]
        out_shape += [jax.ShapeDtypeStruct((m, d), BF16), jax.ShapeDtypeStruct((ms, d), BF16)]
    assert n_t >= FFN_RESIDUAL_SLICES and tm % (FFN_RESIDUAL_SLICES * SUBLANES) == 0
    row_block = pl.BlockSpec((tm, d), lambda i, t: (i, 0))
    last_slice = FFN_RESIDUAL_SLICES - 1
    residual_slice = pl.BlockSpec(
        (tm // FFN_RESIDUAL_SLICES, d),
        lambda i, t: (i * FFN_RESIDUAL_SLICES + jnp.minimum(t, last_slice), 0))
    res = pl.pallas_call(
        functools.partial(_ffn_body, n_t=n_t, with_norm=with_norm),
        grid=(m // tm, n_t),
        in_specs=[row_block, residual_slice, sample_block, sample_block,
                  pl.BlockSpec((None, d, th), lambda i, t: (layer, 0, t)),
                  pl.BlockSpec((None, d, th), lambda i, t: (layer, 0, n_t + t)),
                  pl.BlockSpec((None, th, d), lambda i, t: (layer, t, 0)),
                  pl.BlockSpec((1, d), lambda i, t: (0, 0))],
        out_specs=out_specs,
        out_shape=out_shape,
        compiler_params=pltpu.CompilerParams(dimension_semantics=("arbitrary",) * 2,
                                             vmem_limit_bytes=FFN_VMEM_LIMIT_BYTES),
        name="ffn",
    )(x, h, xs, hs, w_gu, w_gu, w_down, gain.reshape(1, d))
    return tuple(res) if with_norm else (res[0], res[1], None, None)


def _spatial_body(u_ref, v_ref, wm_ref, bs_ref, vg_ref, p_ref, *maybe_vn_ref, chunk, groups):
    v = v_ref[...].astype(F32)
    vn = _rmsnorm_f32(v, vg_ref[...])
    if maybe_vn_ref:
        maybe_vn_ref[0][...] = vn
    vnb = vn.astype(BF16)
    rows, width = v.shape
    gw = width // groups
    for c in range(rows // chunk):
        r0 = c * chunk
        for g in range(groups):
            c0 = g * gw
            s = jnp.dot(wm_ref[g], vnb[r0:r0 + chunk, c0:c0 + gw], preferred_element_type=F32)
            s = s + bs_ref[g]
            u = u_ref[r0:r0 + chunk, c0:c0 + gw].astype(F32)
            p_ref[r0:r0 + chunk, c0:c0 + gw] = (u * s).astype(p_ref.dtype)


def spatial_mix(uv, wm, bs, v_gain, *, chunk, tm, with_vn):
    m, w2 = uv.shape
    width = w2 // 2
    groups = wm.shape[0]
    row_out = pl.BlockSpec((tm, width), lambda i: (i, 0))
    out_specs = [row_out]
    out_shape = [jax.ShapeDtypeStruct((m, width), BF16)]
    if with_vn:
        out_specs.append(row_out)
        out_shape.append(jax.ShapeDtypeStruct((m, width), F32))
    res = pl.pallas_call(
        functools.partial(_spatial_body, chunk=chunk, groups=groups),
        grid=(m // tm,),
        in_specs=[pl.BlockSpec((tm, width), lambda i: (i, 0)),
                  pl.BlockSpec((tm, width), lambda i: (i, 1)),
                  pl.BlockSpec(wm.shape, lambda i: (0, 0, 0)),
                  pl.BlockSpec(bs.shape, lambda i: (0, 0, 0)),
                  pl.BlockSpec((1, width), lambda i: (0, 0))],
        out_specs=out_specs,
        out_shape=out_shape,
        compiler_params=_cparams(1),
        name="spatial_mix",
    )(uv, uv, wm, bs, v_gain.reshape(1, width))
    return (res[0], res[1]) if with_vn else (res[0], None)


def _diff_attn_body(qt_tab, kt_tab, q_ref, k_ref, v_ref, sc_ref, sub_ref, o_ref,
                    qs_ref, m_ref, l_ref, acc_ref, *, tq, hb, out_scale, fixed_shift):
    t = pl.program_id(2)
    qi = qt_tab[t]
    ki = kt_tab[t]
    hd = 2 * B_HEAD_DIM

    @pl.when(ki == 0)
    def _():
        for h in range(hb):
            q = q_ref[:, h * hd:(h + 1) * hd].astype(F32) * (B_HEAD_DIM ** -0.5)
            lane = lax.broadcasted_iota(jnp.int32, q.shape, 1)
            qs_ref[h, 0:tq, :] = jnp.where(lane < B_HEAD_DIM, q, 0.0)
            qs_ref[h, tq:2 * tq, :] = jnp.where(lane >= B_HEAD_DIM, q, 0.0)
        if not fixed_shift:
            m_ref[...] = jnp.full(m_ref.shape, -jnp.inf, F32)
        l_ref[...] = jnp.zeros(l_ref.shape, F32)
        acc_ref[...] = jnp.zeros(acc_ref.shape, F32)

    def strip(h, r0, masked):
        rs = pl.ds(r0, tq)
        keys = k_ref[:, h * hd:(h + 1) * hd]
        vals = v_ref[:, h * B_V_DIM:(h + 1) * B_V_DIM]
        s = lax.dot_general(qs_ref[h, rs, :], keys, (((1,), (1,)), ((), ())),
                            preferred_element_type=F32)
        if masked:
            row = lax.broadcasted_iota(jnp.int32, s.shape, 0)
            col = lax.broadcasted_iota(jnp.int32, s.shape, 1)
            s = jnp.where(col <= row, s, -jnp.inf)
        if fixed_shift:
            p = jnp.exp(s - sc_ref[1])
            part = p[:, 0:LANES]
            for c0 in range(LANES, p.shape[1], LANES):
                part = part + p[:, c0:c0 + LANES]
            l_ref[h, rs, :] += part
            acc_ref[h, rs, :] += jnp.dot(p, vals, preferred_element_type=F32)
        else:
            m_prev = m_ref[h, rs, :]
            m_new = jnp.maximum(m_prev, jnp.max(s, axis=1, keepdims=True))
            alpha = jnp.exp(m_prev - m_new)
            p = jnp.exp(s - m_new[:, 0:1])
            l_ref[h, rs, :] = alpha * l_ref[h, rs, :] + jnp.sum(p, axis=1, keepdims=True)
            acc_ref[h, rs, :] = alpha * acc_ref[h, rs, :] + jnp.dot(p, vals, preferred_element_type=F32)
            m_ref[h, rs, :] = m_new

    @pl.when(ki < qi)
    def _():
        for h in range(hb):
            strip(h, 0, False)
            strip(h, tq, False)

    @pl.when(ki == qi)
    def _():
        for h in range(hb):
            strip(h, 0, True)
            strip(h, tq, True)
            l = l_ref[h]
            if fixed_shift:
                l = jnp.sum(l, axis=1, keepdims=True)
            o = acc_ref[h] / l
            d = o[0:tq] - sc_ref[0] * o[tq:2 * tq]
            o_ref[:, h * B_V_DIM:(h + 1) * B_V_DIM] = (
                _rmsnorm_f32(d, sub_ref[...]) * out_scale).astype(o_ref.dtype)


def diff_attn_prompt(q, k, v, lam, score_bound, subln, lam_init, *, batch, seq, heads, tq, hb,
                     fixed_shift):
    nq = seq // tq
    tri = [(qi, ki) for qi in range(nq) for ki in range(qi + 1)]
    qt_tab = jnp.asarray([a for a, _ in tri], jnp.int32)
    kt_tab = jnp.asarray([b for _, b in tri], jnp.int32)
    hd = 2 * B_HEAD_DIM
    grid_spec = pltpu.PrefetchScalarGridSpec(
        num_scalar_prefetch=2,
        grid=(batch, heads // hb, len(tri)),
        in_specs=[pl.BlockSpec((tq, hb * hd), lambda b, h, t, qt, kt: (b * nq + qt[t], h)),
                  pl.BlockSpec((tq, hb * hd), lambda b, h, t, qt, kt: (b * nq + kt[t], h)),
                  pl.BlockSpec((tq, hb * B_V_DIM), lambda b, h, t, qt, kt: (b * nq + kt[t], h)),
                  pl.BlockSpec(memory_space=pltpu.SMEM),
                  pl.BlockSpec((1, B_V_DIM), lambda b, h, t, qt, kt: (0, 0))],
        out_specs=pl.BlockSpec((tq, hb * B_V_DIM), lambda b, h, t, qt, kt: (b * nq + qt[t], h)),
        scratch_shapes=[pltpu.VMEM((hb, 2 * tq, hd), F32),
                        pltpu.VMEM((hb, 2 * tq, LANES), F32),
                        pltpu.VMEM((hb, 2 * tq, LANES), F32),
                        pltpu.VMEM((hb, 2 * tq, B_V_DIM), F32)],
    )
    return pl.pallas_call(
        functools.partial(_diff_attn_body, tq=tq, hb=hb, out_scale=1.0 - lam_init,
                          fixed_shift=fixed_shift),
        grid_spec=grid_spec,
        out_shape=jax.ShapeDtypeStruct((batch * seq, heads * B_V_DIM), BF16),
        compiler_params=_cparams(3),
        name="diff_attn_prompt_fixed_shift" if fixed_shift else "diff_attn_prompt",
    )(qt_tab, kt_tab, q, k, v, jnp.stack([lam, score_bound]).astype(F32),
      subln.reshape(1, B_V_DIM))


def _decode_attn_body(pt_ref, *refs, pages, heads, n_q, n_groups, out_scale):
    k_refs = refs[:pages]
    v_refs = refs[pages:2 * pages]
    kn_ref, vn_ref, qt_ref, lam_ref, sub_ref, o_ref, m_ref, l_ref, acc_ref = refs[2 * pages:]
    g = pl.program_id(1)
    hg = heads // SUBLANES
    hc = 2 * n_q
    cols = SUBLANES * hc

    @pl.when(g == 0)
    def _():
        m_ref[...] = jnp.full(m_ref.shape, -jnp.inf, F32)
        l_ref[...] = jnp.zeros(l_ref.shape, F32)
        acc_ref[...] = jnp.zeros(acc_ref.shape, F32)

    sub = lax.broadcasted_iota(jnp.int32, (SUBLANES, cols), 0)
    lane = lax.broadcasted_iota(jnp.int32, (SUBLANES, cols), 1)
    own = sub == lane // hc

    def to_column(x8):
        r = jnp.sum(jnp.where(own, x8, 0.0), axis=0, keepdims=True)
        return jnp.broadcast_to(r, (LANES, cols)).T

    def group_rows(ref, j, n_pos):
        x = ref[0:n_pos * heads, :].reshape(n_pos, hg, SUBLANES, 2 * B_HEAD_DIM)
        return x[:, j].reshape(n_pos * SUBLANES, 2 * B_HEAD_DIM)

    def process(page_k_refs, page_v_refs, n_pos, new_tokens):
        for j in range(hg):
            scores = []
            for k_ref in page_k_refs:
                s = jnp.dot(group_rows(k_ref, j, n_pos), qt_ref[j], preferred_element_type=F32)
                s = s.reshape(n_pos, SUBLANES, cols)
                valid = own[None]
                if new_tokens:
                    pos = lax.broadcasted_iota(jnp.int32, s.shape, 0)
                    qry = lax.broadcasted_iota(jnp.int32, s.shape, 2) % n_q
                    valid = valid & (pos <= qry)
                scores.append(jnp.where(valid, s, -jnp.inf))
            m_prev = m_ref[j]
            m_new = m_prev
            for s in scores:
                m_new = jnp.maximum(m_new, jnp.max(s, axis=0))
            m_safe = jnp.where(own, m_new, 0.0)
            alpha = jnp.exp(m_prev - m_safe)
            l_new = alpha * l_ref[j]
            pv = jnp.zeros((cols, B_V_DIM), F32)
            for s, v_ref in zip(scores, page_v_refs):
                p = jnp.exp(s - m_safe[None])
                l_new = l_new + jnp.sum(p, axis=0)
                pv = pv + lax.dot_general(p.reshape(n_pos * SUBLANES, cols), group_rows(v_ref, j, n_pos),
                                          (((0,), (0,)), ((), ())), preferred_element_type=F32)
            acc_ref[j] = acc_ref[j] * to_column(alpha) + pv
            l_ref[j] = l_new
            m_ref[j] = m_new

    process(k_refs, v_refs, PAGE_SIZE, False)

    @pl.when(g == n_groups - 1)
    def _():
        process([kn_ref], [vn_ref], n_q, True)
        for j in range(hg):
            o = acc_ref[j] / to_column(l_ref[j])
            o = o.reshape(SUBLANES, 2, n_q, B_V_DIM)
            d = o[:, 0] - lam_ref[0] * o[:, 1]
            d = d * lax.rsqrt(jnp.mean(d * d, axis=-1, keepdims=True) + EPS) * sub_ref[...]
            o_ref[j * SUBLANES:(j + 1) * SUBLANES] = (d * out_scale).astype(o_ref.dtype)


def diff_attn_decode(qt, cache_k, cache_v, page_table, k_new, v_new, lam, subln, lam_init,
                     *, layer, heads, n_q, pages):
    dec_b, n_pages = page_table.shape
    hd = 2 * B_HEAD_DIM
    rows = PAGE_SIZE * heads
    hg = heads // SUBLANES
    cols = SUBLANES * 2 * n_q
    n_groups = n_pages // pages
    page_spec = lambda p_i: pl.BlockSpec(
        (None, None, rows, hd),
        lambda b, g, pt, p_i=p_i: (layer, pt[b, g * pages + p_i], 0, 0))
    grid_spec = pltpu.PrefetchScalarGridSpec(
        num_scalar_prefetch=1,
        grid=(dec_b, n_groups),
        in_specs=([page_spec(p_i) for p_i in range(pages)] * 2
                  + [pl.BlockSpec((None, n_q * heads, hd), lambda b, g, pt: (b, 0, 0)),
                     pl.BlockSpec((None, n_q * heads, hd), lambda b, g, pt: (b, 0, 0)),
                     pl.BlockSpec((None, hg, hd, cols), lambda b, g, pt: (b, 0, 0, 0)),
                     pl.BlockSpec(memory_space=pltpu.SMEM),
                     pl.BlockSpec((1, B_V_DIM), lambda b, g, pt: (0, 0))]),
        out_specs=pl.BlockSpec((None, heads, n_q, B_V_DIM), lambda b, g, pt: (b, 0, 0, 0)),
        scratch_shapes=[pltpu.VMEM((hg, SUBLANES, cols), F32),
                        pltpu.VMEM((hg, SUBLANES, cols), F32),
                        pltpu.VMEM((hg, cols, B_V_DIM), F32)],
    )
    return pl.pallas_call(
        functools.partial(_decode_attn_body, pages=pages, heads=heads, n_q=n_q,
                          n_groups=n_groups, out_scale=1.0 - lam_init),
        grid_spec=grid_spec,
        out_shape=jax.ShapeDtypeStruct((dec_b, heads, n_q, B_V_DIM), F32),
        compiler_params=_cparams(2),
        name="diff_attn_decode",
    )(page_table, *([cache_k] * pages), *([cache_v] * pages), k_new, v_new, qt,
      lam.reshape(1), subln.reshape(1, B_V_DIM))


def _cumsum_rows(x):
    c = x.shape[0]
    sub = lax.broadcasted_iota(jnp.int32, x.shape, 0) % HGRN_SUB
    d = 1
    while d < HGRN_SUB:
        x = x + jnp.where(sub >= d, pltpu.roll(x, d, axis=0), 0.0)
        d *= 2
    blocks = []
    carry = None
    for j in range(c // HGRN_SUB):
        blk = x[j * HGRN_SUB:(j + 1) * HGRN_SUB]
        if carry is not None:
            blk = blk + carry
        blocks.append(blk)
        carry = blk[HGRN_SUB - 1:HGRN_SUB]
    return jnp.concatenate(blocks, axis=0) if len(blocks) > 1 else blocks[0]


def _hgrn_chunk(q, k, lf, v, st, row_scr):
    c = q.shape[0]
    nb = c // HGRN_SUB
    gcum = _cumsum_rows(lf)
    row_scr[0] = gcum
    row_scr[1] = k
    row_scr[2] = v
    o = lax.dot_general(q * jnp.exp(gcum), st, (((1,), (1,)), ((), ())), preferred_element_type=F32)

    if nb > 1:
        row = lax.broadcasted_iota(jnp.int32, (c, C_HEAD_DIM), 0)
        t_idx = lax.broadcasted_iota(jnp.int32, (c, c), 0)
        s_idx = lax.broadcasted_iota(jnp.int32, (c, c), 1)
        a_off = None
        size = 2 * HGRN_SUB
        while size <= c:
            half = size // 2
            if size < c:
                g_mid = jnp.concatenate(
                    [jnp.broadcast_to(gcum[b0 + half - 1:b0 + half], (size, C_HEAD_DIM))
                     for b0 in range(0, c, size)], axis=0)
            else:
                g_mid = gcum[half - 1:half]
            upper = (row % size) >= half
            qd = q * jnp.exp(jnp.where(upper, gcum - g_mid, -jnp.inf))
            kd = k * jnp.exp(jnp.where(upper, -jnp.inf, g_mid - gcum))
            a = lax.dot_general(qd, kd, (((1,), (1,)), ((), ())), preferred_element_type=F32)
            if size < c:
                a = jnp.where((t_idx // size) == (s_idx // size), a, 0.0)
            a_off = a if a_off is None else a_off + a
            size *= 2
        o = o + jnp.dot(a_off, v, preferred_element_type=F32)

    sub_row = lax.broadcasted_iota(jnp.int32, (HGRN_SUB, C_HEAD_DIM), 0)
    o_blocks = []
    for i in range(nb):
        r0 = i * HGRN_SUB
        gi = gcum[r0:r0 + HGRN_SUB]
        qi = q[r0:r0 + HGRN_SUB]
        oi = o[r0:r0 + HGRN_SUB]
        for s in range(HGRN_SUB):
            r = r0 + s
            dec = jnp.exp(jnp.where(sub_row >= s, gi - row_scr[0, r:r + 1, :], -jnp.inf))
            a_col = jnp.sum(qi * row_scr[1, r:r + 1, :] * dec, axis=-1, keepdims=True)
            oi = oi + a_col * row_scr[2, r:r + 1, :]
        o_blocks.append(oi)
    o = jnp.concatenate(o_blocks, axis=0) if nb > 1 else o_blocks[0]
    g_last = gcum[c - 1:c]
    kd = k * jnp.exp(g_last - gcum)
    if c < C_HEAD_DIM:
        pad = jnp.zeros((C_HEAD_DIM - c, C_HEAD_DIM), F32)
        kd = jnp.concatenate([kd, pad], axis=0)
        v = jnp.concatenate([v, pad], axis=0)
    st_new = st * jnp.exp(g_last) + jnp.dot(v.T, kd, preferred_element_type=F32)
    return o, st_new


def _hgrn_body(*refs, chunk, n_chunks, n_r, hb, with_state):
    if with_state:
        q_ref, k_ref, lf_ref, v_ref, g_ref, gg_ref, s0_ref, o_ref, so_ref, st_ref, gs_ref = refs
    else:
        q_ref, k_ref, lf_ref, v_ref, g_ref, gg_ref, o_ref, so_ref, st_ref, gs_ref = refs
    r = pl.program_id(2)
    hd = C_HEAD_DIM

    @pl.when(r == 0)
    def _():
        for h in range(hb):
            if with_state:
                st_ref[h] = s0_ref[h].astype(F32).T
            else:
                st_ref[h] = jnp.zeros((hd, hd), F32)

    def step(ci, carry):
        base = pl.multiple_of(ci * chunk, chunk)
        sl = pl.ds(base, chunk)
        for h in range(hb):
            cs = slice(h * hd, (h + 1) * hd)
            o, st_new = _hgrn_chunk(q_ref[sl, cs], k_ref[sl, cs], lf_ref[sl, cs], v_ref[sl, cs],
                                    st_ref[h], gs_ref.at[h])
            st_ref[h] = st_new
            gate = g_ref[sl, cs]
            o = _rmsnorm_f32(o, gg_ref[...]) * (gate * _sigmoid(gate))
            o_ref[sl, cs] = o.astype(o_ref.dtype)
        return carry

    lax.fori_loop(0, n_chunks, step, 0)

    @pl.when(r == n_r - 1)
    def _():
        for h in range(hb):
            so_ref[h] = st_ref[h].T.astype(so_ref.dtype)


def hgrn_recurrence(q, k, lf, v, g, g_gain, state0, *, batch, seq, heads, rows, chunk, hb, out_dtype):
    n_r = seq // rows
    hd = C_HEAD_DIM
    with_state = state0 is not None
    row_spec = pl.BlockSpec((rows, hb * hd), lambda b, h, r: (b * n_r + r, h))
    state_spec = pl.BlockSpec((None, hb, hd, hd), lambda b, h, r: (b, h, 0, 0))
    in_specs = [row_spec] * 5 + [pl.BlockSpec((1, hd), lambda b, h, r: (0, 0))]
    args = [q, k, lf, v, g, g_gain.reshape(1, hd)]
    if with_state:
        in_specs.append(state_spec)
        args.append(state0)
    return pl.pallas_call(
        functools.partial(_hgrn_body, chunk=chunk, n_chunks=rows // chunk, n_r=n_r, hb=hb,
                          with_state=with_state),
        grid=(batch, heads // hb, n_r),
        in_specs=in_specs,
        out_specs=[row_spec, state_spec],
        out_shape=[jax.ShapeDtypeStruct((batch * seq, heads * hd), out_dtype),
                   jax.ShapeDtypeStruct((batch, heads, hd, hd), F32)],
        scratch_shapes=[pltpu.VMEM((hb, hd, hd), F32), pltpu.VMEM((hb, 3, chunk, hd), F32)],
        compiler_params=_cparams(3),
        name="hgrn_recurrence",
    )(*args)


def _tiles(m):
    big = m >= 1024
    return dict(
        norm_tm=512 if big else m,
        proj_tm=1024 if big else m,
        proj_tn=256,
        a_proj_tn=1024,
        out_tm=512 if big else m,
        ffn_tm=1024 if big else m,
        ffn_th=256,
        spatial_tm=256 if big else m,
        attn_tq=512,
        hgrn_rows=512,
        attn_heads=8,
        hgrn_heads=2 if big else 8,
    )


def _mixer_a_core(uv, v_gain, w_s, b_s, *, chunk_len, n_seq, tiles, with_vn):
    causal = jnp.tril(jnp.ones((A_CHUNK, A_CHUNK), bool))
    w_masked = jnp.where(causal[None], w_s, 0.0)
    if chunk_len == A_CHUNK:
        wm, bs, chunk = w_masked, b_s, A_CHUNK
    else:
        eye = jnp.eye(n_seq, dtype=w_s.dtype)
        small = w_masked[:, :chunk_len, :chunk_len]
        wm = jnp.einsum("ab,gts->gatbs", eye, small).reshape(
            A_GROUPS, n_seq * chunk_len, n_seq * chunk_len)
        bs = jnp.tile(b_s[:, :chunk_len], (1, n_seq))
        chunk = n_seq * chunk_len
    p, vn = spatial_mix(uv, wm.astype(BF16), bs[:, :, None], v_gain, chunk=chunk,
                        tm=max(tiles["spatial_tm"], chunk) if chunk_len == A_CHUNK else chunk,
                        with_vn=with_vn)
    return p, vn


def _mixer_a_proj(xn, xns, w_in, layer, *, tiles):
    width2 = w_in.shape[2]
    (uv,), (uv_s,) = seg_matmul(xn, xns, w_in, layer, (0,), width2, _gelu_epilogue, (BF16,),
                                tm=tiles["proj_tm"], tn=tiles["a_proj_tn"], name="a_in_proj")
    return uv, uv_s


def _mixer_b_proj(xn, xns, w_in, layer, q_gain, k_gain, *, heads, tiles):
    width = heads * 2 * B_HEAD_DIM
    tn = tiles["proj_tn"]
    lane_group = np.arange(tn) // B_HEAD_DIM
    group_ones = jnp.asarray(lane_group[:, None] == lane_group[None, :], BF16)
    reps = width // B_HEAD_DIM
    gq = jnp.tile(q_gain.astype(F32), reps).reshape(1, width)
    gk = jnp.tile(k_gain.astype(F32), reps).reshape(1, width)
    return seg_matmul(xn, xns, w_in, layer, (0, width, 2 * width), width, _headnorm_epilogue,
                      (F32, F32, F32), vecs=(gq, gk), consts=(group_ones,),
                      tm=tiles["proj_tm"], tn=tn, name="b_in_proj")


def _mixer_c_proj(xn, xns, w_in, layer, lower_bound, *, tiles):
    width = w_in.shape[2] // 4
    return seg_matmul(xn, xns, w_in, layer, (0, width, 2 * width, 3 * width), width, _hgrn_gate_epilogue,
                      (F32,) * 5, vecs=(lower_bound.reshape(1, width),),
                      tm=tiles["proj_tm"], tn=tiles["proj_tn"], name="c_in_proj")


def kernel(x_prompt, x_sample, cache_k, cache_v, page_table, state_hgrn, norm_mix, norm_ffn, ffn_w_gu, ffn_w_down, a_w_in, a_v_norm, a_w_s, a_b_s, a_w_out, b_w_in, b_q_norm, b_k_norm, b_lambda_q1, b_lambda_k1, b_lambda_q2, b_lambda_k2, b_subln, b_w_out, c_w_in, c_g_norm, c_lower_bounds, c_w_out):
    batch, seq, d_model = x_prompt.shape
    dec_b, dec_seq, _ = x_sample.shape
    depth = norm_mix.shape[0]
    b_heads = d_model // (2 * B_HEAD_DIM)
    c_heads = d_model // C_HEAD_DIM
    mp, ms = batch * seq, dec_b * dec_seq
    tp, ts = _tiles(mp), _tiles(ms)

    probs = jax.nn.softmax(c_lower_bounds.astype(F32), axis=0)
    lower_bound = jnp.cumsum(probs, axis=0) - probs[0]

    h_p = x_prompt.reshape(mp, d_model)
    h_s = x_sample.reshape(ms, d_model)
    xn_p = norm_rows(h_p, norm_mix[0], tm=tp["norm_tm"])
    xn_s = norm_rows(h_s, norm_mix[0], tm=ts["norm_tm"])

    n_phys = cache_k.shape[1]
    cache_k2 = cache_k.reshape(cache_k.shape[0], n_phys, PAGE_SIZE * b_heads, 2 * B_HEAD_DIM)
    cache_v2 = cache_v.reshape(cache_v.shape[0], n_phys, PAGE_SIZE * b_heads, B_V_DIM)

    k_p_rows, v_p_rows, k_s_rows, v_s_rows = [], [], [], []
    hgrn_p, hgrn_s, chunk_v_s = [], [], []
    for i in range(depth):
        kind, j = i % 3, i // 3
        if kind == 0:
            uv_p, uv_s = _mixer_a_proj(xn_p, xn_s, a_w_in, j, tiles=tp)
            y_p, _ = _mixer_a_core(uv_p, a_v_norm[j], a_w_s[j], a_b_s[j],
                                   chunk_len=A_CHUNK, n_seq=batch, tiles=tp, with_vn=False)
            y_s, vn_s = _mixer_a_core(uv_s, a_v_norm[j], a_w_s[j], a_b_s[j],
                                      chunk_len=dec_seq, n_seq=dec_b, tiles=ts, with_vn=True)
            chunk_v_s.append(vn_s.reshape(dec_b, dec_seq, -1))
            w_out = a_w_out
        elif kind == 1:
            lam_init = 0.8 - 0.6 * math.exp(-0.3 * i)
            lam = (jnp.exp(jnp.sum(b_lambda_q1[j].astype(F32) * b_lambda_k1[j].astype(F32)))
                   - jnp.exp(jnp.sum(b_lambda_q2[j].astype(F32) * b_lambda_k2[j].astype(F32)))
                   + lam_init)
            (q_p, k_p, v_p), (q_s, k_s, v_s) = _mixer_b_proj(
                xn_p, xn_s, b_w_in, j, b_q_norm[j], b_k_norm[j], heads=b_heads, tiles=tp)
            score_bound = (B_HEAD_DIM ** 0.5) * jnp.max(jnp.abs(b_q_norm[j].astype(F32))) * jnp.max(
                jnp.abs(b_k_norm[j].astype(F32)))
            attn = functools.partial(diff_attn_prompt, batch=batch, seq=seq, heads=b_heads,
                                     tq=tp["attn_tq"], hb=tp["attn_heads"])
            y_p = lax.cond(
                score_bound <= ATTN_FIXED_SHIFT_MAX_BOUND,
                lambda *a: attn(*a, lam_init, fixed_shift=True),
                lambda *a: attn(*a, lam_init, fixed_shift=False),
                q_p, k_p, v_p, lam, score_bound, b_subln[j])
            k_p_rows.append(k_p.reshape(batch, seq, b_heads, 2 * B_HEAD_DIM))
            v_p_rows.append(v_p.reshape(batch, seq, b_heads, B_V_DIM))

            q5 = (q_s *(B_HEAD_DIM ** -0.5)).reshape(dec_b, dec_seq, b_heads, 2, B_HEAD_DIM)
            qt = jnp.einsum("bthcd,ce->bhcted", q5, jnp.eye(2, dtype=F32))
            qt = qt.reshape(dec_b, b_heads // SUBLANES, SUBLANES * 2 * dec_seq, 2 * B_HEAD_DIM)
            qt = qt.transpose(0, 1, 3, 2)
            k_new = k_s.reshape(dec_b, dec_seq * b_heads, 2 * B_HEAD_DIM)
            v_new = v_s.reshape(dec_b, dec_seq * b_heads, B_V_DIM)
            o_s = diff_attn_decode(qt, cache_k2, cache_v2, page_table, k_new, v_new, lam,
                                   b_subln[j], lam_init, layer=j, heads=b_heads, n_q=dec_seq,
                                   pages=DECODE_PAGES_PER_STEP)
            y_s = o_s.transpose(0, 2, 1, 3).reshape(ms, d_model)
            k_s_rows.append(k_s.reshape(dec_b, dec_seq, b_heads, 2 * B_HEAD_DIM))
            v_s_rows.append(v_s.reshape(dec_b, dec_seq, b_heads, B_V_DIM))
            w_out = b_w_out
        else:
            qkv_p, qkv_s = _mixer_c_proj(xn_p, xn_s, c_w_in, j, lower_bound[i], tiles=tp)
            y_p, st_p = hgrn_recurrence(*qkv_p, c_g_norm[j], None, batch=batch, seq=seq,
                                        heads=c_heads, rows=tp["hgrn_rows"], chunk=C_HEAD_DIM,
                                        hb=tp["hgrn_heads"], out_dtype=BF16)
            y_s, st_s = hgrn_recurrence(*qkv_s, c_g_norm[j], state_hgrn[j], batch=dec_b,
                                        seq=dec_seq, heads=c_heads, rows=dec_seq, chunk=dec_seq,
                                        hb=ts["hgrn_heads"], out_dtype=F32)
            hgrn_p.append(st_p)
            hgrn_s.append(st_s)
            w_out = c_w_out
        h_p, xf_p, h_s, xf_s = out_proj(y_p, y_s, w_out, j, h_p, h_s, norm_ffn[i], tm=tp["out_tm"])
        gain_next = norm_mix[i + 1] if i + 1 < depth else None
        h_p, h_s, xn_p, xn_s = ffn(xf_p, h_p, xf_s, h_s, ffn_w_gu, ffn_w_down, i, gain_next,
                                   tm=tp["ffn_tm"], th=tp["ffn_th"])
    return (h_p.reshape(batch, seq, d_model), h_s.reshape(dec_b, dec_seq, d_model),
            jnp.stack(k_p_rows), jnp.stack(v_p_rows), jnp.stack(k_s_rows), jnp.stack(v_s_rows),
            jnp.stack(hgrn_p), jnp.stack(hgrn_s), jnp.stack(chunk_v_s))
```

```python
import functools
import math

import jax
import jax.numpy as jnp
import numpy as np
from jax import lax
from jax.experimental import pallas as pl
from jax.experimental.pallas import tpu as pltpu

F32 = jnp.float32
BF16 = jnp.bfloat16
EPS = 1e-6

LANES = 128
SUBLANES = 8
VMEM_LIMIT_BYTES = 56 << 20
FFN_VMEM_LIMIT_BYTES = 63 << 20
FFN_RESIDUAL_SLICES = 8

A_CHUNK = 128
A_GROUPS = 8
B_HEAD_DIM = 64
B_V_DIM = 2 * B_HEAD_DIM
C_HEAD_DIM = 128
PAGE_SIZE = 128
HGRN_SUB = SUBLANES
DECODE_PAGES_PER_STEP = 8
ATTN_FIXED_SHIFT_MAX_BOUND = 30.0


def _cparams(n_axes):
    return pltpu.CompilerParams(
        dimension_semantics=("arbitrary",) * n_axes,
        vmem_limit_bytes=VMEM_LIMIT_BYTES,
    )


def _rmsnorm_f32(x, gain):
    return x * lax.rsqrt(jnp.mean(x * x, axis=-1, keepdims=True) + EPS) * gain


def _sigmoid(x):
    return 1.0 / (1.0 + jnp.exp(-x))


def _norm_rows_body(x_ref, g_ref, o_ref):
    o_ref[...] = _rmsnorm_f32(x_ref[...], g_ref[...]).astype(o_ref.dtype)


def norm_rows(x, gain, *, tm):
    m, d = x.shape
    return pl.pallas_call(
        _norm_rows_body,
        grid=(m // tm,),
        in_specs=[pl.BlockSpec((tm, d), lambda i: (i, 0)),
                  pl.BlockSpec((1, d), lambda i: (0, 0))],
        out_specs=pl.BlockSpec((tm, d), lambda i: (i, 0)),
        out_shape=jax.ShapeDtypeStruct((m, d), BF16),
        compiler_params=_cparams(1),
        name="norm_rows",
    )(x, gain.reshape(1, d))


def _seg_matmul_body(*refs, ns, nv, nc, no, epilogue):
    x_ref, xs_ref = refs[0:2]
    w_refs = refs[2:2 + ns]
    vec_refs = refs[2 + ns:2 + ns + nv]
    const_refs = refs[2 + ns + nv:2 + ns + nv + nc]
    out_refs = refs[2 + ns + nv + nc:2 + ns + nv + nc + no]
    sample_out_refs = refs[2 + ns + nv + nc + no:-1]
    wb_ref = refs[-1]
    first_row_tile = pl.program_id(1) == 0

    @pl.when(first_row_tile)
    def _():
        for s in range(ns):
            wb_ref[s] = w_refs[s][...].astype(BF16)

    def project(rows_ref, dst_refs):
        x = rows_ref[...]
        accs = [jnp.dot(x, wb_ref[s], preferred_element_type=F32) for s in range(ns)]
        outs = epilogue(accs, [r[...] for r in vec_refs], [r[...] for r in const_refs])
        for r, o in zip(dst_refs, outs):
            r[...] = o.astype(r.dtype)

    project(x_ref, out_refs)

    @pl.when(first_row_tile)
    def _():
        project(xs_ref, sample_out_refs)


def seg_matmul(x, xs, w, layer, seg_starts, seg_width, epilogue, out_dtypes, vecs=(), consts=(),
               *, tm, tn, name):
    m, k = x.shape
    ms = xs.shape[0]
    ns = len(seg_starts)
    no = len(out_dtypes)
    in_specs = [pl.BlockSpec((tm, k), lambda j, i: (i, 0)),
                pl.BlockSpec((ms, k), lambda j, i: (0, 0))]
    for st in seg_starts:
        in_specs.append(pl.BlockSpec((None, k, tn), lambda j, i, off=st // tn: (layer, 0, off + j)))
    for _ in vecs:
        in_specs.append(pl.BlockSpec((1, tn), lambda j, i: (0, j)))
    for c in consts:
        in_specs.append(pl.BlockSpec(c.shape, lambda j, i, nd=c.ndim: (0,) * nd))
    res = pl.pallas_call(
        functools.partial(_seg_matmul_body, ns=ns, nv=len(vecs), nc=len(consts), no=no,
                          epilogue=epilogue),
        grid=(seg_width // tn, m // tm),
        in_specs=in_specs,
        out_specs=([pl.BlockSpec((tm, tn), lambda j, i: (i, j)) for _ in out_dtypes]
                   + [pl.BlockSpec((ms, tn), lambda j, i: (0, j)) for _ in out_dtypes]),
        out_shape=([jax.ShapeDtypeStruct((m, seg_width), dt) for dt in out_dtypes]
                   + [jax.ShapeDtypeStruct((ms, seg_width), dt) for dt in out_dtypes]),
        scratch_shapes=[pltpu.VMEM((ns, k, tn), BF16)],
        compiler_params=_cparams(2),
        name=name,
    )(x, xs, *([w] * ns), *vecs, *consts)
    return res[:no], res[no:]


def _gelu_exact_f32(a):
    z = a * (2.0 ** -0.5)
    az = jnp.abs(z)
    t = 1.0 / (1.0 + 0.3275911 * az)
    poly = t * (0.254829592 + t * (-0.284496736 + t * (1.421413741
                                                       + t * (-1.453152027 + t * 1.061405429))))
    erfc_abs = poly * jnp.exp(-az * az)
    return 0.5 * a * jnp.where(z >= 0, 2.0 - erfc_abs, erfc_abs)


def _gelu_epilogue(accs, vecs, consts):
    (a,) = accs
    return [_gelu_exact_f32(a)]


def _headnorm_epilogue(accs, vecs, consts):
    aq, ak, av = accs
    gq, gk = vecs
    (group_ones,) = consts

    def head_norm(a, g):
        ms = jnp.dot((a * a).astype(BF16), group_ones, preferred_element_type=F32) * (1.0 / B_HEAD_DIM)
        return a * lax.rsqrt(ms + EPS) * g

    return [head_norm(aq, gq), head_norm(ak, gk), av]


def _hgrn_gate_epilogue(accs, vecs, consts):
    aq, af, av, ag = accs
    (lb,) = vecs
    q = aq * _sigmoid(aq)
    sig = _sigmoid(af)
    f = lb + (1.0 - lb) * sig
    k = (1.0 - lb) * (1.0 - sig)
    return [q, k, jnp.log(f), av, ag]


def _out_proj_body(y_ref, ys_ref, w_ref, h_ref, hs_ref, g_ref, ho_ref, xo_ref, hso_ref, xso_ref,
                   wb_ref, *, cast_rows):
    first_row_tile = pl.program_id(0) == 0

    @pl.when(first_row_tile)
    def _():
        def cast(r, carry):
            sl = pl.ds(pl.multiple_of(r * cast_rows, cast_rows), cast_rows)
            wb_ref[sl, :] = w_ref[sl, :].astype(BF16)
            return carry
        lax.fori_loop(0, w_ref.shape[0] // cast_rows, cast, 0)

    def project(rows_ref, res_ref, h_out_ref, x_out_ref):
        hn = res_ref[...] + jnp.dot(rows_ref[...].astype(BF16), wb_ref[...],
                                    preferred_element_type=F32)
        h_out_ref[...] = hn
        x_out_ref[...] = _rmsnorm_f32(hn, g_ref[...]).astype(x_out_ref.dtype)

    project(y_ref, h_ref, ho_ref, xo_ref)

    @pl.when(first_row_tile)
    def _():
        project(ys_ref, hs_ref, hso_ref, xso_ref)


def out_proj(y, ys, w, layer, h, hs, gain_next, *, tm):
    m, k = y.shape
    ms = ys.shape[0]
    n = w.shape[2]
    row = lambda cols: pl.BlockSpec((tm, cols), lambda i: (i, 0))
    sample = lambda cols: pl.BlockSpec((ms, cols), lambda i: (0, 0))
    return pl.pallas_call(
        functools.partial(_out_proj_body, cast_rows=256),
        grid=(m // tm,),
        in_specs=[row(k), sample(k),
                  pl.BlockSpec((None, k, n), lambda i: (layer, 0, 0), pipeline_mode=pl.Buffered(1)),
                  row(n), sample(n),
                  pl.BlockSpec((1, n), lambda i: (0, 0))],
        out_specs=[row(n), row(n), sample(n), sample(n)],
        out_shape=[jax.ShapeDtypeStruct((m, n), F32), jax.ShapeDtypeStruct((m, n), BF16),
                   jax.ShapeDtypeStruct((ms, n), F32), jax.ShapeDtypeStruct((ms, n), BF16)],
        scratch_shapes=[pltpu.VMEM((k, n), BF16)],
        compiler_params=_cparams(1),
        name="out_proj",
    )(y, ys, w, h, hs, gain_next.reshape(1, n))


def _ffn_body(x_ref, h_ref, xs_ref, hs_ref, wg_ref, wu_ref, wd_ref, g_ref, *out_refs, n_t, with_norm):
    if with_norm:
        ho_ref, hso_ref, xo_ref, xso_ref = out_refs
    else:
        ho_ref, hso_ref = out_refs
    i = pl.program_id(0)
    t = pl.program_id(1)

    @pl.when(t == 0)
    def _():
        ho_ref[...] = jnp.zeros(ho_ref.shape, F32)

    slice_rows = h_ref.shape[0]
    for p in range(FFN_RESIDUAL_SLICES):
        @pl.when(t == p)
        def _(p=p):
            ho_ref[p * slice_rows:(p + 1) * slice_rows, :] += h_ref[...]

    def swiglu(x):
        gate = jnp.dot(x, wg_ref[...].astype(BF16), preferred_element_type=F32)
        up = jnp.dot(x, wu_ref[...].astype(BF16), preferred_element_type=F32)
        act = (gate * _sigmoid(gate) * up).astype(BF16)
        return jnp.dot(act, wd_ref[...].astype(BF16), preferred_element_type=F32)

    ho_ref[...] += swiglu(x_ref[...])

    @pl.when((i == 0) & (t == 0))
    def _():
        hso_ref[...] = hs_ref[...]

    @pl.when(i == 0)
    def _():
        hso_ref[...] += swiglu(xs_ref[...])

    if with_norm:
        @pl.when(t == n_t - 1)
        def _():
            xo_ref[...] = _rmsnorm_f32(ho_ref[...], g_ref[...]).astype(BF16)

        @pl.when((i == 0) & (t == n_t - 1))
        def _():
            xso_ref[...] = _rmsnorm_f32(hso_ref[...], g_ref[...]).astype(BF16)


def ffn(x, h, xs, hs, w_gu, w_down, layer, gain_next, *, tm, th):
    m, d = x.shape
    ms = xs.shape[0]
    hidden = w_down.shape[1]
    n_t = hidden // th
    with_norm = gain_next is not None
    gain = gain_next if with_norm else jnp.ones((d,), F32)
    row_out = pl.BlockSpec((tm, d), lambda i, t: (i, 0))
    sample_block = pl.BlockSpec((ms, d), lambda i, t: (0, 0))
    out_specs = [row_out, sample_block]
    out_shape = [jax.ShapeDtypeStruct((m, d), F32), jax.ShapeDtypeStruct((ms, d), F32)]
    if with_norm:
        out_specs += [row_out, sample_block]
        out_shape += [jax.ShapeDtypeStruct((m, d), BF16), jax.ShapeDtypeStruct((ms, d), BF16)]
    assert n_t >= FFN_RESIDUAL_SLICES and tm % (FFN_RESIDUAL_SLICES * SUBLANES) == 0
    row_block = pl.BlockSpec((tm, d), lambda i, t: (i, 0), pipeline_mode=pl.Buffered(1))
    last_slice = FFN_RESIDUAL_SLICES - 1
    residual_slice = pl.BlockSpec(
        (tm // FFN_RESIDUAL_SLICES, d),
        lambda i, t: (i * FFN_RESIDUAL_SLICES + jnp.minimum(t, last_slice), 0))
    res = pl.pallas_call(
        functools.partial(_ffn_body, n_t=n_t, with_norm=with_norm),
        grid=(m // tm, n_t),
        in_specs=[row_block, residual_slice, sample_block, sample_block,
                  pl.BlockSpec((None, d, th), lambda i, t: (layer, 0, t)),
                  pl.BlockSpec((None, d, th), lambda i, t: (layer, 0, n_t + t)),
                  pl.BlockSpec((None, th, d), lambda i, t: (layer, t, 0)),
                  pl.BlockSpec((1, d), lambda i, t: (0, 0))],
        out_specs=out_specs,
        out_shape=out_shape,
        compiler_params=pltpu.CompilerParams(dimension_semantics=("arbitrary",) * 2,
                                             vmem_limit_bytes=FFN_VMEM_LIMIT_BYTES),
        name="ffn",
    )(x, h, xs, hs, w_gu, w_gu, w_down, gain.reshape(1, d))
    return tuple(res) if with_norm else (res[0], res[1], None, None)


def _spatial_body(u_ref, v_ref, wm_ref, bs_ref, vg_ref, p_ref, *maybe_vn_ref, chunk, groups):
    v = v_ref[...].astype(F32)
    vn = _rmsnorm_f32(v, vg_ref[...])
    if maybe_vn_ref:
        maybe_vn_ref[0][...] = vn
    vnb = vn.astype(BF16)
    rows, width = v.shape
    gw = width // groups
    for c in range(rows // chunk):
        r0 = c * chunk
        for g in range(groups):
            c0 = g * gw
            s = jnp.dot(wm_ref[g], vnb[r0:r0 + chunk, c0:c0 + gw], preferred_element_type=F32)
            s = s + bs_ref[g]
            u = u_ref[r0:r0 + chunk, c0:c0 + gw].astype(F32)
            p_ref[r0:r0 + chunk, c0:c0 + gw] = (u * s).astype(p_ref.dtype)


def spatial_mix(uv, wm, bs, v_gain, *, chunk, tm, with_vn):
    m, w2 = uv.shape
    width = w2 // 2
    groups = wm.shape[0]
    row_out = pl.BlockSpec((tm, width), lambda i: (i, 0))
    out_specs = [row_out]
    out_shape = [jax.ShapeDtypeStruct((m, width), BF16)]
    if with_vn:
        out_specs.append(row_out)
        out_shape.append(jax.ShapeDtypeStruct((m, width), F32))
    res = pl.pallas_call(
        functools.partial(_spatial_body, chunk=chunk, groups=groups),
        grid=(m // tm,),
        in_specs=[pl.BlockSpec((tm, width), lambda i: (i, 0)),
                  pl.BlockSpec((tm, width), lambda i: (i, 1)),
                  pl.BlockSpec(wm.shape, lambda i: (0, 0, 0)),
                  pl.BlockSpec(bs.shape, lambda i: (0, 0, 0)),
                  pl.BlockSpec((1, width), lambda i: (0, 0))],
        out_specs=out_specs,
        out_shape=out_shape,
        compiler_params=_cparams(1),
        name="spatial_mix",
    )(uv, uv, wm, bs, v_gain.reshape(1, width))
    return (res[0], res[1]) if with_vn else (res[0], None)


def _diff_attn_body(qt_tab, kt_tab, q_ref, k_ref, v_ref, sc_ref, sub_ref, o_ref,
                    qs_ref, m_ref, l_ref, acc_ref, *, tq, hb, out_scale, fixed_shift):
    t = pl.program_id(2)
    qi = qt_tab[t]
    ki = kt_tab[t]
    hd = 2 * B_HEAD_DIM

    @pl.when(ki == 0)
    def _():
        for h in range(hb):
            q = q_ref[:, h * hd:(h + 1) * hd].astype(F32) * (B_HEAD_DIM ** -0.5)
            lane = lax.broadcasted_iota(jnp.int32, q.shape, 1)
            qs_ref[h, 0:tq, :] = jnp.where(lane < B_HEAD_DIM, q, 0.0)
            qs_ref[h, tq:2 * tq, :] = jnp.where(lane >= B_HEAD_DIM, q, 0.0)
        if not fixed_shift:
            m_ref[...] = jnp.full(m_ref.shape, -jnp.inf, F32)
        l_ref[...] = jnp.zeros(l_ref.shape, F32)
        acc_ref[...] = jnp.zeros(acc_ref.shape, F32)

    def strip(h, r0, masked):
        rs = pl.ds(r0, tq)
        keys = k_ref[:, h * hd:(h + 1) * hd]
        vals = v_ref[:, h * B_V_DIM:(h + 1) * B_V_DIM]
        s = lax.dot_general(qs_ref[h, rs, :], keys, (((1,), (1,)), ((), ())),
                            preferred_element_type=F32)
        if masked:
            row = lax.broadcasted_iota(jnp.int32, s.shape, 0)
            col = lax.broadcasted_iota(jnp.int32, s.shape, 1)
            s = jnp.where(col <= row, s, -jnp.inf)
        if fixed_shift:
            p = jnp.exp(s - sc_ref[1])
            part = p[:, 0:LANES]
            for c0 in range(LANES, p.shape[1], LANES):
                part = part + p[:, c0:c0 + LANES]
            l_ref[h, rs, :] += part
            acc_ref[h, rs, :] += jnp.dot(p, vals, preferred_element_type=F32)
        else:
            m_prev = m_ref[h, rs, :]
            m_new = jnp.maximum(m_prev, jnp.max(s, axis=1, keepdims=True))
            alpha = jnp.exp(m_prev - m_new)
            p = jnp.exp(s - m_new[:, 0:1])
            l_ref[h, rs, :] = alpha * l_ref[h, rs, :] + jnp.sum(p, axis=1, keepdims=True)
            acc_ref[h, rs, :] = alpha * acc_ref[h, rs, :] + jnp.dot(p, vals, preferred_element_type=F32)
            m_ref[h, rs, :] = m_new

    @pl.when(ki < qi)
    def _():
        for h in range(hb):
            strip(h, 0, False)
            strip(h, tq, False)

    @pl.when(ki == qi)
    def _():
        for h in range(hb):
            strip(h, 0, True)
            strip(h, tq, True)
            l = l_ref[h]
            if fixed_shift:
                l = jnp.sum(l, axis=1, keepdims=True)
            o = acc_ref[h] / l
            d = o[0:tq] - sc_ref[0] * o[tq:2 * tq]
            o_ref[:, h * B_V_DIM:(h + 1) * B_V_DIM] = (
                _rmsnorm_f32(d, sub_ref[...]) * out_scale).astype(o_ref.dtype)


def diff_attn_prompt(q, k, v, lam, score_bound, subln, lam_init, *, batch, seq, heads, tq, hb,
                     fixed_shift):
    nq = seq // tq
    tri = [(qi, ki) for qi in range(nq) for ki in range(qi + 1)]
    qt_tab = jnp.asarray([a for a, _ in tri], jnp.int32)
    kt_tab = jnp.asarray([b for _, b in tri], jnp.int32)
    hd = 2 * B_HEAD_DIM
    grid_spec = pltpu.PrefetchScalarGridSpec(
        num_scalar_prefetch=2,
        grid=(batch, heads // hb, len(tri)),
        in_specs=[pl.BlockSpec((tq, hb * hd), lambda b, h, t, qt, kt: (b * nq + qt[t], h)),
                  pl.BlockSpec((tq, hb * hd), lambda b, h, t, qt, kt: (b * nq + kt[t], h)),
                  pl.BlockSpec((tq, hb * B_V_DIM), lambda b, h, t, qt, kt: (b * nq + kt[t], h)),
                  pl.BlockSpec(memory_space=pltpu.SMEM),
                  pl.BlockSpec((1, B_V_DIM), lambda b, h, t, qt, kt: (0, 0))],
        out_specs=pl.BlockSpec((tq, hb * B_V_DIM), lambda b, h, t, qt, kt: (b * nq + qt[t], h)),
        scratch_shapes=[pltpu.VMEM((hb, 2 * tq, hd), F32),
                        pltpu.VMEM((hb, 2 * tq, LANES), F32),
                        pltpu.VMEM((hb, 2 * tq, LANES), F32),
                        pltpu.VMEM((hb, 2 * tq, B_V_DIM), F32)],
    )
    return pl.pallas_call(
        functools.partial(_diff_attn_body, tq=tq, hb=hb, out_scale=1.0 - lam_init,
                          fixed_shift=fixed_shift),
        grid_spec=grid_spec,
        out_shape=jax.ShapeDtypeStruct((batch * seq, heads * B_V_DIM), BF16),
        compiler_params=_cparams(3),
        name="diff_attn_prompt_fixed_shift" if fixed_shift else "diff_attn_prompt",
    )(qt_tab, kt_tab, q, k, v, jnp.stack([lam, score_bound]).astype(F32),
      subln.reshape(1, B_V_DIM))


def _decode_attn_body(pt_ref, *refs, pages, heads, n_q, n_groups, out_scale):
    k_refs = refs[:pages]
    v_refs = refs[pages:2 * pages]
    kn_ref, vn_ref, qt_ref, lam_ref, sub_ref, o_ref, m_ref, l_ref, acc_ref = refs[2 * pages:]
    g = pl.program_id(1)
    hg = heads // SUBLANES
    hc = 2 * n_q
    cols = SUBLANES * hc

    @pl.when(g == 0)
    def _():
        m_ref[...] = jnp.full(m_ref.shape, -jnp.inf, F32)
        l_ref[...] = jnp.zeros(l_ref.shape, F32)
        acc_ref[...] = jnp.zeros(acc_ref.shape, F32)

    sub = lax.broadcasted_iota(jnp.int32, (SUBLANES, cols), 0)
    lane = lax.broadcasted_iota(jnp.int32, (SUBLANES, cols), 1)
    own = sub == lane // hc

    def to_column(x8):
        r = jnp.sum(jnp.where(own, x8, 0.0), axis=0, keepdims=True)
        return jnp.broadcast_to(r, (LANES, cols)).T

    def group_rows(ref, j, n_pos):
        x = ref[0:n_pos * heads, :].reshape(n_pos, hg, SUBLANES, 2 * B_HEAD_DIM)
        return x[:, j].reshape(n_pos * SUBLANES, 2 * B_HEAD_DIM)

    def process(page_k_refs, page_v_refs, n_pos, new_tokens):
        for j in range(hg):
            scores = []
            for k_ref in page_k_refs:
                s = jnp.dot(group_rows(k_ref, j, n_pos), qt_ref[j], preferred_element_type=F32)
                s = s.reshape(n_pos, SUBLANES, cols)
                valid = own[None]
                if new_tokens:
                    pos = lax.broadcasted_iota(jnp.int32, s.shape, 0)
                    qry = lax.broadcasted_iota(jnp.int32, s.shape, 2) % n_q
                    valid = valid & (pos <= qry)
                scores.append(jnp.where(valid, s, -jnp.inf))
            m_prev = m_ref[j]
            m_new = m_prev
            for s in scores:
                m_new = jnp.maximum(m_new, jnp.max(s, axis=0))
            m_safe = jnp.where(own, m_new, 0.0)
            alpha = jnp.exp(m_prev - m_safe)
            l_new = alpha * l_ref[j]
            pv = jnp.zeros((cols, B_V_DIM), F32)
            for s, v_ref in zip(scores, page_v_refs):
                p = jnp.exp(s - m_safe[None])
                l_new = l_new + jnp.sum(p, axis=0)
                pv = pv + lax.dot_general(p.reshape(n_pos * SUBLANES, cols), group_rows(v_ref, j, n_pos),
                                          (((0,), (0,)), ((), ())), preferred_element_type=F32)
            acc_ref[j] = acc_ref[j] * to_column(alpha) + pv
            l_ref[j] = l_new
            m_ref[j] = m_new

    process(k_refs, v_refs, PAGE_SIZE, False)

    @pl.when(g == n_groups - 1)
    def _():
        process([kn_ref], [vn_ref], n_q, True)
        for j in range(hg):
            o = acc_ref[j] / to_column(l_ref[j])
            o = o.reshape(SUBLANES, 2, n_q, B_V_DIM)
            d = o[:, 0] - lam_ref[0] * o[:, 1]
            d = d * lax.rsqrt(jnp.mean(d * d, axis=-1, keepdims=True) + EPS) * sub_ref[...]
            o_ref[j * SUBLANES:(j + 1) * SUBLANES] = (d * out_scale).astype(o_ref.dtype)


def diff_attn_decode(qt, cache_k, cache_v, page_table, k_new, v_new, lam, subln, lam_init,
                     *, layer, heads, n_q, pages):
    dec_b, n_pages = page_table.shape
    hd = 2 * B_HEAD_DIM
    rows = PAGE_SIZE * heads
    hg = heads // SUBLANES
    cols = SUBLANES * 2 * n_q
    n_groups = n_pages // pages
    page_spec = lambda p_i: pl.BlockSpec(
        (None, None, rows, hd),
        lambda b, g, pt, p_i=p_i: (layer, pt[b, g * pages + p_i], 0, 0))
    grid_spec = pltpu.PrefetchScalarGridSpec(
        num_scalar_prefetch=1,
        grid=(dec_b, n_groups),
        in_specs=([page_spec(p_i) for p_i in range(pages)] * 2
                  + [pl.BlockSpec((None, n_q * heads, hd), lambda b, g, pt: (b, 0, 0)),
                     pl.BlockSpec((None, n_q * heads, hd), lambda b, g, pt: (b, 0, 0)),
                     pl.BlockSpec((None, hg, hd, cols), lambda b, g, pt: (b, 0, 0, 0)),
                     pl.BlockSpec(memory_space=pltpu.SMEM),
                     pl.BlockSpec((1, B_V_DIM), lambda b, g, pt: (0, 0))]),
        out_specs=pl.BlockSpec((None, heads, n_q, B_V_DIM), lambda b, g, pt: (b, 0, 0, 0)),
        scratch_shapes=[pltpu.VMEM((hg, SUBLANES, cols), F32),
                        pltpu.VMEM((hg, SUBLANES, cols), F32),
                        pltpu.VMEM((hg, cols, B_V_DIM), F32)],
    )
    return pl.pallas_call(
        functools.partial(_decode_attn_body, pages=pages, heads=heads, n_q=n_q,
                          n_groups=n_groups, out_scale=1.0 - lam_init),
        grid_spec=grid_spec,
        out_shape=jax.ShapeDtypeStruct((dec_b, heads, n_q, B_V_DIM), F32),
        compiler_params=_cparams(2),
        name="diff_attn_decode",
    )(page_table, *([cache_k] * pages), *([cache_v] * pages), k_new, v_new, qt,
      lam.reshape(1), subln.reshape(1, B_V_DIM))


def _cumsum_rows(x):
    c = x.shape[0]
    sub = lax.broadcasted_iota(jnp.int32, x.shape, 0) % HGRN_SUB
    d = 1
    while d < HGRN_SUB:
        x = x + jnp.where(sub >= d, pltpu.roll(x, d, axis=0), 0.0)
        d *= 2
    blocks = []
    carry = None
    for j in range(c // HGRN_SUB):
        blk = x[j * HGRN_SUB:(j + 1) * HGRN_SUB]
        if carry is not None:
            blk = blk + carry
        blocks.append(blk)
        carry = blk[HGRN_SUB - 1:HGRN_SUB]
    return jnp.concatenate(blocks, axis=0) if len(blocks) > 1 else blocks[0]


def _hgrn_chunk(q, k, lf, v, st, row_scr):
    c = q.shape[0]
    nb = c // HGRN_SUB
    gcum = _cumsum_rows(lf)
    row_scr[0] = gcum
    row_scr[1] = k
    row_scr[2] = v
    o = lax.dot_general(q * jnp.exp(gcum), st, (((1,), (1,)), ((), ())), preferred_element_type=F32)

    if nb > 1:
        row = lax.broadcasted_iota(jnp.int32, (c, C_HEAD_DIM), 0)
        t_idx = lax.broadcasted_iota(jnp.int32, (c, c), 0)
        s_idx = lax.broadcasted_iota(jnp.int32, (c, c), 1)
        a_off = None
        size = 2 * HGRN_SUB
        while size <= c:
            half = size // 2
            if size < c:
                g_mid = jnp.concatenate(
                    [jnp.broadcast_to(gcum[b0 + half - 1:b0 + half], (size, C_HEAD_DIM))
                     for b0 in range(0, c, size)], axis=0)
            else:
                g_mid = gcum[half - 1:half]
            upper = (row % size) >= half
            qd = q * jnp.exp(jnp.where(upper, gcum - g_mid, -jnp.inf))
            kd = k * jnp.exp(jnp.where(upper, -jnp.inf, g_mid - gcum))
            a = lax.dot_general(qd, kd, (((1,), (1,)), ((), ())), preferred_element_type=F32)
            if size < c:
                a = jnp.where((t_idx // size) == (s_idx // size), a, 0.0)
            a_off = a if a_off is None else a_off + a
            size *= 2
        o = o + jnp.dot(a_off, v, preferred_element_type=F32)

    sub_row = lax.broadcasted_iota(jnp.int32, (HGRN_SUB, C_HEAD_DIM), 0)
    o_blocks = []
    for i in range(nb):
        r0 = i * HGRN_SUB
        gi = gcum[r0:r0 + HGRN_SUB]
        qi = q[r0:r0 + HGRN_SUB]
        oi = o[r0:r0 + HGRN_SUB]
        for s in range(HGRN_SUB):
            r = r0 + s
            dec = jnp.exp(jnp.where(sub_row >= s, gi - row_scr[0, r:r + 1, :], -jnp.inf))
            a_col = jnp.sum(qi * row_scr[1, r:r + 1, :] * dec, axis=-1, keepdims=True)
            oi = oi + a_col * row_scr[2, r:r + 1, :]
        o_blocks.append(oi)
    o = jnp.concatenate(o_blocks, axis=0) if nb > 1 else o_blocks[0]
    g_last = gcum[c - 1:c]
    kd = k * jnp.exp(g_last - gcum)
    if c < C_HEAD_DIM:
        pad = jnp.zeros((C_HEAD_DIM - c, C_HEAD_DIM), F32)
        kd = jnp.concatenate([kd, pad], axis=0)
        v = jnp.concatenate([v, pad], axis=0)
    st_new = st * jnp.exp(g_last) + jnp.dot(v.T, kd, preferred_element_type=F32)
    return o, st_new


def _hgrn_body(*refs, chunk, n_chunks, n_r, hb, with_state):
    if with_state:
        q_ref, k_ref, lf_ref, v_ref, g_ref, gg_ref, s0_ref, o_ref, so_ref, st_ref, gs_ref = refs
    else:
        q_ref, k_ref, lf_ref, v_ref, g_ref, gg_ref, o_ref, so_ref, st_ref, gs_ref = refs
    r = pl.program_id(2)
    hd = C_HEAD_DIM

    @pl.when(r == 0)
    def _():
        for h in range(hb):
            if with_state:
                st_ref[h] = s0_ref[h].astype(F32).T
            else:
                st_ref[h] = jnp.zeros((hd, hd), F32)

    def step(ci, carry):
        base = pl.multiple_of(ci * chunk, chunk)
        sl = pl.ds(base, chunk)
        for h in range(hb):
            cs = slice(h * hd, (h + 1) * hd)
            o, st_new = _hgrn_chunk(q_ref[sl, cs], k_ref[sl, cs], lf_ref[sl, cs], v_ref[sl, cs],
                                    st_ref[h], gs_ref.at[h])
            st_ref[h] = st_new
            gate = g_ref[sl, cs]
            o = _rmsnorm_f32(o, gg_ref[...]) * (gate * _sigmoid(gate))
            o_ref[sl, cs] = o.astype(o_ref.dtype)
        return carry

    lax.fori_loop(0, n_chunks, step, 0)

    @pl.when(r == n_r - 1)
    def _():
        for h in range(hb):
            so_ref[h] = st_ref[h].T.astype(so_ref.dtype)


def hgrn_recurrence(q, k, lf, v, g, g_gain, state0, *, batch, seq, heads, rows, chunk, hb, out_dtype):
    n_r = seq // rows
    hd = C_HEAD_DIM
    with_state = state0 is not None
    row_spec = pl.BlockSpec((rows, hb * hd), lambda b, h, r: (b * n_r + r, h))
    state_spec = pl.BlockSpec((None, hb, hd, hd), lambda b, h, r: (b, h, 0, 0))
    in_specs = [row_spec] * 5 + [pl.BlockSpec((1, hd), lambda b, h, r: (0, 0))]
    args = [q, k, lf, v, g, g_gain.reshape(1, hd)]
    if with_state:
        in_specs.append(state_spec)
        args.append(state0)
    return pl.pallas_call(
        functools.partial(_hgrn_body, chunk=chunk, n_chunks=rows // chunk, n_r=n_r, hb=hb,
                          with_state=with_state),
        grid=(batch, heads // hb, n_r),
        in_specs=in_specs,
        out_specs=[row_spec, state_spec],
        out_shape=[jax.ShapeDtypeStruct((batch * seq, heads * hd), out_dtype),
                   jax.ShapeDtypeStruct((batch, heads, hd, hd), F32)],
        scratch_shapes=[pltpu.VMEM((hb, hd, hd), F32), pltpu.VMEM((hb, 3, chunk, hd), F32)],
        compiler_params=_cparams(3),
        name="hgrn_recurrence",
    )(*args)


def _tiles(m):
    big = m >= 1024
    return dict(
        norm_tm=512 if big else m,
        proj_tm=1024 if big else m,
        proj_tn=256,
        a_proj_tn=1024,
        out_tm=512 if big else m,
        ffn_tm=1024 if big else m,
        ffn_th=512,
        spatial_tm=256 if big else m,
        attn_tq=512,
        hgrn_rows=512,
        attn_heads=8,
        hgrn_heads=2 if big else 8,
    )


def _mixer_a_core(uv, v_gain, w_s, b_s, *, chunk_len, n_seq, tiles, with_vn):
    causal = jnp.tril(jnp.ones((A_CHUNK, A_CHUNK), bool))
    w_masked = jnp.where(causal[None], w_s, 0.0)
    if chunk_len == A_CHUNK:
        wm, bs, chunk = w_masked, b_s, A_CHUNK
    else:
        eye = jnp.eye(n_seq, dtype=w_s.dtype)
        small = w_masked[:, :chunk_len, :chunk_len]
        wm = jnp.einsum("ab,gts->gatbs", eye, small).reshape(
            A_GROUPS, n_seq * chunk_len, n_seq * chunk_len)
        bs = jnp.tile(b_s[:, :chunk_len], (1, n_seq))
        chunk = n_seq * chunk_len
    p, vn = spatial_mix(uv, wm.astype(BF16), bs[:, :, None], v_gain, chunk=chunk,
                        tm=max(tiles["spatial_tm"], chunk) if chunk_len == A_CHUNK else chunk,
                        with_vn=with_vn)
    return p, vn


def _mixer_a_proj(xn, xns, w_in, layer, *, tiles):
    width2 = w_in.shape[2]
    (uv,), (uv_s,) = seg_matmul(xn, xns, w_in, layer, (0,), width2, _gelu_epilogue, (BF16,),
                                tm=tiles["proj_tm"], tn=tiles["a_proj_tn"], name="a_in_proj")
    return uv, uv_s


def _mixer_b_proj(xn, xns, w_in, layer, q_gain, k_gain, *, heads, tiles):
    width = heads * 2 * B_HEAD_DIM
    tn = tiles["proj_tn"]
    lane_group = np.arange(tn) // B_HEAD_DIM
    group_ones = jnp.asarray(lane_group[:, None] == lane_group[None, :], BF16)
    reps = width // B_HEAD_DIM
    gq = jnp.tile(q_gain.astype(F32), reps).reshape(1, width)
    gk = jnp.tile(k_gain.astype(F32), reps).reshape(1, width)
    return seg_matmul(xn, xns, w_in, layer, (0, width, 2 * width), width, _headnorm_epilogue,
                      (F32, F32, F32), vecs=(gq, gk), consts=(group_ones,),
                      tm=tiles["proj_tm"], tn=tn, name="b_in_proj")


def _mixer_c_proj(xn, xns, w_in, layer, lower_bound, *, tiles):
    width = w_in.shape[2] // 4
    return seg_matmul(xn, xns, w_in, layer, (0, width, 2 * width, 3 * width), width, _hgrn_gate_epilogue,
                      (F32,) * 5, vecs=(lower_bound.reshape(1, width),),
                      tm=tiles["proj_tm"], tn=tiles["proj_tn"], name="c_in_proj")


def kernel(x_prompt, x_sample, cache_k, cache_v, page_table, state_hgrn, norm_mix, norm_ffn, ffn_w_gu, ffn_w_down, a_w_in, a_v_norm, a_w_s, a_b_s, a_w_out, b_w_in, b_q_norm, b_k_norm, b_lambda_q1, b_lambda_k1, b_lambda_q2, b_lambda_k2, b_subln, b_w_out, c_w_in, c_g_norm, c_lower_bounds, c_w_out):
    batch, seq, d_model = x_prompt.shape
    dec_b, dec_seq, _ = x_sample.shape
    depth = norm_mix.shape[0]
    b_heads = d_model // (2 * B_HEAD_DIM)
    c_heads = d_model // C_HEAD_DIM
    mp, ms = batch * seq, dec_b * dec_seq
    tp, ts = _tiles(mp), _tiles(ms)

    probs = jax.nn.softmax(c_lower_bounds.astype(F32), axis=0)
    lower_bound = jnp.cumsum(probs, axis=0) - probs[0]

    h_p = x_prompt.reshape(mp, d_model)
    h_s = x_sample.reshape(ms, d_model)
    xn_p = norm_rows(h_p, norm_mix[0], tm=tp["norm_tm"])
    xn_s = norm_rows(h_s, norm_mix[0], tm=ts["norm_tm"])

    n_phys = cache_k.shape[1]
    cache_k2 = cache_k.reshape(cache_k.shape[0], n_phys, PAGE_SIZE * b_heads, 2 * B_HEAD_DIM)
    cache_v2 = cache_v.reshape(cache_v.shape[0], n_phys, PAGE_SIZE * b_heads, B_V_DIM)

    k_p_rows, v_p_rows, k_s_rows, v_s_rows = [], [], [], []
    hgrn_p, hgrn_s, chunk_v_s = [], [], []
    for i in range(depth):
        kind, j = i % 3, i // 3
        if kind == 0:
            uv_p, uv_s = _mixer_a_proj(xn_p, xn_s, a_w_in, j, tiles=tp)
            y_p, _ = _mixer_a_core(uv_p, a_v_norm[j], a_w_s[j], a_b_s[j],
                                   chunk_len=A_CHUNK, n_seq=batch, tiles=tp, with_vn=False)
            y_s, vn_s = _mixer_a_core(uv_s, a_v_norm[j], a_w_s[j], a_b_s[j],
                                      chunk_len=dec_seq, n_seq=dec_b, tiles=ts, with_vn=True)
            chunk_v_s.append(vn_s.reshape(dec_b, dec_seq, -1))
            w_out = a_w_out
        elif kind == 1:
            lam_init = 0.8 - 0.6 * math.exp(-0.3 * i)
            lam = (jnp.exp(jnp.sum(b_lambda_q1[j].astype(F32) * b_lambda_k1[j].astype(F32)))
                   - jnp.exp(jnp.sum(b_lambda_q2[j].astype(F32) * b_lambda_k2[j].astype(F32)))
                   + lam_init)
            (q_p, k_p, v_p), (q_s, k_s, v_s) = _mixer_b_proj(
                xn_p, xn_s, b_w_in, j, b_q_norm[j], b_k_norm[j], heads=b_heads, tiles=tp)
            score_bound = (B_HEAD_DIM ** 0.5) * jnp.max(jnp.abs(b_q_norm[j].astype(F32))) * jnp.max(
                jnp.abs(b_k_norm[j].astype(F32)))
            attn = functools.partial(diff_attn_prompt, batch=batch, seq=seq, heads=b_heads,
                                     tq=tp["attn_tq"], hb=tp["attn_heads"])
            y_p = lax.cond(
                score_bound <= ATTN_FIXED_SHIFT_MAX_BOUND,
                lambda *a: attn(*a, lam_init, fixed_shift=True),
                lambda *a: attn(*a, lam_init, fixed_shift=False),
                q_p, k_p, v_p, lam, score_bound, b_subln[j])
            k_p_rows.append(k_p.reshape(batch, seq, b_heads, 2 * B_HEAD_DIM))
            v_p_rows.append(v_p.reshape(batch, seq, b_heads, B_V_DIM))

            q5 = (q_s *(B_HEAD_DIM ** -0.5)).reshape(dec_b, dec_seq, b_heads, 2, B_HEAD_DIM)
            qt = jnp.einsum("bthcd,ce->bhcted", q5, jnp.eye(2, dtype=F32))
            qt = qt.reshape(dec_b, b_heads // SUBLANES, SUBLANES * 2 * dec_seq, 2 * B_HEAD_DIM)
            qt = qt.transpose(0, 1, 3, 2)
            k_new = k_s.reshape(dec_b, dec_seq * b_heads, 2 * B_HEAD_DIM)
            v_new = v_s.reshape(dec_b, dec_seq * b_heads, B_V_DIM)
            o_s = diff_attn_decode(qt, cache_k2, cache_v2, page_table, k_new, v_new, lam,
                                   b_subln[j], lam_init, layer=j, heads=b_heads, n_q=dec_seq,
                                   pages=DECODE_PAGES_PER_STEP)
            y_s = o_s.transpose(0, 2, 1, 3).reshape(ms, d_model)
            k_s_rows.append(k_s.reshape(dec_b, dec_seq, b_heads, 2 * B_HEAD_DIM))
            v_s_rows.append(v_s.reshape(dec_b, dec_seq, b_heads, B_V_DIM))
            w_out = b_w_out
        else:
            qkv_p, qkv_s = _mixer_c_proj(xn_p, xn_s, c_w_in, j, lower_bound[i], tiles=tp)
            y_p, st_p = hgrn_recurrence(*qkv_p, c_g_norm[j], None, batch=batch, seq=seq,
                                        heads=c_heads, rows=tp["hgrn_rows"], chunk=C_HEAD_DIM,
                                        hb=tp["hgrn_heads"], out_dtype=BF16)
            y_s, st_s = hgrn_recurrence(*qkv_s, c_g_norm[j], state_hgrn[j], batch=dec_b,
                                        seq=dec_seq, heads=c_heads, rows=dec_seq, chunk=dec_seq,
                                        hb=ts["hgrn_heads"], out_dtype=F32)
            hgrn_p.append(st_p)
            hgrn_s.append(st_s)
            w_out = c_w_out
        h_p, xf_p, h_s, xf_s = out_proj(y_p, y_s, w_out, j, h_p, h_s, norm_ffn[i], tm=tp["out_tm"])
        gain_next = norm_mix[i + 1] if i + 1 < depth else None
        h_p, h_s, xn_p, xn_s = ffn(xf_p, h_p, xf_s, h_s, ffn_w_gu, ffn_w_down, i, gain_next,
                                   tm=tp["ffn_tm"], th=math.gcd(ffn_w_down.shape[1], tp["ffn_th"]))
    return (h_p.reshape(batch, seq, d_model), h_s.reshape(dec_b, dec_seq, d_model),
            jnp.stack(k_p_rows), jnp.stack(v_p_rows), jnp.stack(k_s_rows), jnp.stack(v_s_rows),
            jnp.stack(hgrn_p), jnp.stack(hgrn_s), jnp.stack(chunk_v_s))
```

```python
import functools
import math

import jax
import jax.numpy as jnp
import numpy as np
from jax import lax
from jax.experimental import pallas as pl
from jax.experimental.pallas import tpu as pltpu

F32 = jnp.float32
BF16 = jnp.bfloat16
EPS = 1e-6

LANES = 128
SUBLANES = 8
VMEM_LIMIT_BYTES = 56 << 20
FFN_VMEM_LIMIT_BYTES = 63 << 20
FFN_RESIDUAL_SLICES = 8

A_CHUNK = 128
A_GROUPS = 8
B_HEAD_DIM = 64
B_V_DIM = 2 * B_HEAD_DIM
C_HEAD_DIM = 128
PAGE_SIZE = 128
HGRN_SUB = SUBLANES
DECODE_PAGES_PER_STEP = 8
ATTN_FIXED_SHIFT_MAX_BOUND = 30.0


def _cparams(n_axes):
    return pltpu.CompilerParams(
        dimension_semantics=("arbitrary",) * n_axes,
        vmem_limit_bytes=VMEM_LIMIT_BYTES,
    )


def _rmsnorm_f32(x, gain):
    return x * lax.rsqrt(jnp.mean(x * x, axis=-1, keepdims=True) + EPS) * gain


def _sigmoid(x):
    return 1.0 / (1.0 + jnp.exp(-x))


def _norm_rows_body(x_ref, g_ref, o_ref):
    o_ref[...] = _rmsnorm_f32(x_ref[...], g_ref[...]).astype(o_ref.dtype)


def norm_rows(x, gain, *, tm):
    m, d = x.shape
    return pl.pallas_call(
        _norm_rows_body,
        grid=(m // tm,),
        in_specs=[pl.BlockSpec((tm, d), lambda i: (i, 0)),
                  pl.BlockSpec((1, d), lambda i: (0, 0))],
        out_specs=pl.BlockSpec((tm, d), lambda i: (i, 0)),
        out_shape=jax.ShapeDtypeStruct((m, d), BF16),
        compiler_params=_cparams(1),
        name="norm_rows",
    )(x, gain.reshape(1, d))


def _seg_matmul_body(*refs, ns, nv, nc, no, epilogue):
    x_ref, xs_ref = refs[0:2]
    w_refs = refs[2:2 + ns]
    vec_refs = refs[2 + ns:2 + ns + nv]
    const_refs = refs[2 + ns + nv:2 + ns + nv + nc]
    out_refs = refs[2 + ns + nv + nc:2 + ns + nv + nc + no]
    sample_out_refs = refs[2 + ns + nv + nc + no:-1]
    wb_ref = refs[-1]
    first_row_tile = pl.program_id(1) == 0

    @pl.when(first_row_tile)
    def _():
        for s in range(ns):
            wb_ref[s] = w_refs[s][...].astype(BF16)

    def project(rows_ref, dst_refs):
        x = rows_ref[...]
        accs = [jnp.dot(x, wb_ref[s], preferred_element_type=F32) for s in range(ns)]
        outs = epilogue(accs, [r[...] for r in vec_refs], [r[...] for r in const_refs])
        for r, o in zip(dst_refs, outs):
            r[...] = o.astype(r.dtype)

    project(x_ref, out_refs)

    @pl.when(first_row_tile)
    def _():
        project(xs_ref, sample_out_refs)


def seg_matmul(x, xs, w, layer, seg_starts, seg_width, epilogue, out_dtypes, vecs=(), consts=(),
               *, tm, tn, name):
    m, k = x.shape
    ms = xs.shape[0]
    ns = len(seg_starts)
    no = len(out_dtypes)
    in_specs = [pl.BlockSpec((tm, k), lambda j, i: (i, 0)),
                pl.BlockSpec((ms, k), lambda j, i: (0, 0))]
    for st in seg_starts:
        in_specs.append(pl.BlockSpec((None, k, tn), lambda j, i, off=st // tn: (layer, 0, off + j)))
    for _ in vecs:
        in_specs.append(pl.BlockSpec((1, tn), lambda j, i: (0, j)))
    for c in consts:
        in_specs.append(pl.BlockSpec(c.shape, lambda j, i, nd=c.ndim: (0,) * nd))
    res = pl.pallas_call(
        functools.partial(_seg_matmul_body, ns=ns, nv=len(vecs), nc=len(consts), no=no,
                          epilogue=epilogue),
        grid=(seg_width // tn, m // tm),
        in_specs=in_specs,
        out_specs=([pl.BlockSpec((tm, tn), lambda j, i: (i, j)) for _ in out_dtypes]
                   + [pl.BlockSpec((ms, tn), lambda j, i: (0, j)) for _ in out_dtypes]),
        out_shape=([jax.ShapeDtypeStruct((m, seg_width), dt) for dt in out_dtypes]
                   + [jax.ShapeDtypeStruct((ms, seg_width), dt) for dt in out_dtypes]),
        scratch_shapes=[pltpu.VMEM((ns, k, tn), BF16)],
        compiler_params=_cparams(2),
        name=name,
    )(x, xs, *([w] * ns), *vecs, *consts)
    return res[:no], res[no:]


def _gelu_exact_f32(a):
    z = a * (2.0 ** -0.5)
    az = jnp.abs(z)
    t = 1.0 / (1.0 + 0.3275911 * az)
    poly = t * (0.254829592 + t * (-0.284496736 + t * (1.421413741
                                                       + t * (-1.453152027 + t * 1.061405429))))
    erfc_abs = poly * jnp.exp(-az * az)
    return 0.5 * a * jnp.where(z >= 0, 2.0 - erfc_abs, erfc_abs)


def _gelu_epilogue(accs, vecs, consts):
    (a,) = accs
    return [_gelu_exact_f32(a)]


def _headnorm_epilogue(accs, vecs, consts):
    aq, ak, av = accs
    gq, gk = vecs
    (group_ones,) = consts

    def head_norm(a, g):
        ms = jnp.dot((a * a).astype(BF16), group_ones, preferred_element_type=F32) * (1.0 / B_HEAD_DIM)
        return a * lax.rsqrt(ms + EPS) * g

    return [head_norm(aq, gq), head_norm(ak, gk), av]


def _hgrn_gate_epilogue(accs, vecs, consts):
    aq, af, av, ag = accs
    (lb,) = vecs
    q = aq * _sigmoid(aq)
    sig = _sigmoid(af)
    f = lb + (1.0 - lb) * sig
    k = (1.0 - lb) * (1.0 - sig)
    return [q, k, jnp.log(f), av, ag]


def _out_proj_body(y_ref, ys_ref, w_ref, h_ref, hs_ref, g_ref, ho_ref, xo_ref, hso_ref, xso_ref,
                   wb_ref, *, cast_rows):
    first_row_tile = pl.program_id(0) == 0

    @pl.when(first_row_tile)
    def _():
        def cast(r, carry):
            sl = pl.ds(pl.multiple_of(r * cast_rows, cast_rows), cast_rows)
            wb_ref[sl, :] = w_ref[sl, :].astype(BF16)
            return carry
        lax.fori_loop(0, w_ref.shape[0] // cast_rows, cast, 0)

    def project(rows_ref, res_ref, h_out_ref, x_out_ref):
        hn = res_ref[...] + jnp.dot(rows_ref[...].astype(BF16), wb_ref[...],
                                    preferred_element_type=F32)
        h_out_ref[...] = hn
        x_out_ref[...] = _rmsnorm_f32(hn, g_ref[...]).astype(x_out_ref.dtype)

    project(y_ref, h_ref, ho_ref, xo_ref)

    @pl.when(first_row_tile)
    def _():
        project(ys_ref, hs_ref, hso_ref, xso_ref)


def out_proj(y, ys, w, layer, h, hs, gain_next, *, tm):
    m, k = y.shape
    ms = ys.shape[0]
    n = w.shape[2]
    row = lambda cols: pl.BlockSpec((tm, cols), lambda i: (i, 0))
    sample = lambda cols: pl.BlockSpec((ms, cols), lambda i: (0, 0))
    return pl.pallas_call(
        functools.partial(_out_proj_body, cast_rows=256),
        grid=(m // tm,),
        in_specs=[row(k), sample(k),
                  pl.BlockSpec((None, k, n), lambda i: (layer, 0, 0), pipeline_mode=pl.Buffered(1)),
                  row(n), sample(n),
                  pl.BlockSpec((1, n), lambda i: (0, 0))],
        out_specs=[row(n), row(n), sample(n), sample(n)],
        out_shape=[jax.ShapeDtypeStruct((m, n), F32), jax.ShapeDtypeStruct((m, n), BF16),
                   jax.ShapeDtypeStruct((ms, n), F32), jax.ShapeDtypeStruct((ms, n), BF16)],
        scratch_shapes=[pltpu.VMEM((k, n), BF16)],
        compiler_params=_cparams(1),
        name="out_proj",
    )(y, ys, w, h, hs, gain_next.reshape(1, n))


def _ffn_body(x_ref, h_ref, xs_ref, hs_ref, wg_ref, wu_ref, wd_ref, g_ref, *out_refs, n_t, with_norm):
    if with_norm:
        ho_ref, hso_ref, xo_ref, xso_ref = out_refs
    else:
        ho_ref, hso_ref = out_refs
    i = pl.program_id(0)
    t = pl.program_id(1)

    @pl.when(t == 0)
    def _():
        ho_ref[...] = jnp.zeros(ho_ref.shape, F32)

    slice_rows = h_ref.shape[0]
    for p in range(FFN_RESIDUAL_SLICES):
        @pl.when(t == p)
        def _(p=p):
            ho_ref[p * slice_rows:(p + 1) * slice_rows, :] += h_ref[...]

    def swiglu(x):
        gate = jnp.dot(x, wg_ref[...].astype(BF16), preferred_element_type=F32)
        up = jnp.dot(x, wu_ref[...].astype(BF16), preferred_element_type=F32)
        act = (gate * _sigmoid(gate) * up).astype(BF16)
        return jnp.dot(act, wd_ref[...].astype(BF16), preferred_element_type=F32)

    ho_ref[...] += swiglu(x_ref[...])

    @pl.when((i == 0) & (t == 0))
    def _():
        hso_ref[...] = hs_ref[...]

    @pl.when(i == 0)
    def _():
        hso_ref[...] += swiglu(xs_ref[...])

    if with_norm:
        @pl.when(t == n_t - 1)
        def _():
            xo_ref[...] = _rmsnorm_f32(ho_ref[...], g_ref[...]).astype(BF16)

        @pl.when((i == 0) & (t == n_t - 1))
        def _():
            xso_ref[...] = _rmsnorm_f32(hso_ref[...], g_ref[...]).astype(BF16)


def ffn(x, h, xs, hs, w_gu, w_down, layer, gain_next, *, tm, th):
    m, d = x.shape
    ms = xs.shape[0]
    hidden = w_down.shape[1]
    n_t = hidden // th
    with_norm = gain_next is not None
    gain = gain_next if with_norm else jnp.ones((d,), F32)
    row_out = pl.BlockSpec((tm, d), lambda i, t: (i, 0))
    sample_block = pl.BlockSpec((ms, d), lambda i, t: (0, 0))
    out_specs = [row_out, sample_block]
    out_shape = [jax.ShapeDtypeStruct((m, d), F32), jax.ShapeDtypeStruct((ms, d), F32)]
    if with_norm:
        out_specs += [row_out, sample_block]
        out_shape += [jax.ShapeDtypeStruct((m, d), BF16), jax.ShapeDtypeStruct((ms, d), BF16)]
    assert n_t >= FFN_RESIDUAL_SLICES and tm % (FFN_RESIDUAL_SLICES * SUBLANES) == 0
    row_block = pl.BlockSpec((tm, d), lambda i, t: (i, 0), pipeline_mode=pl.Buffered(1))
    last_slice = FFN_RESIDUAL_SLICES - 1
    residual_slice = pl.BlockSpec(
        (tm // FFN_RESIDUAL_SLICES, d),
        lambda i, t: (i * FFN_RESIDUAL_SLICES + jnp.minimum(t, last_slice), 0))
    res = pl.pallas_call(
        functools.partial(_ffn_body, n_t=n_t, with_norm=with_norm),
        grid=(m // tm, n_t),
        in_specs=[row_block, residual_slice, sample_block, sample_block,
                  pl.BlockSpec((None, d, th), lambda i, t: (layer, 0, t)),
                  pl.BlockSpec((None, d, th), lambda i, t: (layer, 0, n_t + t)),
                  pl.BlockSpec((None, th, d), lambda i, t: (layer, t, 0)),
                  pl.BlockSpec((1, d), lambda i, t: (0, 0))],
        out_specs=out_specs,
        out_shape=out_shape,
        compiler_params=pltpu.CompilerParams(dimension_semantics=("arbitrary",) * 2,
                                             vmem_limit_bytes=FFN_VMEM_LIMIT_BYTES),
        name="ffn",
    )(x, h, xs, hs, w_gu, w_gu, w_down, gain.reshape(1, d))
    return tuple(res) if with_norm else (res[0], res[1], None, None)


def _spatial_body(u_ref, v_ref, wm_ref, bs_ref, vg_ref, p_ref, *maybe_vn_ref, chunk, groups):
    v = v_ref[...].astype(F32)
    vn = _rmsnorm_f32(v, vg_ref[...])
    if maybe_vn_ref:
        maybe_vn_ref[0][...] = vn
    vnb = vn.astype(BF16)
    rows, width = v.shape
    gw = width // groups
    for c in range(rows // chunk):
        r0 = c * chunk
        for g in range(groups):
            c0 = g * gw
            s = jnp.dot(wm_ref[g], vnb[r0:r0 + chunk, c0:c0 + gw], preferred_element_type=F32)
            s = s + bs_ref[g]
            u = u_ref[r0:r0 + chunk, c0:c0 + gw].astype(F32)
            p_ref[r0:r0 + chunk, c0:c0 + gw] = (u * s).astype(p_ref.dtype)


def spatial_mix(uv, wm, bs, v_gain, *, chunk, tm, with_vn):
    m, w2 = uv.shape
    width = w2 // 2
    groups = wm.shape[0]
    row_out = pl.BlockSpec((tm, width), lambda i: (i, 0))
    out_specs = [row_out]
    out_shape = [jax.ShapeDtypeStruct((m, width), BF16)]
    if with_vn:
        out_specs.append(row_out)
        out_shape.append(jax.ShapeDtypeStruct((m, width), F32))
    res = pl.pallas_call(
        functools.partial(_spatial_body, chunk=chunk, groups=groups),
        grid=(m // tm,),
        in_specs=[pl.BlockSpec((tm, width), lambda i: (i, 0)),
                  pl.BlockSpec((tm, width), lambda i: (i, 1)),
                  pl.BlockSpec(wm.shape, lambda i: (0, 0, 0)),
                  pl.BlockSpec(bs.shape, lambda i: (0, 0, 0)),
                  pl.BlockSpec((1, width), lambda i: (0, 0))],
        out_specs=out_specs,
        out_shape=out_shape,
        compiler_params=_cparams(1),
        name="spatial_mix",
    )(uv, uv, wm, bs, v_gain.reshape(1, width))
    return (res[0], res[1]) if with_vn else (res[0], None)


def _diff_attn_body(qt_tab, kt_tab, q_ref, k_ref, v_ref, sc_ref, sub_ref, o_ref,
                    qs_ref, m_ref, l_ref, acc_ref, *, tq, hb, out_scale, fixed_shift):
    t = pl.program_id(2)
    qi = qt_tab[t]
    ki = kt_tab[t]
    hd = 2 * B_HEAD_DIM

    @pl.when(ki == 0)
    def _():
        for h in range(hb):
            q = q_ref[:, h * hd:(h + 1) * hd].astype(F32) * (B_HEAD_DIM ** -0.5)
            lane = lax.broadcasted_iota(jnp.int32, q.shape, 1)
            qs_ref[h, 0:tq, :] = jnp.where(lane < B_HEAD_DIM, q, 0.0)
            qs_ref[h, tq:2 * tq, :] = jnp.where(lane >= B_HEAD_DIM, q, 0.0)
        if not fixed_shift:
            m_ref[...] = jnp.full(m_ref.shape, -jnp.inf, F32)
        l_ref[...] = jnp.zeros(l_ref.shape, F32)
        acc_ref[...] = jnp.zeros(acc_ref.shape, F32)

    def strip(h, r0, masked):
        rs = pl.ds(r0, tq)
        keys = k_ref[:, h * hd:(h + 1) * hd]
        vals = v_ref[:, h * B_V_DIM:(h + 1) * B_V_DIM]
        s = lax.dot_general(qs_ref[h, rs, :], keys, (((1,), (1,)), ((), ())),
                            preferred_element_type=F32)
        if masked:
            row = lax.broadcasted_iota(jnp.int32, s.shape, 0)
            col = lax.broadcasted_iota(jnp.int32, s.shape, 1)
            s = jnp.where(col <= row, s, -jnp.inf)
        if fixed_shift:
            p = jnp.exp(s - sc_ref[1])
            part = p[:, 0:LANES]
            for c0 in range(LANES, p.shape[1], LANES):
                part = part + p[:, c0:c0 + LANES]
            l_ref[h, rs, :] += part
            acc_ref[h, rs, :] += jnp.dot(p, vals, preferred_element_type=F32)
        else:
            m_prev = m_ref[h, rs, :]
            m_new = jnp.maximum(m_prev, jnp.max(s, axis=1, keepdims=True))
            alpha = jnp.exp(m_prev - m_new)
            p = jnp.exp(s - m_new[:, 0:1])
            l_ref[h, rs, :] = alpha * l_ref[h, rs, :] + jnp.sum(p, axis=1, keepdims=True)
            acc_ref[h, rs, :] = alpha * acc_ref[h, rs, :] + jnp.dot(p, vals, preferred_element_type=F32)
            m_ref[h, rs, :] = m_new

    @pl.when(ki < qi)
    def _():
        for h in range(hb):
            strip(h, 0, False)
            strip(h, tq, False)

    @pl.when(ki == qi)
    def _():
        for h in range(hb):
            strip(h, 0, True)
            strip(h, tq, True)
            l = l_ref[h]
            if fixed_shift:
                l = jnp.sum(l, axis=1, keepdims=True)
            o = acc_ref[h] / l
            d = o[0:tq] - sc_ref[0] * o[tq:2 * tq]
            o_ref[:, h * B_V_DIM:(h + 1) * B_V_DIM] = (
                _rmsnorm_f32(d, sub_ref[...]) * out_scale).astype(o_ref.dtype)


def diff_attn_prompt(q, k, v, lam, score_bound, subln, lam_init, *, batch, seq, heads, tq, hb,
                     fixed_shift):
    nq = seq // tq
    tri = [(qi, ki) for qi in range(nq) for ki in range(qi + 1)]
    qt_tab = jnp.asarray([a for a, _ in tri], jnp.int32)
    kt_tab = jnp.asarray([b for _, b in tri], jnp.int32)
    hd = 2 * B_HEAD_DIM
    grid_spec = pltpu.PrefetchScalarGridSpec(
        num_scalar_prefetch=2,
        grid=(batch, heads // hb, len(tri)),
        in_specs=[pl.BlockSpec((tq, hb * hd), lambda b, h, t, qt, kt: (b * nq + qt[t], h)),
                  pl.BlockSpec((tq, hb * hd), lambda b, h, t, qt, kt: (b * nq + kt[t], h)),
                  pl.BlockSpec((tq, hb * B_V_DIM), lambda b, h, t, qt, kt: (b * nq + kt[t], h)),
                  pl.BlockSpec(memory_space=pltpu.SMEM),
                  pl.BlockSpec((1, B_V_DIM), lambda b, h, t, qt, kt: (0, 0))],
        out_specs=pl.BlockSpec((tq, hb * B_V_DIM), lambda b, h, t, qt, kt: (b * nq + qt[t], h)),
        scratch_shapes=[pltpu.VMEM((hb, 2 * tq, hd), F32),
                        pltpu.VMEM((hb, 2 * tq, LANES), F32),
                        pltpu.VMEM((hb, 2 * tq, LANES), F32),
                        pltpu.VMEM((hb, 2 * tq, B_V_DIM), F32)],
    )
    return pl.pallas_call(
        functools.partial(_diff_attn_body, tq=tq, hb=hb, out_scale=1.0 - lam_init,
                          fixed_shift=fixed_shift),
        grid_spec=grid_spec,
        out_shape=jax.ShapeDtypeStruct((batch * seq, heads * B_V_DIM), BF16),
        compiler_params=_cparams(3),
        name="diff_attn_prompt_fixed_shift" if fixed_shift else "diff_attn_prompt",
    )(qt_tab, kt_tab, q, k, v, jnp.stack([lam, score_bound]).astype(F32),
      subln.reshape(1, B_V_DIM))


def _decode_attn_body(pt_ref, *refs, pages, heads, n_q, n_groups, out_scale):
    k_refs = refs[:pages]
    v_refs = refs[pages:2 * pages]
    kn_ref, vn_ref, qt_ref, lam_ref, sub_ref, o_ref, m_ref, l_ref, acc_ref = refs[2 * pages:]
    g = pl.program_id(1)
    hg = heads // SUBLANES
    hc = 2 * n_q
    cols = SUBLANES * hc

    @pl.when(g == 0)
    def _():
        m_ref[...] = jnp.full(m_ref.shape, -jnp.inf, F32)
        l_ref[...] = jnp.zeros(l_ref.shape, F32)
        acc_ref[...] = jnp.zeros(acc_ref.shape, F32)

    sub = lax.broadcasted_iota(jnp.int32, (SUBLANES, cols), 0)
    lane = lax.broadcasted_iota(jnp.int32, (SUBLANES, cols), 1)
    own = sub == lane // hc

    def to_column(x8):
        r = jnp.sum(jnp.where(own, x8, 0.0), axis=0, keepdims=True)
        return jnp.broadcast_to(r, (LANES, cols)).T

    def group_rows(ref, j, n_pos):
        x = ref[0:n_pos * heads, :].reshape(n_pos, hg, SUBLANES, 2 * B_HEAD_DIM)
        return x[:, j].reshape(n_pos * SUBLANES, 2 * B_HEAD_DIM)

    def process(page_k_refs, page_v_refs, n_pos, new_tokens):
        for j in range(hg):
            scores = []
            for k_ref in page_k_refs:
                s = jnp.dot(group_rows(k_ref, j, n_pos), qt_ref[j], preferred_element_type=F32)
                s = s.reshape(n_pos, SUBLANES, cols)
                valid = own[None]
                if new_tokens:
                    pos = lax.broadcasted_iota(jnp.int32, s.shape, 0)
                    qry = lax.broadcasted_iota(jnp.int32, s.shape, 2) % n_q
                    valid = valid & (pos <= qry)
                scores.append(jnp.where(valid, s, -jnp.inf))
            m_prev = m_ref[j]
            m_new = m_prev
            for s in scores:
                m_new = jnp.maximum(m_new, jnp.max(s, axis=0))
            m_safe = jnp.where(own, m_new, 0.0)
            alpha = jnp.exp(m_prev - m_safe)
            l_new = alpha * l_ref[j]
            pv = jnp.zeros((cols, B_V_DIM), F32)
            for s, v_ref in zip(scores, page_v_refs):
                p = jnp.exp(s - m_safe[None])
                l_new = l_new + jnp.sum(p, axis=0)
                pv = pv + lax.dot_general(p.reshape(n_pos * SUBLANES, cols), group_rows(v_ref, j, n_pos),
                                          (((0,), (0,)), ((), ())), preferred_element_type=F32)
            acc_ref[j] = acc_ref[j] * to_column(alpha) + pv
            l_ref[j] = l_new
            m_ref[j] = m_new

    process(k_refs, v_refs, PAGE_SIZE, False)

    @pl.when(g == n_groups - 1)
    def _():
        process([kn_ref], [vn_ref], n_q, True)
        for j in range(hg):
            o = acc_ref[j] / to_column(l_ref[j])
            o = o.reshape(SUBLANES, 2, n_q, B_V_DIM)
            d = o[:, 0] - lam_ref[0] * o[:, 1]
            d = d * lax.rsqrt(jnp.mean(d * d, axis=-1, keepdims=True) + EPS) * sub_ref[...]
            o_ref[j * SUBLANES:(j + 1) * SUBLANES] = (d * out_scale).astype(o_ref.dtype)


def diff_attn_decode(qt, cache_k, cache_v, page_table, k_new, v_new, lam, subln, lam_init,
                     *, layer, heads, n_q, pages):
    dec_b, n_pages = page_table.shape
    hd = 2 * B_HEAD_DIM
    rows = PAGE_SIZE * heads
    hg = heads // SUBLANES
    cols = SUBLANES * 2 * n_q
    n_groups = n_pages // pages
    page_spec = lambda p_i: pl.BlockSpec(
        (None, None, rows, hd),
        lambda b, g, pt, p_i=p_i: (layer, pt[b, g * pages + p_i], 0, 0))
    grid_spec = pltpu.PrefetchScalarGridSpec(
        num_scalar_prefetch=1,
        grid=(dec_b, n_groups),
        in_specs=([page_spec(p_i) for p_i in range(pages)] * 2
                  + [pl.BlockSpec((None, n_q * heads, hd), lambda b, g, pt: (b, 0, 0)),
                     pl.BlockSpec((None, n_q * heads, hd), lambda b, g, pt: (b, 0, 0)),
                     pl.BlockSpec((None, hg, hd, cols), lambda b, g, pt: (b, 0, 0, 0)),
                     pl.BlockSpec(memory_space=pltpu.SMEM),
                     pl.BlockSpec((1, B_V_DIM), lambda b, g, pt: (0, 0))]),
        out_specs=pl.BlockSpec((None, heads, n_q, B_V_DIM), lambda b, g, pt: (b, 0, 0, 0)),
        scratch_shapes=[pltpu.VMEM((hg, SUBLANES, cols), F32),
                        pltpu.VMEM((hg, SUBLANES, cols), F32),
                        pltpu.VMEM((hg, cols, B_V_DIM), F32)],
    )
    return pl.pallas_call(
        functools.partial(_decode_attn_body, pages=pages, heads=heads, n_q=n_q,
                          n_groups=n_groups, out_scale=1.0 - lam_init),
        grid_spec=grid_spec,
        out_shape=jax.ShapeDtypeStruct((dec_b, heads, n_q, B_V_DIM), F32),
        compiler_params=_cparams(2),
        name="diff_attn_decode",
    )(page_table, *([cache_k] * pages), *([cache_v] * pages), k_new, v_new, qt,
      lam.reshape(1), subln.reshape(1, B_V_DIM))


def _cumsum_rows(x):
    c = x.shape[0]
    sub = lax.broadcasted_iota(jnp.int32, x.shape, 0) % HGRN_SUB
    d = 1
    while d < HGRN_SUB:
        x = x + jnp.where(sub >= d, pltpu.roll(x, d, axis=0), 0.0)
        d *= 2
    blocks = []
    carry = None
    for j in range(c // HGRN_SUB):
        blk = x[j * HGRN_SUB:(j + 1) * HGRN_SUB]
        if carry is not None:
            blk = blk + carry
        blocks.append(blk)
        carry = blk[HGRN_SUB - 1:HGRN_SUB]
    return jnp.concatenate(blocks, axis=0) if len(blocks) > 1 else blocks[0]


def _hgrn_chunk(q, k, lf, v, st, row_scr):
    c = q.shape[0]
    nb = c // HGRN_SUB
    gcum = _cumsum_rows(lf)
    row_scr[0] = gcum
    row_scr[1] = k
    row_scr[2] = v
    o = lax.dot_general(q * jnp.exp(gcum), st, (((1,), (1,)), ((), ())), preferred_element_type=F32)

    if nb > 1:
        row = lax.broadcasted_iota(jnp.int32, (c, C_HEAD_DIM), 0)
        t_idx = lax.broadcasted_iota(jnp.int32, (c, c), 0)
        s_idx = lax.broadcasted_iota(jnp.int32, (c, c), 1)
        a_off = None
        size = 2 * HGRN_SUB
        while size <= c:
            half = size // 2
            if size < c:
                g_mid = jnp.concatenate(
                    [jnp.broadcast_to(gcum[b0 + half - 1:b0 + half], (size, C_HEAD_DIM))
                     for b0 in range(0, c, size)], axis=0)
            else:
                g_mid = gcum[half - 1:half]
            upper = (row % size) >= half
            qd = q * jnp.exp(jnp.where(upper, gcum - g_mid, -jnp.inf))
            kd = k * jnp.exp(jnp.where(upper, -jnp.inf, g_mid - gcum))
            a = lax.dot_general(qd, kd, (((1,), (1,)), ((), ())), preferred_element_type=F32)
            if size < c:
                a = jnp.where((t_idx // size) == (s_idx // size), a, 0.0)
            a_off = a if a_off is None else a_off + a
            size *= 2
        o = o + jnp.dot(a_off, v, preferred_element_type=F32)

    sub_row = lax.broadcasted_iota(jnp.int32, (HGRN_SUB, C_HEAD_DIM), 0)
    o_blocks = []
    for i in range(nb):
        r0 = i * HGRN_SUB
        gi = gcum[r0:r0 + HGRN_SUB]
        qi = q[r0:r0 + HGRN_SUB]
        oi = o[r0:r0 + HGRN_SUB]
        for s in range(HGRN_SUB):
            r = r0 + s
            dec = jnp.exp(jnp.where(sub_row >= s, gi - row_scr[0, r:r + 1, :], -jnp.inf))
            a_col = jnp.sum(qi * row_scr[1, r:r + 1, :] * dec, axis=-1, keepdims=True)
            oi = oi + a_col * row_scr[2, r:r + 1, :]
        o_blocks.append(oi)
    o = jnp.concatenate(o_blocks, axis=0) if nb > 1 else o_blocks[0]
    g_last = gcum[c - 1:c]
    kd = k * jnp.exp(g_last - gcum)
    if c < C_HEAD_DIM:
        pad = jnp.zeros((C_HEAD_DIM - c, C_HEAD_DIM), F32)
        kd = jnp.concatenate([kd, pad], axis=0)
        v = jnp.concatenate([v, pad], axis=0)
    st_new = st * jnp.exp(g_last) + jnp.dot(v.T, kd, preferred_element_type=F32)
    return o, st_new


def _hgrn_body(*refs, chunk, n_chunks, n_r, hb, with_state):
    if with_state:
        q_ref, k_ref, lf_ref, v_ref, g_ref, gg_ref, s0_ref, o_ref, so_ref, st_ref, gs_ref = refs
    else:
        q_ref, k_ref, lf_ref, v_ref, g_ref, gg_ref, o_ref, so_ref, st_ref, gs_ref = refs
    r = pl.program_id(2)
    hd = C_HEAD_DIM

    @pl.when(r == 0)
    def _():
        for h in range(hb):
            if with_state:
                st_ref[h] = s0_ref[h].astype(F32).T
            else:
                st_ref[h] = jnp.zeros((hd, hd), F32)

    def step(ci, carry):
        base = pl.multiple_of(ci * chunk, chunk)
        sl = pl.ds(base, chunk)
        for h in range(hb):
            cs = slice(h * hd, (h + 1) * hd)
            o, st_new = _hgrn_chunk(q_ref[sl, cs], k_ref[sl, cs], lf_ref[sl, cs], v_ref[sl, cs],
                                    st_ref[h], gs_ref.at[h])
            st_ref[h] = st_new
            gate = g_ref[sl, cs]
            o = _rmsnorm_f32(o, gg_ref[...]) * (gate * _sigmoid(gate))
            o_ref[sl, cs] = o.astype(o_ref.dtype)
        return carry

    lax.fori_loop(0, n_chunks, step, 0)

    @pl.when(r == n_r - 1)
    def _():
        for h in range(hb):
            so_ref[h] = st_ref[h].T.astype(so_ref.dtype)


def hgrn_recurrence(q, k, lf, v, g, g_gain, state0, *, batch, seq, heads, rows, chunk, hb, out_dtype):
    n_r = seq // rows
    hd = C_HEAD_DIM
    with_state = state0 is not None
    row_spec = pl.BlockSpec((rows, hb * hd), lambda b, h, r: (b * n_r + r, h))
    state_spec = pl.BlockSpec((None, hb, hd, hd), lambda b, h, r: (b, h, 0, 0))
    in_specs = [row_spec] * 5 + [pl.BlockSpec((1, hd), lambda b, h, r: (0, 0))]
    args = [q, k, lf, v, g, g_gain.reshape(1, hd)]
    if with_state:
        in_specs.append(state_spec)
        args.append(state0)
    return pl.pallas_call(
        functools.partial(_hgrn_body, chunk=chunk, n_chunks=rows // chunk, n_r=n_r, hb=hb,
                          with_state=with_state),
        grid=(batch, heads // hb, n_r),
        in_specs=in_specs,
        out_specs=[row_spec, state_spec],
        out_shape=[jax.ShapeDtypeStruct((batch * seq, heads * hd), out_dtype),
                   jax.ShapeDtypeStruct((batch, heads, hd, hd), F32)],
        scratch_shapes=[pltpu.VMEM((hb, hd, hd), F32), pltpu.VMEM((hb, 3, chunk, hd), F32)],
        compiler_params=_cparams(3),
        name="hgrn_recurrence",
    )(*args)


def _tiles(m):
    big = m >= 1024
    return dict(
        norm_tm=512 if big else m,
        proj_tm=1024 if big else m,
        proj_tn=256,
        a_proj_tn=1024,
        out_tm=512 if big else m,
        ffn_tm=1024 if big else m,
        ffn_th=512,
        spatial_tm=512 if big else m,
        attn_tq=512,
        hgrn_rows=1024,
        attn_heads=8,
        attn_heads_fixed_shift=16,
        hgrn_heads=2 if big else 8,
    )


def _mixer_a_core(uv, v_gain, w_s, b_s, *, chunk_len, n_seq, tiles, with_vn):
    causal = jnp.tril(jnp.ones((A_CHUNK, A_CHUNK), bool))
    w_masked = jnp.where(causal[None], w_s, 0.0)
    if chunk_len == A_CHUNK:
        wm, bs, chunk = w_masked, b_s, A_CHUNK
    else:
        eye = jnp.eye(n_seq, dtype=w_s.dtype)
        small = w_masked[:, :chunk_len, :chunk_len]
        wm = jnp.einsum("ab,gts->gatbs", eye, small).reshape(
            A_GROUPS, n_seq * chunk_len, n_seq * chunk_len)
        bs = jnp.tile(b_s[:, :chunk_len], (1, n_seq))
        chunk = n_seq * chunk_len
    p, vn = spatial_mix(uv, wm.astype(BF16), bs[:, :, None], v_gain, chunk=chunk,
                        tm=max(tiles["spatial_tm"], chunk) if chunk_len == A_CHUNK else chunk,
                        with_vn=with_vn)
    return p, vn


def _mixer_a_proj(xn, xns, w_in, layer, *, tiles):
    width2 = w_in.shape[2]
    (uv,), (uv_s,) = seg_matmul(xn, xns, w_in, layer, (0,), width2, _gelu_epilogue, (BF16,),
                                tm=tiles["proj_tm"], tn=tiles["a_proj_tn"], name="a_in_proj")
    return uv, uv_s


def _mixer_b_proj(xn, xns, w_in, layer, q_gain, k_gain, *, heads, tiles):
    width = heads * 2 * B_HEAD_DIM
    tn = tiles["proj_tn"]
    lane_group = np.arange(tn) // B_HEAD_DIM
    group_ones = jnp.asarray(lane_group[:, None] == lane_group[None, :], BF16)
    reps = width // B_HEAD_DIM
    gq = jnp.tile(q_gain.astype(F32), reps).reshape(1, width)
    gk = jnp.tile(k_gain.astype(F32), reps).reshape(1, width)
    return seg_matmul(xn, xns, w_in, layer, (0, width, 2 * width), width, _headnorm_epilogue,
                      (F32, F32, F32), vecs=(gq, gk), consts=(group_ones,),
                      tm=tiles["proj_tm"], tn=tn, name="b_in_proj")


def _mixer_c_proj(xn, xns, w_in, layer, lower_bound, *, tiles):
    width = w_in.shape[2] // 4
    return seg_matmul(xn, xns, w_in, layer, (0, width, 2 * width, 3 * width), width, _hgrn_gate_epilogue,
                      (F32,) * 5, vecs=(lower_bound.reshape(1, width),),
                      tm=tiles["proj_tm"], tn=tiles["proj_tn"], name="c_in_proj")


def kernel(x_prompt, x_sample, cache_k, cache_v, page_table, state_hgrn, norm_mix, norm_ffn, ffn_w_gu, ffn_w_down, a_w_in, a_v_norm, a_w_s, a_b_s, a_w_out, b_w_in, b_q_norm, b_k_norm, b_lambda_q1, b_lambda_k1, b_lambda_q2, b_lambda_k2, b_subln, b_w_out, c_w_in, c_g_norm, c_lower_bounds, c_w_out):
    batch, seq, d_model = x_prompt.shape
    dec_b, dec_seq, _ = x_sample.shape
    depth = norm_mix.shape[0]
    b_heads = d_model // (2 * B_HEAD_DIM)
    c_heads = d_model // C_HEAD_DIM
    mp, ms = batch * seq, dec_b * dec_seq
    tp, ts = _tiles(mp), _tiles(ms)

    probs = jax.nn.softmax(c_lower_bounds.astype(F32), axis=0)
    lower_bound = jnp.cumsum(probs, axis=0) - probs[0]

    h_p = x_prompt.reshape(mp, d_model)
    h_s = x_sample.reshape(ms, d_model)
    xn_p = norm_rows(h_p, norm_mix[0], tm=tp["norm_tm"])
    xn_s = norm_rows(h_s, norm_mix[0], tm=ts["norm_tm"])

    n_phys = cache_k.shape[1]
    cache_k2 = cache_k.reshape(cache_k.shape[0], n_phys, PAGE_SIZE * b_heads, 2 * B_HEAD_DIM)
    cache_v2 = cache_v.reshape(cache_v.shape[0], n_phys, PAGE_SIZE * b_heads, B_V_DIM)

    k_p_rows, v_p_rows, k_s_rows, v_s_rows = [], [], [], []
    hgrn_p, hgrn_s, chunk_v_s = [], [], []
    for i in range(depth):
        kind, j = i % 3, i // 3
        if kind == 0:
            uv_p, uv_s = _mixer_a_proj(xn_p, xn_s, a_w_in, j, tiles=tp)
            y_p, _ = _mixer_a_core(uv_p, a_v_norm[j], a_w_s[j], a_b_s[j],
                                   chunk_len=A_CHUNK, n_seq=batch, tiles=tp, with_vn=False)
            y_s, vn_s = _mixer_a_core(uv_s, a_v_norm[j], a_w_s[j], a_b_s[j],
                                      chunk_len=dec_seq, n_seq=dec_b, tiles=ts, with_vn=True)
            chunk_v_s.append(vn_s.reshape(dec_b, dec_seq, -1))
            w_out = a_w_out
        elif kind == 1:
            lam_init = 0.8 - 0.6 * math.exp(-0.3 * i)
            lam = (jnp.exp(jnp.sum(b_lambda_q1[j].astype(F32) * b_lambda_k1[j].astype(F32)))
                   - jnp.exp(jnp.sum(b_lambda_q2[j].astype(F32) * b_lambda_k2[j].astype(F32)))
                   + lam_init)
            (q_p, k_p, v_p), (q_s, k_s, v_s) = _mixer_b_proj(
                xn_p, xn_s, b_w_in, j, b_q_norm[j], b_k_norm[j], heads=b_heads, tiles=tp)
            score_bound = (B_HEAD_DIM ** 0.5) * jnp.max(jnp.abs(b_q_norm[j].astype(F32))) * jnp.max(
                jnp.abs(b_k_norm[j].astype(F32)))
            attn = functools.partial(diff_attn_prompt, batch=batch, seq=seq, heads=b_heads,
                                     tq=tp["attn_tq"])
            y_p = lax.cond(
                score_bound <= ATTN_FIXED_SHIFT_MAX_BOUND,
                lambda *a: attn(*a, lam_init, hb=min(b_heads, tp["attn_heads_fixed_shift"]),
                                fixed_shift=True),
                lambda *a: attn(*a, lam_init, hb=min(b_heads, tp["attn_heads"]), fixed_shift=False),
                q_p, k_p, v_p, lam, score_bound, b_subln[j])
            k_p_rows.append(k_p.reshape(batch, seq, b_heads, 2 * B_HEAD_DIM))
            v_p_rows.append(v_p.reshape(batch, seq, b_heads, B_V_DIM))

            q5 = (q_s *(B_HEAD_DIM ** -0.5)).reshape(dec_b, dec_seq, b_heads, 2, B_HEAD_DIM)
            qt = jnp.einsum("bthcd,ce->bhcted", q5, jnp.eye(2, dtype=F32))
            qt = qt.reshape(dec_b, b_heads // SUBLANES, SUBLANES * 2 * dec_seq, 2 * B_HEAD_DIM)
            qt = qt.transpose(0, 1, 3, 2)
            k_new = k_s.reshape(dec_b, dec_seq * b_heads, 2 * B_HEAD_DIM)
            v_new = v_s.reshape(dec_b, dec_seq * b_heads, B_V_DIM)
            o_s = diff_attn_decode(qt, cache_k2, cache_v2, page_table, k_new, v_new, lam,
                                   b_subln[j], lam_init, layer=j, heads=b_heads, n_q=dec_seq,
                                   pages=DECODE_PAGES_PER_STEP)
            y_s = o_s.transpose(0, 2, 1, 3).reshape(ms, d_model)
            k_s_rows.append(k_s.reshape(dec_b, dec_seq, b_heads, 2 * B_HEAD_DIM))
            v_s_rows.append(v_s.reshape(dec_b, dec_seq, b_heads, B_V_DIM))
            w_out = b_w_out
        else:
            qkv_p, qkv_s = _mixer_c_proj(xn_p, xn_s, c_w_in, j, lower_bound[i], tiles=tp)
            y_p, st_p = hgrn_recurrence(*qkv_p, c_g_norm[j], None, batch=batch, seq=seq,
                                        heads=c_heads, rows=tp["hgrn_rows"], chunk=C_HEAD_DIM,
                                        hb=tp["hgrn_heads"], out_dtype=BF16)
            y_s, st_s = hgrn_recurrence(*qkv_s, c_g_norm[j], state_hgrn[j], batch=dec_b,
                                        seq=dec_seq, heads=c_heads, rows=dec_seq, chunk=dec_seq,
                                        hb=ts["hgrn_heads"], out_dtype=F32)
            hgrn_p.append(st_p)
            hgrn_s.append(st_s)
            w_out = c_w_out
        h_p, xf_p, h_s, xf_s = out_proj(y_p, y_s, w_out, j, h_p, h_s, norm_ffn[i], tm=tp["out_tm"])
        gain_next = norm_mix[i + 1] if i + 1 < depth else None
        h_p, h_s, xn_p, xn_s = ffn(xf_p, h_p, xf_s, h_s, ffn_w_gu, ffn_w_down, i, gain_next,
                                   tm=tp["ffn_tm"], th=math.gcd(ffn_w_down.shape[1], tp["ffn_th"]))
    return (h_p.reshape(batch, seq, d_model), h_s.reshape(dec_b, dec_seq, d_model),
            jnp.stack(k_p_rows), jnp.stack(v_p_rows), jnp.stack(k_s_rows), jnp.stack(v_s_rows),
            jnp.stack(hgrn_p), jnp.stack(hgrn_s), jnp.stack(chunk_v_s))
```

```python
import functools
import math

import jax
import jax.numpy as jnp
import numpy as np
from jax import lax
from jax.experimental import pallas as pl
from jax.experimental.pallas import tpu as pltpu

F32 = jnp.float32
BF16 = jnp.bfloat16
EPS = 1e-6

LANES = 128
SUBLANES = 8
VMEM_LIMIT_BYTES = 56 << 20
FFN_VMEM_LIMIT_BYTES = 63 << 20
FFN_RESIDUAL_SLICES = 8

A_CHUNK = 128
A_GROUPS = 8
B_HEAD_DIM = 64
B_V_DIM = 2 * B_HEAD_DIM
C_HEAD_DIM = 128
PAGE_SIZE = 128
HGRN_SUB = SUBLANES
DECODE_PAGES_PER_STEP = 8
ATTN_FIXED_SHIFT_MAX_BOUND = 30.0


def _cparams(n_axes):
    return pltpu.CompilerParams(
        dimension_semantics=("arbitrary",) * n_axes,
        vmem_limit_bytes=VMEM_LIMIT_BYTES,
    )


def _rmsnorm_f32(x, gain):
    return x * lax.rsqrt(jnp.mean(x * x, axis=-1, keepdims=True) + EPS) * gain


def _sigmoid(x):
    return 1.0 / (1.0 + jnp.exp(-x))


def _norm_rows_body(x_ref, g_ref, o_ref):
    o_ref[...] = _rmsnorm_f32(x_ref[...], g_ref[...]).astype(o_ref.dtype)


def norm_rows(x, gain, *, tm):
    m, d = x.shape
    return pl.pallas_call(
        _norm_rows_body,
        grid=(m // tm,),
        in_specs=[pl.BlockSpec((tm, d), lambda i: (i, 0)),
                  pl.BlockSpec((1, d), lambda i: (0, 0))],
        out_specs=pl.BlockSpec((tm, d), lambda i: (i, 0)),
        out_shape=jax.ShapeDtypeStruct((m, d), BF16),
        compiler_params=_cparams(1),
        name="norm_rows",
    )(x, gain.reshape(1, d))


def _seg_matmul_body(*refs, ns, nv, nc, no, epilogue):
    x_ref, xs_ref = refs[0:2]
    w_refs = refs[2:2 + ns]
    vec_refs = refs[2 + ns:2 + ns + nv]
    const_refs = refs[2 + ns + nv:2 + ns + nv + nc]
    out_refs = refs[2 + ns + nv + nc:2 + ns + nv + nc + no]
    sample_out_refs = refs[2 + ns + nv + nc + no:-1]
    wb_ref = refs[-1]
    first_row_tile = pl.program_id(1) == 0

    @pl.when(first_row_tile)
    def _():
        for s in range(ns):
            wb_ref[s] = w_refs[s][...].astype(BF16)

    def project(rows_ref, dst_refs):
        x = rows_ref[...]
        accs = [jnp.dot(x, wb_ref[s], preferred_element_type=F32) for s in range(ns)]
        outs = epilogue(accs, [r[...] for r in vec_refs], [r[...] for r in const_refs])
        for r, o in zip(dst_refs, outs):
            r[...] = o.astype(r.dtype)

    project(x_ref, out_refs)

    @pl.when(first_row_tile)
    def _():
        project(xs_ref, sample_out_refs)


def seg_matmul(x, xs, w, layer, seg_starts, seg_width, epilogue, out_dtypes, vecs=(), consts=(),
               *, tm, tn, name):
    m, k = x.shape
    ms = xs.shape[0]
    ns = len(seg_starts)
    no = len(out_dtypes)
    in_specs = [pl.BlockSpec((tm, k), lambda j, i: (i, 0)),
                pl.BlockSpec((ms, k), lambda j, i: (0, 0))]
    for st in seg_starts:
        in_specs.append(pl.BlockSpec((None, k, tn), lambda j, i, off=st // tn: (layer, 0, off + j)))
    for _ in vecs:
        in_specs.append(pl.BlockSpec((1, tn), lambda j, i: (0, j)))
    for c in consts:
        in_specs.append(pl.BlockSpec(c.shape, lambda j, i, nd=c.ndim: (0,) * nd))
    res = pl.pallas_call(
        functools.partial(_seg_matmul_body, ns=ns, nv=len(vecs), nc=len(consts), no=no,
                          epilogue=epilogue),
        grid=(seg_width // tn, m // tm),
        in_specs=in_specs,
        out_specs=([pl.BlockSpec((tm, tn), lambda j, i: (i, j)) for _ in out_dtypes]
                   + [pl.BlockSpec((ms, tn), lambda j, i: (0, j)) for _ in out_dtypes]),
        out_shape=([jax.ShapeDtypeStruct((m, seg_width), dt) for dt in out_dtypes]
                   + [jax.ShapeDtypeStruct((ms, seg_width), dt) for dt in out_dtypes]),
        scratch_shapes=[pltpu.VMEM((ns, k, tn), BF16)],
        compiler_params=_cparams(2),
        name=name,
    )(x, xs, *([w] * ns), *vecs, *consts)
    return res[:no], res[no:]


def _gelu_exact_f32(a):
    z = a * (2.0 ** -0.5)
    az = jnp.abs(z)
    t = 1.0 / (1.0 + 0.3275911 * az)
    poly = t * (0.254829592 + t * (-0.284496736 + t * (1.421413741
                                                       + t * (-1.453152027 + t * 1.061405429))))
    erfc_abs = poly * jnp.exp(-az * az)
    return 0.5 * a * jnp.where(z >= 0, 2.0 - erfc_abs, erfc_abs)


def _gelu_epilogue(accs, vecs, consts):
    (a,) = accs
    return [_gelu_exact_f32(a)]


def _headnorm_epilogue(accs, vecs, consts):
    aq, ak, av = accs
    gq, gk = vecs
    (group_ones,) = consts

    def head_norm(a, g):
        ms = jnp.dot((a * a).astype(BF16), group_ones, preferred_element_type=F32) * (1.0 / B_HEAD_DIM)
        return a * lax.rsqrt(ms + EPS) * g

    return [head_norm(aq, gq), head_norm(ak, gk), av]


def _hgrn_gate_epilogue(accs, vecs, consts):
    aq, af, av, ag = accs
    (lb,) = vecs
    q = aq * _sigmoid(aq)
    sig = _sigmoid(af)
    f = lb + (1.0 - lb) * sig
    k = (1.0 - lb) * (1.0 - sig)
    return [q, k, jnp.log(f), av, ag]


def _out_proj_body(y_ref, ys_ref, w_ref, h_ref, hs_ref, g_ref, ho_ref, xo_ref, hso_ref, xso_ref,
                   wb_ref, *, cast_rows):
    first_row_tile = pl.program_id(0) == 0

    @pl.when(first_row_tile)
    def _():
        def cast(r, carry):
            sl = pl.ds(pl.multiple_of(r * cast_rows, cast_rows), cast_rows)
            wb_ref[sl, :] = w_ref[sl, :].astype(BF16)
            return carry
        lax.fori_loop(0, w_ref.shape[0] // cast_rows, cast, 0)

    def project(rows_ref, res_ref, h_out_ref, x_out_ref):
        hn = res_ref[...] + jnp.dot(rows_ref[...].astype(BF16), wb_ref[...],
                                    preferred_element_type=F32)
        h_out_ref[...] = hn
        x_out_ref[...] = _rmsnorm_f32(hn, g_ref[...]).astype(x_out_ref.dtype)

    project(y_ref, h_ref, ho_ref, xo_ref)

    @pl.when(first_row_tile)
    def _():
        project(ys_ref, hs_ref, hso_ref, xso_ref)


def out_proj(y, ys, w, layer, h, hs, gain_next, *, tm):
    m, k = y.shape
    ms = ys.shape[0]
    n = w.shape[2]
    row = lambda cols: pl.BlockSpec((tm, cols), lambda i: (i, 0))
    sample = lambda cols: pl.BlockSpec((ms, cols), lambda i: (0, 0))
    return pl.pallas_call(
        functools.partial(_out_proj_body, cast_rows=256),
        grid=(m // tm,),
        in_specs=[row(k), sample(k),
                  pl.BlockSpec((None, k, n), lambda i: (layer, 0, 0), pipeline_mode=pl.Buffered(1)),
                  row(n), sample(n),
                  pl.BlockSpec((1, n), lambda i: (0, 0))],
        out_specs=[row(n), row(n), sample(n), sample(n)],
        out_shape=[jax.ShapeDtypeStruct((m, n), F32), jax.ShapeDtypeStruct((m, n), BF16),
                   jax.ShapeDtypeStruct((ms, n), F32), jax.ShapeDtypeStruct((ms, n), BF16)],
        scratch_shapes=[pltpu.VMEM((k, n), BF16)],
        compiler_params=_cparams(1),
        name="out_proj",
    )(y, ys, w, h, hs, gain_next.reshape(1, n))


def _ffn_body(x_ref, h_ref, xs_ref, hs_ref, wg_ref, wu_ref, wd_ref, g_ref, *out_refs, n_t, with_norm):
    if with_norm:
        ho_ref, hso_ref, xo_ref, xso_ref = out_refs
    else:
        ho_ref, hso_ref = out_refs
    i = pl.program_id(0)
    t = pl.program_id(1)

    @pl.when(t == 0)
    def _():
        ho_ref[...] = jnp.zeros(ho_ref.shape, F32)

    slice_rows = h_ref.shape[0]
    for p in range(FFN_RESIDUAL_SLICES):
        @pl.when(t == p)
        def _(p=p):
            ho_ref[p * slice_rows:(p + 1) * slice_rows, :] += h_ref[...]

    def swiglu(x):
        gate = jnp.dot(x, wg_ref[...].astype(BF16), preferred_element_type=F32)
        up = jnp.dot(x, wu_ref[...].astype(BF16), preferred_element_type=F32)
        act = (gate * _sigmoid(gate) * up).astype(BF16)
        return jnp.dot(act, wd_ref[...].astype(BF16), preferred_element_type=F32)

    ho_ref[...] += swiglu(x_ref[...])

    @pl.when((i == 0) & (t == 0))
    def _():
        hso_ref[...] = hs_ref[...]

    @pl.when(i == 0)
    def _():
        hso_ref[...] += swiglu(xs_ref[...])

    if with_norm:
        @pl.when(t == n_t - 1)
        def _():
            xo_ref[...] = _rmsnorm_f32(ho_ref[...], g_ref[...]).astype(BF16)

        @pl.when((i == 0) & (t == n_t - 1))
        def _():
            xso_ref[...] = _rmsnorm_f32(hso_ref[...], g_ref[...]).astype(BF16)


def ffn(x, h, xs, hs, w_gu, w_down, layer, gain_next, *, tm, th):
    m, d = x.shape
    ms = xs.shape[0]
    hidden = w_down.shape[1]
    n_t = hidden // th
    with_norm = gain_next is not None
    gain = gain_next if with_norm else jnp.ones((d,), F32)
    row_out = pl.BlockSpec((tm, d), lambda i, t: (i, 0))
    sample_block = pl.BlockSpec((ms, d), lambda i, t: (0, 0))
    out_specs = [row_out, sample_block]
    out_shape = [jax.ShapeDtypeStruct((m, d), F32), jax.ShapeDtypeStruct((ms, d), F32)]
    if with_norm:
        out_specs += [row_out, sample_block]
        out_shape += [jax.ShapeDtypeStruct((m, d), BF16), jax.ShapeDtypeStruct((ms, d), BF16)]
    assert n_t >= FFN_RESIDUAL_SLICES and tm % (FFN_RESIDUAL_SLICES * SUBLANES) == 0
    row_block = pl.BlockSpec((tm, d), lambda i, t: (i, 0), pipeline_mode=pl.Buffered(1))
    last_slice = FFN_RESIDUAL_SLICES - 1
    residual_slice = pl.BlockSpec(
        (tm // FFN_RESIDUAL_SLICES, d),
        lambda i, t: (i * FFN_RESIDUAL_SLICES + jnp.minimum(t, last_slice), 0))
    res = pl.pallas_call(
        functools.partial(_ffn_body, n_t=n_t, with_norm=with_norm),
        grid=(m // tm, n_t),
        in_specs=[row_block, residual_slice, sample_block, sample_block,
                  pl.BlockSpec((None, d, th), lambda i, t: (layer, 0, t)),
                  pl.BlockSpec((None, d, th), lambda i, t: (layer, 0, n_t + t)),
                  pl.BlockSpec((None, th, d), lambda i, t: (layer, t, 0)),
                  pl.BlockSpec((1, d), lambda i, t: (0, 0))],
        out_specs=out_specs,
        out_shape=out_shape,
        compiler_params=pltpu.CompilerParams(dimension_semantics=("arbitrary",) * 2,
                                             vmem_limit_bytes=FFN_VMEM_LIMIT_BYTES),
        name="ffn",
    )(x, h, xs, hs, w_gu, w_gu, w_down, gain.reshape(1, d))
    return tuple(res) if with_norm else (res[0], res[1], None, None)


def _spatial_body(u_ref, v_ref, wm_ref, bs_ref, vg_ref, p_ref, *maybe_vn_ref, chunk, groups):
    v = v_ref[...].astype(F32)
    vn = _rmsnorm_f32(v, vg_ref[...])
    if maybe_vn_ref:
        maybe_vn_ref[0][...] = vn
    vnb = vn.astype(BF16)
    rows, width = v.shape
    gw = width // groups
    for c in range(rows // chunk):
        r0 = c * chunk
        for g in range(groups):
            c0 = g * gw
            s = jnp.dot(wm_ref[g], vnb[r0:r0 + chunk, c0:c0 + gw], preferred_element_type=F32)
            s = s + bs_ref[g]
            u = u_ref[r0:r0 + chunk, c0:c0 + gw].astype(F32)
            p_ref[r0:r0 + chunk, c0:c0 + gw] = (u * s).astype(p_ref.dtype)


def spatial_mix(uv, wm, bs, v_gain, *, chunk, tm, with_vn):
    m, w2 = uv.shape
    width = w2 // 2
    groups = wm.shape[0]
    row_out = pl.BlockSpec((tm, width), lambda i: (i, 0))
    out_specs = [row_out]
    out_shape = [jax.ShapeDtypeStruct((m, width), BF16)]
    if with_vn:
        out_specs.append(row_out)
        out_shape.append(jax.ShapeDtypeStruct((m, width), F32))
    res = pl.pallas_call(
        functools.partial(_spatial_body, chunk=chunk, groups=groups),
        grid=(m // tm,),
        in_specs=[pl.BlockSpec((tm, width), lambda i: (i, 0)),
                  pl.BlockSpec((tm, width), lambda i: (i, 1)),
                  pl.BlockSpec(wm.shape, lambda i: (0, 0, 0)),
                  pl.BlockSpec(bs.shape, lambda i: (0, 0, 0)),
                  pl.BlockSpec((1, width), lambda i: (0, 0))],
        out_specs=out_specs,
        out_shape=out_shape,
        compiler_params=_cparams(1),
        name="spatial_mix",
    )(uv, uv, wm, bs, v_gain.reshape(1, width))
    return (res[0], res[1]) if with_vn else (res[0], None)


def _diff_attn_body(qt_tab, kt_tab, q_ref, k_ref, v_ref, sc_ref, sub_ref, o_ref,
                    qs_ref, m_ref, l_ref, acc_ref, *, tq, hb, out_scale, fixed_shift):
    t = pl.program_id(2)
    qi = qt_tab[t]
    ki = kt_tab[t]
    hd = 2 * B_HEAD_DIM

    @pl.when(ki == 0)
    def _():
        for h in range(hb):
            q = q_ref[:, h * hd:(h + 1) * hd].astype(F32) * (B_HEAD_DIM ** -0.5)
            lane = lax.broadcasted_iota(jnp.int32, q.shape, 1)
            qs_ref[h, 0:tq, :] = jnp.where(lane < B_HEAD_DIM, q, 0.0)
            qs_ref[h, tq:2 * tq, :] = jnp.where(lane >= B_HEAD_DIM, q, 0.0)
        if not fixed_shift:
            m_ref[...] = jnp.full(m_ref.shape, -jnp.inf, F32)
        l_ref[...] = jnp.zeros(l_ref.shape, F32)
        acc_ref[...] = jnp.zeros(acc_ref.shape, F32)

    def strip(h, r0, masked):
        rs = pl.ds(r0, tq)
        keys = k_ref[:, h * hd:(h + 1) * hd]
        vals = v_ref[:, h * B_V_DIM:(h + 1) * B_V_DIM]
        s = lax.dot_general(qs_ref[h, rs, :], keys, (((1,), (1,)), ((), ())),
                            preferred_element_type=F32)
        if masked:
            row = lax.broadcasted_iota(jnp.int32, s.shape, 0)
            col = lax.broadcasted_iota(jnp.int32, s.shape, 1)
            s = jnp.where(col <= row, s, -jnp.inf)
        if fixed_shift:
            p = jnp.exp(s - sc_ref[1])
            part = p[:, 0:LANES]
            for c0 in range(LANES, p.shape[1], LANES):
                part = part + p[:, c0:c0 + LANES]
            l_ref[h, rs, :] += part
            acc_ref[h, rs, :] += jnp.dot(p, vals, preferred_element_type=F32)
        else:
            m_prev = m_ref[h, rs, :]
            m_new = jnp.maximum(m_prev, jnp.max(s, axis=1, keepdims=True))
            alpha = jnp.exp(m_prev - m_new)
            p = jnp.exp(s - m_new[:, 0:1])
            l_ref[h, rs, :] = alpha * l_ref[h, rs, :] + jnp.sum(p, axis=1, keepdims=True)
            acc_ref[h, rs, :] = alpha * acc_ref[h, rs, :] + jnp.dot(p, vals, preferred_element_type=F32)
            m_ref[h, rs, :] = m_new

    @pl.when(ki < qi)
    def _():
        for h in range(hb):
            strip(h, 0, False)
            strip(h, tq, False)

    @pl.when(ki == qi)
    def _():
        for h in range(hb):
            strip(h, 0, True)
            strip(h, tq, True)
            l = l_ref[h]
            if fixed_shift:
                l = jnp.sum(l, axis=1, keepdims=True)
            o = acc_ref[h] / l
            d = o[0:tq] - sc_ref[0] * o[tq:2 * tq]
            o_ref[:, h * B_V_DIM:(h + 1) * B_V_DIM] = (
                _rmsnorm_f32(d, sub_ref[...]) * out_scale).astype(o_ref.dtype)


def diff_attn_prompt(q, k, v, lam, score_bound, subln, lam_init, *, batch, seq, heads, tq, hb,
                     fixed_shift):
    nq = seq // tq
    tri = [(qi, ki) for qi in range(nq) for ki in range(qi + 1)]
    qt_tab = jnp.asarray([a for a, _ in tri], jnp.int32)
    kt_tab = jnp.asarray([b for _, b in tri], jnp.int32)
    hd = 2 * B_HEAD_DIM
    grid_spec = pltpu.PrefetchScalarGridSpec(
        num_scalar_prefetch=2,
        grid=(batch, heads // hb, len(tri)),
        in_specs=[pl.BlockSpec((tq, hb * hd), lambda b, h, t, qt, kt: (b * nq + qt[t], h)),
                  pl.BlockSpec((tq, hb * hd), lambda b, h, t, qt, kt: (b * nq + kt[t], h)),
                  pl.BlockSpec((tq, hb * B_V_DIM), lambda b, h, t, qt, kt: (b * nq + kt[t], h)),
                  pl.BlockSpec(memory_space=pltpu.SMEM),
                  pl.BlockSpec((1, B_V_DIM), lambda b, h, t, qt, kt: (0, 0))],
        out_specs=pl.BlockSpec((tq, hb * B_V_DIM), lambda b, h, t, qt, kt: (b * nq + qt[t], h)),
        scratch_shapes=[pltpu.VMEM((hb, 2 * tq, hd), F32),
                        pltpu.VMEM((hb, 2 * tq, LANES), F32),
                        pltpu.VMEM((hb, 2 * tq, LANES), F32),
                        pltpu.VMEM((hb, 2 * tq, B_V_DIM), F32)],
    )
    return pl.pallas_call(
        functools.partial(_diff_attn_body, tq=tq, hb=hb, out_scale=1.0 - lam_init,
                          fixed_shift=fixed_shift),
        grid_spec=grid_spec,
        out_shape=jax.ShapeDtypeStruct((batch * seq, heads * B_V_DIM), BF16),
        compiler_params=_cparams(3),
        name="diff_attn_prompt_fixed_shift" if fixed_shift else "diff_attn_prompt",
    )(qt_tab, kt_tab, q, k, v, jnp.stack([lam, score_bound]).astype(F32),
      subln.reshape(1, B_V_DIM))


def _decode_attn_body(pt_ref, *refs, pages, heads, n_q, n_groups, out_scale):
    k_refs = refs[:pages]
    v_refs = refs[pages:2 * pages]
    kn_ref, vn_ref, qt_ref, lam_ref, sub_ref, o_ref, m_ref, l_ref, acc_ref = refs[2 * pages:]
    g = pl.program_id(1)
    hg = heads // SUBLANES
    hc = 2 * n_q
    cols = SUBLANES * hc

    @pl.when(g == 0)
    def _():
        m_ref[...] = jnp.full(m_ref.shape, -jnp.inf, F32)
        l_ref[...] = jnp.zeros(l_ref.shape, F32)
        acc_ref[...] = jnp.zeros(acc_ref.shape, F32)

    sub = lax.broadcasted_iota(jnp.int32, (SUBLANES, cols), 0)
    lane = lax.broadcasted_iota(jnp.int32, (SUBLANES, cols), 1)
    own = sub == lane // hc

    def to_column(x8):
        r = jnp.sum(jnp.where(own, x8, 0.0), axis=0, keepdims=True)
        return jnp.broadcast_to(r, (LANES, cols)).T

    def group_rows(ref, j, n_pos):
        x = ref[0:n_pos * heads, :].reshape(n_pos, hg, SUBLANES, 2 * B_HEAD_DIM)
        return x[:, j].reshape(n_pos * SUBLANES, 2 * B_HEAD_DIM)

    def process(page_k_refs, page_v_refs, n_pos, new_tokens):
        for j in range(hg):
            scores = []
            for k_ref in page_k_refs:
                s = jnp.dot(group_rows(k_ref, j, n_pos), qt_ref[j], preferred_element_type=F32)
                s = s.reshape(n_pos, SUBLANES, cols)
                valid = own[None]
                if new_tokens:
                    pos = lax.broadcasted_iota(jnp.int32, s.shape, 0)
                    qry = lax.broadcasted_iota(jnp.int32, s.shape, 2) % n_q
                    valid = valid & (pos <= qry)
                scores.append(jnp.where(valid, s, -jnp.inf))
            m_prev = m_ref[j]
            m_new = m_prev
            for s in scores:
                m_new = jnp.maximum(m_new, jnp.max(s, axis=0))
            m_safe = jnp.where(own, m_new, 0.0)
            alpha = jnp.exp(m_prev - m_safe)
            l_new = alpha * l_ref[j]
            pv = jnp.zeros((cols, B_V_DIM), F32)
            for s, v_ref in zip(scores, page_v_refs):
                p = jnp.exp(s - m_safe[None])
                l_new = l_new + jnp.sum(p, axis=0)
                pv = pv + lax.dot_general(p.reshape(n_pos * SUBLANES, cols), group_rows(v_ref, j, n_pos),
                                          (((0,), (0,)), ((), ())), preferred_element_type=F32)
            acc_ref[j] = acc_ref[j] * to_column(alpha) + pv
            l_ref[j] = l_new
            m_ref[j] = m_new

    process(k_refs, v_refs, PAGE_SIZE, False)

    @pl.when(g == n_groups - 1)
    def _():
        process([kn_ref], [vn_ref], n_q, True)
        for j in range(hg):
            o = acc_ref[j] / to_column(l_ref[j])
            o = o.reshape(SUBLANES, 2, n_q, B_V_DIM)
            d = o[:, 0] - lam_ref[0] * o[:, 1]
            d = d * lax.rsqrt(jnp.mean(d * d, axis=-1, keepdims=True) + EPS) * sub_ref[...]
            o_ref[j * SUBLANES:(j + 1) * SUBLANES] = (d * out_scale).astype(o_ref.dtype)


def diff_attn_decode(qt, cache_k, cache_v, page_table, k_new, v_new, lam, subln, lam_init,
                     *, layer, heads, n_q, pages):
    dec_b, n_pages = page_table.shape
    hd = 2 * B_HEAD_DIM
    rows = PAGE_SIZE * heads
    hg = heads // SUBLANES
    cols = SUBLANES * 2 * n_q
    n_groups = n_pages // pages
    page_spec = lambda p_i: pl.BlockSpec(
        (None, None, rows, hd),
        lambda b, g, pt, p_i=p_i: (layer, pt[b, g * pages + p_i], 0, 0))
    grid_spec = pltpu.PrefetchScalarGridSpec(
        num_scalar_prefetch=1,
        grid=(dec_b, n_groups),
        in_specs=([page_spec(p_i) for p_i in range(pages)] * 2
                  + [pl.BlockSpec((None, n_q * heads, hd), lambda b, g, pt: (b, 0, 0)),
                     pl.BlockSpec((None, n_q * heads, hd), lambda b, g, pt: (b, 0, 0)),
                     pl.BlockSpec((None, hg, hd, cols), lambda b, g, pt: (b, 0, 0, 0)),
                     pl.BlockSpec(memory_space=pltpu.SMEM),
                     pl.BlockSpec((1, B_V_DIM), lambda b, g, pt: (0, 0))]),
        out_specs=pl.BlockSpec((None, heads, n_q, B_V_DIM), lambda b, g, pt: (b, 0, 0, 0)),
        scratch_shapes=[pltpu.VMEM((hg, SUBLANES, cols), F32),
                        pltpu.VMEM((hg, SUBLANES, cols), F32),
                        pltpu.VMEM((hg, cols, B_V_DIM), F32)],
    )
    return pl.pallas_call(
        functools.partial(_decode_attn_body, pages=pages, heads=heads, n_q=n_q,
                          n_groups=n_groups, out_scale=1.0 - lam_init),
        grid_spec=grid_spec,
        out_shape=jax.ShapeDtypeStruct((dec_b, heads, n_q, B_V_DIM), F32),
        compiler_params=_cparams(2),
        name="diff_attn_decode",
    )(page_table, *([cache_k] * pages), *([cache_v] * pages), k_new, v_new, qt,
      lam.reshape(1), subln.reshape(1, B_V_DIM))


def _cumsum_rows(x):
    c = x.shape[0]
    sub = lax.broadcasted_iota(jnp.int32, x.shape, 0) % HGRN_SUB
    d = 1
    while d < HGRN_SUB:
        x = x + jnp.where(sub >= d, pltpu.roll(x, d, axis=0), 0.0)
        d *= 2
    blocks = []
    carry = None
    for j in range(c // HGRN_SUB):
        blk = x[j * HGRN_SUB:(j + 1) * HGRN_SUB]
        if carry is not None:
            blk = blk + carry
        blocks.append(blk)
        carry = blk[HGRN_SUB - 1:HGRN_SUB]
    return jnp.concatenate(blocks, axis=0) if len(blocks) > 1 else blocks[0]


def _hgrn_chunk(q, k, lf, v, st, row_scr):
    c = q.shape[0]
    nb = c // HGRN_SUB
    gcum = _cumsum_rows(lf)
    row_scr[0] = gcum
    row_scr[1] = k
    row_scr[2] = v
    o = lax.dot_general(q * jnp.exp(gcum), st, (((1,), (1,)), ((), ())), preferred_element_type=F32)

    if nb > 1:
        row = lax.broadcasted_iota(jnp.int32, (c, C_HEAD_DIM), 0)
        t_idx = lax.broadcasted_iota(jnp.int32, (c, c), 0)
        s_idx = lax.broadcasted_iota(jnp.int32, (c, c), 1)
        a_off = None
        size = 2 * HGRN_SUB
        while size <= c:
            half = size // 2
            if size < c:
                g_mid = jnp.concatenate(
                    [jnp.broadcast_to(gcum[b0 + half - 1:b0 + half], (size, C_HEAD_DIM))
                     for b0 in range(0, c, size)], axis=0)
            else:
                g_mid = gcum[half - 1:half]
            upper = (row % size) >= half
            qd = q * jnp.exp(jnp.where(upper, gcum - g_mid, -jnp.inf))
            kd = k * jnp.exp(jnp.where(upper, -jnp.inf, g_mid - gcum))
            a = lax.dot_general(qd, kd, (((1,), (1,)), ((), ())), preferred_element_type=F32)
            if size < c:
                a = jnp.where((t_idx // size) == (s_idx // size), a, 0.0)
            a_off = a if a_off is None else a_off + a
            size *= 2
        o = o + jnp.dot(a_off, v, preferred_element_type=F32)

    sub_row = lax.broadcasted_iota(jnp.int32, (HGRN_SUB, C_HEAD_DIM), 0)
    o_blocks = []
    for i in range(nb):
        r0 = i * HGRN_SUB
        gi = gcum[r0:r0 + HGRN_SUB]
        qi = q[r0:r0 + HGRN_SUB]
        oi = o[r0:r0 + HGRN_SUB]
        for s in range(HGRN_SUB):
            r = r0 + s
            dec = jnp.exp(jnp.where(sub_row >= s, gi - row_scr[0, r:r + 1, :], -jnp.inf))
            a_col = jnp.sum(qi * row_scr[1, r:r + 1, :] * dec, axis=-1, keepdims=True)
            oi = oi + a_col * row_scr[2, r:r + 1, :]
        o_blocks.append(oi)
    o = jnp.concatenate(o_blocks, axis=0) if nb > 1 else o_blocks[0]
    g_last = gcum[c - 1:c]
    kd = k * jnp.exp(g_last - gcum)
    if c < C_HEAD_DIM:
        pad = jnp.zeros((C_HEAD_DIM - c, C_HEAD_DIM), F32)
        kd = jnp.concatenate([kd, pad], axis=0)
        v = jnp.concatenate([v, pad], axis=0)
    st_new = st * jnp.exp(g_last) + jnp.dot(v.T, kd, preferred_element_type=F32)
    return o, st_new


def _hgrn_body(*refs, chunk, n_chunks, n_r, hb, with_state):
    if with_state:
        q_ref, k_ref, lf_ref, v_ref, g_ref, gg_ref, s0_ref, o_ref, so_ref, st_ref, gs_ref = refs
    else:
        q_ref, k_ref, lf_ref, v_ref, g_ref, gg_ref, o_ref, so_ref, st_ref, gs_ref = refs
    r = pl.program_id(2)
    hd = C_HEAD_DIM

    @pl.when(r == 0)
    def _():
        for h in range(hb):
            if with_state:
                st_ref[h] = s0_ref[h].astype(F32).T
            else:
                st_ref[h] = jnp.zeros((hd, hd), F32)

    def step(ci, carry):
        base = pl.multiple_of(ci * chunk, chunk)
        sl = pl.ds(base, chunk)
        for h in range(hb):
            cs = slice(h * hd, (h + 1) * hd)
            o, st_new = _hgrn_chunk(q_ref[sl, cs], k_ref[sl, cs], lf_ref[sl, cs], v_ref[sl, cs],
                                    st_ref[h], gs_ref.at[h])
            st_ref[h] = st_new
            gate = g_ref[sl, cs]
            o = _rmsnorm_f32(o, gg_ref[...]) * (gate * _sigmoid(gate))
            o_ref[sl, cs] = o.astype(o_ref.dtype)
        return carry

    lax.fori_loop(0, n_chunks, step, 0)

    @pl.when(r == n_r - 1)
    def _():
        for h in range(hb):
            so_ref[h] = st_ref[h].T.astype(so_ref.dtype)


def hgrn_recurrence(q, k, lf, v, g, g_gain, state0, *, batch, seq, heads, rows, chunk, hb, out_dtype):
    n_r = seq // rows
    hd = C_HEAD_DIM
    with_state = state0 is not None
    row_spec = pl.BlockSpec((rows, hb * hd), lambda b, h, r: (b * n_r + r, h))
    state_spec = pl.BlockSpec((None, hb, hd, hd), lambda b, h, r: (b, h, 0, 0))
    in_specs = [row_spec] * 5 + [pl.BlockSpec((1, hd), lambda b, h, r: (0, 0))]
    args = [q, k, lf, v, g, g_gain.reshape(1, hd)]
    if with_state:
        in_specs.append(state_spec)
        args.append(state0)
    return pl.pallas_call(
        functools.partial(_hgrn_body, chunk=chunk, n_chunks=rows // chunk, n_r=n_r, hb=hb,
                          with_state=with_state),
        grid=(batch, heads // hb, n_r),
        in_specs=in_specs,
        out_specs=[row_spec, state_spec],
        out_shape=[jax.ShapeDtypeStruct((batch * seq, heads * hd), out_dtype),
                   jax.ShapeDtypeStruct((batch, heads, hd, hd), F32)],
        scratch_shapes=[pltpu.VMEM((hb, hd, hd), F32), pltpu.VMEM((hb, 3, chunk, hd), F32)],
        compiler_params=_cparams(3),
        name="hgrn_recurrence",
    )(*args)


def _tiles(m):
    big = m >= 1024
    return dict(
        norm_tm=512 if big else m,
        proj_tm=1024 if big else m,
        proj_tn=256,
        a_proj_tn=1024,
        out_tm=512 if big else m,
        ffn_tm=1024 if big else m,
        ffn_th=512,
        spatial_tm=512 if big else m,
        attn_tq=512,
        hgrn_rows=1024,
        attn_heads=8,
        attn_heads_fixed_shift=16,
        hgrn_heads=8,
    )


def _mixer_a_core(uv, v_gain, w_s, b_s, *, chunk_len, n_seq, tiles, with_vn):
    causal = jnp.tril(jnp.ones((A_CHUNK, A_CHUNK), bool))
    w_masked = jnp.where(causal[None], w_s, 0.0)
    if chunk_len == A_CHUNK:
        wm, bs, chunk = w_masked, b_s, A_CHUNK
    else:
        eye = jnp.eye(n_seq, dtype=w_s.dtype)
        small = w_masked[:, :chunk_len, :chunk_len]
        wm = jnp.einsum("ab,gts->gatbs", eye, small).reshape(
            A_GROUPS, n_seq * chunk_len, n_seq * chunk_len)
        bs = jnp.tile(b_s[:, :chunk_len], (1, n_seq))
        chunk = n_seq * chunk_len
    p, vn = spatial_mix(uv, wm.astype(BF16), bs[:, :, None], v_gain, chunk=chunk,
                        tm=max(tiles["spatial_tm"], chunk) if chunk_len == A_CHUNK else chunk,
                        with_vn=with_vn)
    return p, vn


def _mixer_a_proj(xn, xns, w_in, layer, *, tiles):
    width2 = w_in.shape[2]
    (uv,), (uv_s,) = seg_matmul(xn, xns, w_in, layer, (0,), width2, _gelu_epilogue, (BF16,),
                                tm=tiles["proj_tm"], tn=tiles["a_proj_tn"], name="a_in_proj")
    return uv, uv_s


def _mixer_b_proj(xn, xns, w_in, layer, q_gain, k_gain, *, heads, tiles):
    width = heads * 2 * B_HEAD_DIM
    tn = tiles["proj_tn"]
    lane_group = np.arange(tn) // B_HEAD_DIM
    group_ones = jnp.asarray(lane_group[:, None] == lane_group[None, :], BF16)
    reps = width // B_HEAD_DIM
    gq = jnp.tile(q_gain.astype(F32), reps).reshape(1, width)
    gk = jnp.tile(k_gain.astype(F32), reps).reshape(1, width)
    return seg_matmul(xn, xns, w_in, layer, (0, width, 2 * width), width, _headnorm_epilogue,
                      (F32, F32, F32), vecs=(gq, gk), consts=(group_ones,),
                      tm=tiles["proj_tm"], tn=tn, name="b_in_proj")


def _mixer_c_proj(xn, xns, w_in, layer, lower_bound, *, tiles):
    width = w_in.shape[2] // 4
    return seg_matmul(xn, xns, w_in, layer, (0, width, 2 * width, 3 * width), width, _hgrn_gate_epilogue,
                      (F32,) * 5, vecs=(lower_bound.reshape(1, width),),
                      tm=tiles["proj_tm"], tn=tiles["proj_tn"], name="c_in_proj")


def kernel(x_prompt, x_sample, cache_k, cache_v, page_table, state_hgrn, norm_mix, norm_ffn, ffn_w_gu, ffn_w_down, a_w_in, a_v_norm, a_w_s, a_b_s, a_w_out, b_w_in, b_q_norm, b_k_norm, b_lambda_q1, b_lambda_k1, b_lambda_q2, b_lambda_k2, b_subln, b_w_out, c_w_in, c_g_norm, c_lower_bounds, c_w_out):
    batch, seq, d_model = x_prompt.shape
    dec_b, dec_seq, _ = x_sample.shape
    depth = norm_mix.shape[0]
    b_heads = d_model // (2 * B_HEAD_DIM)
    c_heads = d_model // C_HEAD_DIM
    mp, ms = batch * seq, dec_b * dec_seq
    tp, ts = _tiles(mp), _tiles(ms)

    probs = jax.nn.softmax(c_lower_bounds.astype(F32), axis=0)
    lower_bound = jnp.cumsum(probs, axis=0) - probs[0]

    h_p = x_prompt.reshape(mp, d_model)
    h_s = x_sample.reshape(ms, d_model)
    xn_p = norm_rows(h_p, norm_mix[0], tm=tp["norm_tm"])
    xn_s = norm_rows(h_s, norm_mix[0], tm=ts["norm_tm"])

    n_phys = cache_k.shape[1]
    cache_k2 = cache_k.reshape(cache_k.shape[0], n_phys, PAGE_SIZE * b_heads, 2 * B_HEAD_DIM)
    cache_v2 = cache_v.reshape(cache_v.shape[0], n_phys, PAGE_SIZE * b_heads, B_V_DIM)

    k_p_rows, v_p_rows, k_s_rows, v_s_rows = [], [], [], []
    hgrn_p, hgrn_s, chunk_v_s = [], [], []
    for i in range(depth):
        kind, j = i % 3, i // 3
        if kind == 0:
            uv_p, uv_s = _mixer_a_proj(xn_p, xn_s, a_w_in, j, tiles=tp)
            y_p, _ = _mixer_a_core(uv_p, a_v_norm[j], a_w_s[j], a_b_s[j],
                                   chunk_len=A_CHUNK, n_seq=batch, tiles=tp, with_vn=False)
            y_s, vn_s = _mixer_a_core(uv_s, a_v_norm[j], a_w_s[j], a_b_s[j],
                                      chunk_len=dec_seq, n_seq=dec_b, tiles=ts, with_vn=True)
            chunk_v_s.append(vn_s.reshape(dec_b, dec_seq, -1))
            w_out = a_w_out
        elif kind == 1:
            lam_init = 0.8 - 0.6 * math.exp(-0.3 * i)
            lam = (jnp.exp(jnp.sum(b_lambda_q1[j].astype(F32) * b_lambda_k1[j].astype(F32)))
                   - jnp.exp(jnp.sum(b_lambda_q2[j].astype(F32) * b_lambda_k2[j].astype(F32)))
                   + lam_init)
            (q_p, k_p, v_p), (q_s, k_s, v_s) = _mixer_b_proj(
                xn_p, xn_s, b_w_in, j, b_q_norm[j], b_k_norm[j], heads=b_heads, tiles=tp)
            score_bound = (B_HEAD_DIM ** 0.5) * jnp.max(jnp.abs(b_q_norm[j].astype(F32))) * jnp.max(
                jnp.abs(b_k_norm[j].astype(F32)))
            attn = functools.partial(diff_attn_prompt, batch=batch, seq=seq, heads=b_heads,
                                     tq=tp["attn_tq"])
            y_p = lax.cond(
                score_bound <= ATTN_FIXED_SHIFT_MAX_BOUND,
                lambda *a: attn(*a, lam_init, hb=min(b_heads, tp["attn_heads_fixed_shift"]),
                                fixed_shift=True),
                lambda *a: attn(*a, lam_init, hb=min(b_heads, tp["attn_heads"]), fixed_shift=False),
                q_p, k_p, v_p, lam, score_bound, b_subln[j])
            k_p_rows.append(k_p.reshape(batch, seq, b_heads, 2 * B_HEAD_DIM))
            v_p_rows.append(v_p.reshape(batch, seq, b_heads, B_V_DIM))

            q5 = (q_s *(B_HEAD_DIM ** -0.5)).reshape(dec_b, dec_seq, b_heads, 2, B_HEAD_DIM)
            qt = jnp.einsum("bthcd,ce->bhcted", q5, jnp.eye(2, dtype=F32))
            qt = qt.reshape(dec_b, b_heads // SUBLANES, SUBLANES * 2 * dec_seq, 2 * B_HEAD_DIM)
            qt = qt.transpose(0, 1, 3, 2)
            k_new = k_s.reshape(dec_b, dec_seq * b_heads, 2 * B_HEAD_DIM)
            v_new = v_s.reshape(dec_b, dec_seq * b_heads, B_V_DIM)
            o_s = diff_attn_decode(qt, cache_k2, cache_v2, page_table, k_new, v_new, lam,
                                   b_subln[j], lam_init, layer=j, heads=b_heads, n_q=dec_seq,
                                   pages=DECODE_PAGES_PER_STEP)
            y_s = o_s.transpose(0, 2, 1, 3).reshape(ms, d_model)
            k_s_rows.append(k_s.reshape(dec_b, dec_seq, b_heads, 2 * B_HEAD_DIM))
            v_s_rows.append(v_s.reshape(dec_b, dec_seq, b_heads, B_V_DIM))
            w_out = b_w_out
        else:
            qkv_p, qkv_s = _mixer_c_proj(xn_p, xn_s, c_w_in, j, lower_bound[i], tiles=tp)
            y_p, st_p = hgrn_recurrence(*qkv_p, c_g_norm[j], None, batch=batch, seq=seq,
                                        heads=c_heads, rows=tp["hgrn_rows"], chunk=C_HEAD_DIM,
                                        hb=tp["hgrn_heads"], out_dtype=BF16)
            y_s, st_s = hgrn_recurrence(*qkv_s, c_g_norm[j], state_hgrn[j], batch=dec_b,
                                        seq=dec_seq, heads=c_heads, rows=dec_seq, chunk=dec_seq,
                                        hb=ts["hgrn_heads"], out_dtype=F32)
            hgrn_p.append(st_p)
            hgrn_s.append(st_s)
            w_out = c_w_out
        h_p, xf_p, h_s, xf_s = out_proj(y_p, y_s, w_out, j, h_p, h_s, norm_ffn[i], tm=tp["out_tm"])
        gain_next = norm_mix[i + 1] if i + 1 < depth else None
        h_p, h_s, xn_p, xn_s = ffn(xf_p, h_p, xf_s, h_s, ffn_w_gu, ffn_w_down, i, gain_next,
                                   tm=tp["ffn_tm"], th=math.gcd(ffn_w_down.shape[1], tp["ffn_th"]))
    return (h_p.reshape(batch, seq, d_model), h_s.reshape(dec_b, dec_seq, d_model),
            jnp.stack(k_p_rows), jnp.stack(v_p_rows), jnp.stack(k_s_rows), jnp.stack(v_s_rows),
            jnp.stack(hgrn_p), jnp.stack(hgrn_s), jnp.stack(chunk_v_s))
```

```python
import functools
import math

import jax
import jax.numpy as jnp
import numpy as np
from jax import lax
from jax.experimental import pallas as pl
from jax.experimental.pallas import tpu as pltpu

F32 = jnp.float32
BF16 = jnp.bfloat16
EPS = 1e-6
LOG2_E = 1.4426950408889634

LANES = 128
SUBLANES = 8
VMEM_LIMIT_BYTES = 56 << 20
FFN_VMEM_LIMIT_BYTES = 63 << 20
FFN_RESIDUAL_SLICES = 8

A_CHUNK = 128
A_GROUPS = 8
B_HEAD_DIM = 64
B_V_DIM = 2 * B_HEAD_DIM
C_HEAD_DIM = 128
PAGE_SIZE = 128
HGRN_SUB = SUBLANES
DECODE_PAGES_PER_STEP = 8
ATTN_FIXED_SHIFT_MAX_BOUND = 30.0


def _cparams(n_axes):
    return pltpu.CompilerParams(
        dimension_semantics=("arbitrary",) * n_axes,
        vmem_limit_bytes=VMEM_LIMIT_BYTES,
    )


def _rmsnorm_f32(x, gain):
    return x * lax.rsqrt(jnp.mean(x * x, axis=-1, keepdims=True) + EPS) * gain


def _sigmoid(x):
    return 1.0 / (1.0 + jnp.exp(-x))


def _norm_rows_body(x_ref, g_ref, o_ref):
    o_ref[...] = _rmsnorm_f32(x_ref[...], g_ref[...]).astype(o_ref.dtype)


def norm_rows(x, gain, *, tm):
    m, d = x.shape
    return pl.pallas_call(
        _norm_rows_body,
        grid=(m // tm,),
        in_specs=[pl.BlockSpec((tm, d), lambda i: (i, 0)),
                  pl.BlockSpec((1, d), lambda i: (0, 0))],
        out_specs=pl.BlockSpec((tm, d), lambda i: (i, 0)),
        out_shape=jax.ShapeDtypeStruct((m, d), BF16),
        compiler_params=_cparams(1),
        name="norm_rows",
    )(x, gain.reshape(1, d))


def _seg_matmul_body(*refs, ns, nv, nc, no, epilogue):
    x_ref, xs_ref = refs[0:2]
    w_refs = refs[2:2 + ns]
    vec_refs = refs[2 + ns:2 + ns + nv]
    const_refs = refs[2 + ns + nv:2 + ns + nv + nc]
    out_refs = refs[2 + ns + nv + nc:2 + ns + nv + nc + no]
    sample_out_refs = refs[2 + ns + nv + nc + no:-1]
    wb_ref = refs[-1]
    first_row_tile = pl.program_id(1) == 0

    @pl.when(first_row_tile)
    def _():
        for s in range(ns):
            wb_ref[s] = w_refs[s][...].astype(BF16)

    def project(rows_ref, dst_refs):
        x = rows_ref[...]
        accs = [jnp.dot(x, wb_ref[s], preferred_element_type=F32) for s in range(ns)]
        outs = epilogue(accs, [r[...] for r in vec_refs], [r[...] for r in const_refs])
        for r, o in zip(dst_refs, outs):
            r[...] = o.astype(r.dtype)

    project(x_ref, out_refs)

    @pl.when(first_row_tile)
    def _():
        project(xs_ref, sample_out_refs)


def seg_matmul(x, xs, w, layer, seg_starts, seg_width, epilogue, out_dtypes, vecs=(), consts=(),
               *, tm, tn, name):
    m, k = x.shape
    ms = xs.shape[0]
    ns = len(seg_starts)
    no = len(out_dtypes)
    in_specs = [pl.BlockSpec((tm, k), lambda j, i: (i, 0)),
                pl.BlockSpec((ms, k), lambda j, i: (0, 0))]
    for st in seg_starts:
        in_specs.append(pl.BlockSpec((None, k, tn), lambda j, i, off=st // tn: (layer, 0, off + j)))
    for _ in vecs:
        in_specs.append(pl.BlockSpec((1, tn), lambda j, i: (0, j)))
    for c in consts:
        in_specs.append(pl.BlockSpec(c.shape, lambda j, i, nd=c.ndim: (0,) * nd))
    res = pl.pallas_call(
        functools.partial(_seg_matmul_body, ns=ns, nv=len(vecs), nc=len(consts), no=no,
                          epilogue=epilogue),
        grid=(seg_width // tn, m // tm),
        in_specs=in_specs,
        out_specs=([pl.BlockSpec((tm, tn), lambda j, i: (i, j)) for _ in out_dtypes]
                   + [pl.BlockSpec((ms, tn), lambda j, i: (0, j)) for _ in out_dtypes]),
        out_shape=([jax.ShapeDtypeStruct((m, seg_width), dt) for dt in out_dtypes]
                   + [jax.ShapeDtypeStruct((ms, seg_width), dt) for dt in out_dtypes]),
        scratch_shapes=[pltpu.VMEM((ns, k, tn), BF16)],
        compiler_params=_cparams(2),
        name=name,
    )(x, xs, *([w] * ns), *vecs, *consts)
    return res[:no], res[no:]


def _gelu_exact_f32(a):
    z = a * (2.0 ** -0.5)
    az = jnp.abs(z)
    t = 1.0 / (1.0 + 0.3275911 * az)
    poly = t * (0.254829592 + t * (-0.284496736 + t * (1.421413741
                                                       + t * (-1.453152027 + t * 1.061405429))))
    erfc_abs = poly * jnp.exp(-az * az)
    return 0.5 * a * jnp.where(z >= 0, 2.0 - erfc_abs, erfc_abs)


def _gelu_epilogue(accs, vecs, consts):
    (a,) = accs
    return [_gelu_exact_f32(a)]


def _headnorm_epilogue(accs, vecs, consts):
    aq, ak, av = accs
    gq, gk = vecs
    (group_ones,) = consts

    def head_norm(a, g):
        ms = jnp.dot((a * a).astype(BF16), group_ones, preferred_element_type=F32) * (1.0 / B_HEAD_DIM)
        return a * lax.rsqrt(ms + EPS) * g

    return [head_norm(aq, gq), head_norm(ak, gk), av]


def _hgrn_gate_epilogue(accs, vecs, consts):
    aq, af, av, ag = accs
    (lb,) = vecs
    q = aq * _sigmoid(aq)
    sig = _sigmoid(af)
    f = lb + (1.0 - lb) * sig
    k = (1.0 - lb) * (1.0 - sig)
    return [q, k, jnp.log(f), av, ag]


def _out_proj_body(y_ref, ys_ref, w_ref, h_ref, hs_ref, g_ref, ho_ref, xo_ref, hso_ref, xso_ref,
                   wb_ref, *, cast_rows):
    first_row_tile = pl.program_id(0) == 0

    @pl.when(first_row_tile)
    def _():
        def cast(r, carry):
            sl = pl.ds(pl.multiple_of(r * cast_rows, cast_rows), cast_rows)
            wb_ref[sl, :] = w_ref[sl, :].astype(BF16)
            return carry
        lax.fori_loop(0, w_ref.shape[0] // cast_rows, cast, 0)

    def project(rows_ref, res_ref, h_out_ref, x_out_ref):
        hn = res_ref[...] + jnp.dot(rows_ref[...].astype(BF16), wb_ref[...],
                                    preferred_element_type=F32)
        h_out_ref[...] = hn
        x_out_ref[...] = _rmsnorm_f32(hn, g_ref[...]).astype(x_out_ref.dtype)

    project(y_ref, h_ref, ho_ref, xo_ref)

    @pl.when(first_row_tile)
    def _():
        project(ys_ref, hs_ref, hso_ref, xso_ref)


def out_proj(y, ys, w, layer, h, hs, gain_next, *, tm):
    m, k = y.shape
    ms = ys.shape[0]
    n = w.shape[2]
    row = lambda cols: pl.BlockSpec((tm, cols), lambda i: (i, 0))
    sample = lambda cols: pl.BlockSpec((ms, cols), lambda i: (0, 0))
    return pl.pallas_call(
        functools.partial(_out_proj_body, cast_rows=256),
        grid=(m // tm,),
        in_specs=[row(k), sample(k),
                  pl.BlockSpec((None, k, n), lambda i: (layer, 0, 0), pipeline_mode=pl.Buffered(1)),
                  row(n), sample(n),
                  pl.BlockSpec((1, n), lambda i: (0, 0))],
        out_specs=[row(n), row(n), sample(n), sample(n)],
        out_shape=[jax.ShapeDtypeStruct((m, n), F32), jax.ShapeDtypeStruct((m, n), BF16),
                   jax.ShapeDtypeStruct((ms, n), F32), jax.ShapeDtypeStruct((ms, n), BF16)],
        scratch_shapes=[pltpu.VMEM((k, n), BF16)],
        compiler_params=_cparams(1),
        name="out_proj",
    )(y, ys, w, h, hs, gain_next.reshape(1, n))


def _ffn_body(x_ref, h_ref, xs_ref, hs_ref, wg_ref, wu_ref, wd_ref, g_ref, *out_refs, n_t, with_norm):
    if with_norm:
        ho_ref, hso_ref, xo_ref, xso_ref = out_refs
    else:
        ho_ref, hso_ref = out_refs
    i = pl.program_id(0)
    t = pl.program_id(1)

    @pl.when(t == 0)
    def _():
        ho_ref[...] = jnp.zeros(ho_ref.shape, F32)

    slice_rows = h_ref.shape[0]
    for p in range(FFN_RESIDUAL_SLICES):
        @pl.when(t == p)
        def _(p=p):
            ho_ref[p * slice_rows:(p + 1) * slice_rows, :] += h_ref[...]

    def swiglu(x):
        gate = jnp.dot(x, wg_ref[...].astype(BF16), preferred_element_type=F32)
        up = jnp.dot(x, wu_ref[...].astype(BF16), preferred_element_type=F32)
        act = (gate * _sigmoid(gate) * up).astype(BF16)
        return jnp.dot(act, wd_ref[...].astype(BF16), preferred_element_type=F32)

    ho_ref[...] += swiglu(x_ref[...])

    @pl.when((i == 0) & (t == 0))
    def _():
        hso_ref[...] = hs_ref[...]

    @pl.when(i == 0)
    def _():
        hso_ref[...] += swiglu(xs_ref[...])

    if with_norm:
        @pl.when(t == n_t - 1)
        def _():
            xo_ref[...] = _rmsnorm_f32(ho_ref[...], g_ref[...]).astype(BF16)

        @pl.when((i == 0) & (t == n_t - 1))
        def _():
            xso_ref[...] = _rmsnorm_f32(hso_ref[...], g_ref[...]).astype(BF16)


def ffn(x, h, xs, hs, w_gu, w_down, layer, gain_next, *, tm, th):
    m, d = x.shape
    ms = xs.shape[0]
    hidden = w_down.shape[1]
    n_t = hidden // th
    with_norm = gain_next is not None
    gain = gain_next if with_norm else jnp.ones((d,), F32)
    row_out = pl.BlockSpec((tm, d), lambda i, t: (i, 0))
    sample_block = pl.BlockSpec((ms, d), lambda i, t: (0, 0))
    out_specs = [row_out, sample_block]
    out_shape = [jax.ShapeDtypeStruct((m, d), F32), jax.ShapeDtypeStruct((ms, d), F32)]
    if with_norm:
        out_specs += [row_out, sample_block]
        out_shape += [jax.ShapeDtypeStruct((m, d), BF16), jax.ShapeDtypeStruct((ms, d), BF16)]
    assert n_t >= FFN_RESIDUAL_SLICES and tm % (FFN_RESIDUAL_SLICES * SUBLANES) == 0
    row_block = pl.BlockSpec((tm, d), lambda i, t: (i, 0), pipeline_mode=pl.Buffered(1))
    last_slice = FFN_RESIDUAL_SLICES - 1
    residual_slice = pl.BlockSpec(
        (tm // FFN_RESIDUAL_SLICES, d),
        lambda i, t: (i * FFN_RESIDUAL_SLICES + jnp.minimum(t, last_slice), 0))
    res = pl.pallas_call(
        functools.partial(_ffn_body, n_t=n_t, with_norm=with_norm),
        grid=(m // tm, n_t),
        in_specs=[row_block, residual_slice, sample_block, sample_block,
                  pl.BlockSpec((None, d, th), lambda i, t: (layer, 0, t)),
                  pl.BlockSpec((None, d, th), lambda i, t: (layer, 0, n_t + t)),
                  pl.BlockSpec((None, th, d), lambda i, t: (layer, t, 0)),
                  pl.BlockSpec((1, d), lambda i, t: (0, 0))],
        out_specs=out_specs,
        out_shape=out_shape,
        compiler_params=pltpu.CompilerParams(dimension_semantics=("arbitrary",) * 2,
                                             vmem_limit_bytes=FFN_VMEM_LIMIT_BYTES),
        name="ffn",
    )(x, h, xs, hs, w_gu, w_gu, w_down, gain.reshape(1, d))
    return tuple(res) if with_norm else (res[0], res[1], None, None)


def _spatial_body(u_ref, v_ref, wm_ref, bs_ref, vg_ref, p_ref, *maybe_vn_ref, chunk, groups):
    v = v_ref[...].astype(F32)
    vn = _rmsnorm_f32(v, vg_ref[...])
    if maybe_vn_ref:
        maybe_vn_ref[0][...] = vn
    vnb = vn.astype(BF16)
    rows, width = v.shape
    gw = width // groups
    for c in range(rows // chunk):
        r0 = c * chunk
        for g in range(groups):
            c0 = g * gw
            s = jnp.dot(wm_ref[g], vnb[r0:r0 + chunk, c0:c0 + gw], preferred_element_type=F32)
            s = s + bs_ref[g]
            u = u_ref[r0:r0 + chunk, c0:c0 + gw].astype(F32)
            p_ref[r0:r0 + chunk, c0:c0 + gw] = (u * s).astype(p_ref.dtype)


def spatial_mix(uv, wm, bs, v_gain, *, chunk, tm, with_vn):
    m, w2 = uv.shape
    width = w2 // 2
    groups = wm.shape[0]
    row_out = pl.BlockSpec((tm, width), lambda i: (i, 0))
    out_specs = [row_out]
    out_shape = [jax.ShapeDtypeStruct((m, width), BF16)]
    if with_vn:
        out_specs.append(row_out)
        out_shape.append(jax.ShapeDtypeStruct((m, width), F32))
    res = pl.pallas_call(
        functools.partial(_spatial_body, chunk=chunk, groups=groups),
        grid=(m // tm,),
        in_specs=[pl.BlockSpec((tm, width), lambda i: (i, 0)),
                  pl.BlockSpec((tm, width), lambda i: (i, 1)),
                  pl.BlockSpec(wm.shape, lambda i: (0, 0, 0)),
                  pl.BlockSpec(bs.shape, lambda i: (0, 0, 0)),
                  pl.BlockSpec((1, width), lambda i: (0, 0))],
        out_specs=out_specs,
        out_shape=out_shape,
        compiler_params=_cparams(1),
        name="spatial_mix",
    )(uv, uv, wm, bs, v_gain.reshape(1, width))
    return (res[0], res[1]) if with_vn else (res[0], None)


def _diff_attn_body(qt_tab, kt_tab, q_ref, k_ref, v_ref, sc_ref, sub_ref, o_ref,
                    qs_ref, m_ref, l_ref, acc_ref, *, tq, hb, out_scale, fixed_shift):
    t = pl.program_id(2)
    qi = qt_tab[t]
    ki = kt_tab[t]
    hd = 2 * B_HEAD_DIM

    @pl.when(ki == 0)
    def _():
        for h in range(hb):
            q = q_ref[:, h * hd:(h + 1) * hd].astype(F32) * (B_HEAD_DIM ** -0.5)
            lane = lax.broadcasted_iota(jnp.int32, q.shape, 1)
            qs_ref[h, 0:tq, :] = jnp.where(lane < B_HEAD_DIM, q, 0.0)
            qs_ref[h, tq:2 * tq, :] = jnp.where(lane >= B_HEAD_DIM, q, 0.0)
        if not fixed_shift:
            m_ref[...] = jnp.full(m_ref.shape, -jnp.inf, F32)
        l_ref[...] = jnp.zeros(l_ref.shape, F32)
        acc_ref[...] = jnp.zeros(acc_ref.shape, F32)

    def strip(h, r0, masked):
        rs = pl.ds(r0, tq)
        keys = k_ref[:, h * hd:(h + 1) * hd]
        vals = v_ref[:, h * B_V_DIM:(h + 1) * B_V_DIM]
        s = lax.dot_general(qs_ref[h, rs, :], keys, (((1,), (1,)), ((), ())),
                            preferred_element_type=F32)
        if masked:
            row = lax.broadcasted_iota(jnp.int32, s.shape, 0)
            col = lax.broadcasted_iota(jnp.int32, s.shape, 1)
            s = jnp.where(col <= row, s, -jnp.inf)
        if fixed_shift:
            p = jnp.exp(s - sc_ref[1])
            part = p[:, 0:LANES]
            for c0 in range(LANES, p.shape[1], LANES):
                part = part + p[:, c0:c0 + LANES]
            l_ref[h, rs, :] += part
            acc_ref[h, rs, :] += jnp.dot(p, vals, preferred_element_type=F32)
        else:
            m_prev = m_ref[h, rs, :]
            m_new = jnp.maximum(m_prev, jnp.max(s, axis=1, keepdims=True))
            alpha = jnp.exp(m_prev - m_new)
            p = jnp.exp(s - m_new[:, 0:1])
            l_ref[h, rs, :] = alpha * l_ref[h, rs, :] + jnp.sum(p, axis=1, keepdims=True)
            acc_ref[h, rs, :] = alpha * acc_ref[h, rs, :] + jnp.dot(p, vals, preferred_element_type=F32)
            m_ref[h, rs, :] = m_new

    @pl.when(ki < qi)
    def _():
        for h in range(hb):
            strip(h, 0, False)
            strip(h, tq, False)

    @pl.when(ki == qi)
    def _():
        for h in range(hb):
            strip(h, 0, True)
            strip(h, tq, True)
            l = l_ref[h]
            if fixed_shift:
                l = jnp.sum(l, axis=1, keepdims=True)
            o = acc_ref[h] / l
            d = o[0:tq] - sc_ref[0] * o[tq:2 * tq]
            o_ref[:, h * B_V_DIM:(h + 1) * B_V_DIM] = (
                _rmsnorm_f32(d, sub_ref[...]) * out_scale).astype(o_ref.dtype)


def diff_attn_prompt(q, k, v, lam, score_bound, subln, lam_init, *, batch, seq, heads, tq, hb,
                     fixed_shift):
    nq = seq // tq
    tri = [(qi, ki) for qi in range(nq) for ki in range(qi + 1)]
    qt_tab = jnp.asarray([a for a, _ in tri], jnp.int32)
    kt_tab = jnp.asarray([b for _, b in tri], jnp.int32)
    hd = 2 * B_HEAD_DIM
    grid_spec = pltpu.PrefetchScalarGridSpec(
        num_scalar_prefetch=2,
        grid=(batch, heads // hb, len(tri)),
        in_specs=[pl.BlockSpec((tq, hb * hd), lambda b, h, t, qt, kt: (b * nq + qt[t], h)),
                  pl.BlockSpec((tq, hb * hd), lambda b, h, t, qt, kt: (b * nq + kt[t], h)),
                  pl.BlockSpec((tq, hb * B_V_DIM), lambda b, h, t, qt, kt: (b * nq + kt[t], h)),
                  pl.BlockSpec(memory_space=pltpu.SMEM),
                  pl.BlockSpec((1, B_V_DIM), lambda b, h, t, qt, kt: (0, 0))],
        out_specs=pl.BlockSpec((tq, hb * B_V_DIM), lambda b, h, t, qt, kt: (b * nq + qt[t], h)),
        scratch_shapes=[pltpu.VMEM((hb, 2 * tq, hd), F32),
                        pltpu.VMEM((hb, 2 * tq, LANES), F32),
                        pltpu.VMEM((hb, 2 * tq, LANES), F32),
                        pltpu.VMEM((hb, 2 * tq, B_V_DIM), F32)],
    )
    return pl.pallas_call(
        functools.partial(_diff_attn_body, tq=tq, hb=hb, out_scale=1.0 - lam_init,
                          fixed_shift=fixed_shift),
        grid_spec=grid_spec,
        out_shape=jax.ShapeDtypeStruct((batch * seq, heads * B_V_DIM), BF16),
        compiler_params=_cparams(3),
        name="diff_attn_prompt_fixed_shift" if fixed_shift else "diff_attn_prompt",
    )(qt_tab, kt_tab, q, k, v, jnp.stack([lam, score_bound]).astype(F32),
      subln.reshape(1, B_V_DIM))


def _decode_attn_body(pt_ref, *refs, pages, heads, n_q, n_groups, out_scale):
    k_refs = refs[:pages]
    v_refs = refs[pages:2 * pages]
    kn_ref, vn_ref, qt_ref, lam_ref, sub_ref, o_ref, m_ref, l_ref, acc_ref = refs[2 * pages:]
    g = pl.program_id(1)
    hg = heads // SUBLANES
    hc = 2 * n_q
    cols = SUBLANES * hc

    @pl.when(g == 0)
    def _():
        m_ref[...] = jnp.full(m_ref.shape, -jnp.inf, F32)
        l_ref[...] = jnp.zeros(l_ref.shape, F32)
        acc_ref[...] = jnp.zeros(acc_ref.shape, F32)

    sub = lax.broadcasted_iota(jnp.int32, (SUBLANES, cols), 0)
    lane = lax.broadcasted_iota(jnp.int32, (SUBLANES, cols), 1)
    own = sub == lane // hc

    def to_column(x8):
        r = jnp.sum(jnp.where(own, x8, 0.0), axis=0, keepdims=True)
        return jnp.broadcast_to(r, (LANES, cols)).T

    def group_rows(ref, j, n_pos):
        x = ref[0:n_pos * heads, :].reshape(n_pos, hg, SUBLANES, 2 * B_HEAD_DIM)
        return x[:, j].reshape(n_pos * SUBLANES, 2 * B_HEAD_DIM)

    def process(page_k_refs, page_v_refs, n_pos, new_tokens):
        for j in range(hg):
            scores = []
            for k_ref in page_k_refs:
                s = jnp.dot(group_rows(k_ref, j, n_pos), qt_ref[j], preferred_element_type=F32)
                s = s.reshape(n_pos, SUBLANES, cols)
                valid = own[None]
                if new_tokens:
                    pos = lax.broadcasted_iota(jnp.int32, s.shape, 0)
                    qry = lax.broadcasted_iota(jnp.int32, s.shape, 2) % n_q
                    valid = valid & (pos <= qry)
                scores.append(jnp.where(valid, s, -jnp.inf))
            m_prev = m_ref[j]
            m_new = m_prev
            for s in scores:
                m_new = jnp.maximum(m_new, jnp.max(s, axis=0))
            m_safe = jnp.where(own, m_new, 0.0)
            alpha = jnp.exp2(m_prev - m_safe)
            l_new = alpha * l_ref[j]
            pv = jnp.zeros((cols, B_V_DIM), F32)
            for s, v_ref in zip(scores, page_v_refs):
                p = jnp.exp2(s - m_safe[None])
                l_new = l_new + jnp.sum(p, axis=0)
                pv = pv + lax.dot_general(p.reshape(n_pos * SUBLANES, cols), group_rows(v_ref, j, n_pos),
                                          (((0,), (0,)), ((), ())), preferred_element_type=F32)
            acc_ref[j] = acc_ref[j] * to_column(alpha) + pv
            l_ref[j] = l_new
            m_ref[j] = m_new

    process(k_refs, v_refs, PAGE_SIZE, False)

    @pl.when(g == n_groups - 1)
    def _():
        process([kn_ref], [vn_ref], n_q, True)
        for j in range(hg):
            o = acc_ref[j] / to_column(l_ref[j])
            o = o.reshape(SUBLANES, 2, n_q, B_V_DIM)
            d = o[:, 0] - lam_ref[0] * o[:, 1]
            d = d * lax.rsqrt(jnp.mean(d * d, axis=-1, keepdims=True) + EPS) * sub_ref[...]
            o_ref[j * SUBLANES:(j + 1) * SUBLANES] = (d * out_scale).astype(o_ref.dtype)


def diff_attn_decode(qt, cache_k, cache_v, page_table, k_new, v_new, lam, subln, lam_init,
                     *, layer, heads, n_q, pages):
    dec_b, n_pages = page_table.shape
    hd = 2 * B_HEAD_DIM
    rows = PAGE_SIZE * heads
    hg = heads // SUBLANES
    cols = SUBLANES * 2 * n_q
    n_groups = n_pages // pages
    page_spec = lambda p_i: pl.BlockSpec(
        (None, None, rows, hd),
        lambda b, g, pt, p_i=p_i: (layer, pt[b, g * pages + p_i], 0, 0))
    grid_spec = pltpu.PrefetchScalarGridSpec(
        num_scalar_prefetch=1,
        grid=(dec_b, n_groups),
        in_specs=([page_spec(p_i) for p_i in range(pages)] * 2
                  + [pl.BlockSpec((None, n_q * heads, hd), lambda b, g, pt: (b, 0, 0)),
                     pl.BlockSpec((None, n_q * heads, hd), lambda b, g, pt: (b, 0, 0)),
                     pl.BlockSpec((None, hg, hd, cols), lambda b, g, pt: (b, 0, 0, 0)),
                     pl.BlockSpec(memory_space=pltpu.SMEM),
                     pl.BlockSpec((1, B_V_DIM), lambda b, g, pt: (0, 0))]),
        out_specs=pl.BlockSpec((None, heads, n_q, B_V_DIM), lambda b, g, pt: (b, 0, 0, 0)),
        scratch_shapes=[pltpu.VMEM((hg, SUBLANES, cols), F32),
                        pltpu.VMEM((hg, SUBLANES, cols), F32),
                        pltpu.VMEM((hg, cols, B_V_DIM), F32)],
    )
    return pl.pallas_call(
        functools.partial(_decode_attn_body, pages=pages, heads=heads, n_q=n_q,
                          n_groups=n_groups, out_scale=1.0 - lam_init),
        grid_spec=grid_spec,
        out_shape=jax.ShapeDtypeStruct((dec_b, heads, n_q, B_V_DIM), F32),
        compiler_params=_cparams(2),
        name="diff_attn_decode",
    )(page_table, *([cache_k] * pages), *([cache_v] * pages), k_new, v_new, qt,
      lam.reshape(1), subln.reshape(1, B_V_DIM))


def _cumsum_rows(x):
    c = x.shape[0]
    sub = lax.broadcasted_iota(jnp.int32, x.shape, 0) % HGRN_SUB
    d = 1
    while d < HGRN_SUB:
        x = x + jnp.where(sub >= d, pltpu.roll(x, d, axis=0), 0.0)
        d *= 2
    blocks = []
    carry = None
    for j in range(c // HGRN_SUB):
        blk = x[j * HGRN_SUB:(j + 1) * HGRN_SUB]
        if carry is not None:
            blk = blk + carry
        blocks.append(blk)
        carry = blk[HGRN_SUB - 1:HGRN_SUB]
    return jnp.concatenate(blocks, axis=0) if len(blocks) > 1 else blocks[0]


def _hgrn_chunk(q, k, lf, v, st, row_scr):
    c = q.shape[0]
    nb = c // HGRN_SUB
    gcum = _cumsum_rows(lf) * LOG2_E
    row_scr[0] = gcum
    row_scr[1] = k
    row_scr[2] = v
    o = lax.dot_general(q * jnp.exp2(gcum), st, (((1,), (1,)), ((), ())), preferred_element_type=F32)

    if nb > 1:
        row = lax.broadcasted_iota(jnp.int32, (c, C_HEAD_DIM), 0)
        t_idx = lax.broadcasted_iota(jnp.int32, (c, c), 0)
        s_idx = lax.broadcasted_iota(jnp.int32, (c, c), 1)
        a_off = None
        size = 2 * HGRN_SUB
        while size <= c:
            half = size // 2
            if size < c:
                g_mid = jnp.concatenate(
                    [jnp.broadcast_to(gcum[b0 + half - 1:b0 + half], (size, C_HEAD_DIM))
                     for b0 in range(0, c, size)], axis=0)
            else:
                g_mid = gcum[half - 1:half]
            upper = (row % size) >= half
            qd = q * jnp.exp2(jnp.where(upper, gcum - g_mid, -jnp.inf))
            kd = k * jnp.exp2(jnp.where(upper, -jnp.inf, g_mid - gcum))
            a = lax.dot_general(qd, kd, (((1,), (1,)), ((), ())), preferred_element_type=F32)
            if size < c:
                a = jnp.where((t_idx // size) == (s_idx // size), a, 0.0)
            a_off = a if a_off is None else a_off + a
            size *= 2
        o = o + jnp.dot(a_off, v, preferred_element_type=F32)

    sub_row = lax.broadcasted_iota(jnp.int32, (HGRN_SUB, C_HEAD_DIM), 0)
    o_blocks = []
    for i in range(nb):
        r0 = i * HGRN_SUB
        gi = gcum[r0:r0 + HGRN_SUB]
        qi = q[r0:r0 + HGRN_SUB]
        oi = o[r0:r0 + HGRN_SUB]
        for s in range(HGRN_SUB):
            r = r0 + s
            dec = jnp.exp2(jnp.where(sub_row >= s, gi - row_scr[0, r:r + 1, :], -jnp.inf))
            a_col = jnp.sum(qi * row_scr[1, r:r + 1, :] * dec, axis=-1, keepdims=True)
            oi = oi + a_col * row_scr[2, r:r + 1, :]
        o_blocks.append(oi)
    o = jnp.concatenate(o_blocks, axis=0) if nb > 1 else o_blocks[0]
    g_last = gcum[c - 1:c]
    kd = k * jnp.exp2(g_last - gcum)
    if c < C_HEAD_DIM:
        pad = jnp.zeros((C_HEAD_DIM - c, C_HEAD_DIM), F32)
        kd = jnp.concatenate([kd, pad], axis=0)
        v = jnp.concatenate([v, pad], axis=0)
    st_new = st * jnp.exp2(g_last) + jnp.dot(v.T, kd, preferred_element_type=F32)
    return o, st_new


def _hgrn_body(*refs, chunk, n_chunks, n_r, hb, with_state):
    if with_state:
        q_ref, k_ref, lf_ref, v_ref, g_ref, gg_ref, s0_ref, o_ref, so_ref, st_ref, gs_ref = refs
    else:
        q_ref, k_ref, lf_ref, v_ref, g_ref, gg_ref, o_ref, so_ref, st_ref, gs_ref = refs
    r = pl.program_id(2)
    hd = C_HEAD_DIM

    @pl.when(r == 0)
    def _():
        for h in range(hb):
            if with_state:
                st_ref[h] = s0_ref[h].astype(F32).T
            else:
                st_ref[h] = jnp.zeros((hd, hd), F32)

    def step(ci, carry):
        base = pl.multiple_of(ci * chunk, chunk)
        sl = pl.ds(base, chunk)
        for h in range(hb):
            cs = slice(h * hd, (h + 1) * hd)
            o, st_new = _hgrn_chunk(q_ref[sl, cs], k_ref[sl, cs], lf_ref[sl, cs], v_ref[sl, cs],
                                    st_ref[h], gs_ref.at[h])
            st_ref[h] = st_new
            gate = g_ref[sl, cs]
            o = _rmsnorm_f32(o, gg_ref[...]) * (gate * _sigmoid(gate))
            o_ref[sl, cs] = o.astype(o_ref.dtype)
        return carry

    lax.fori_loop(0, n_chunks, step, 0)

    @pl.when(r == n_r - 1)
    def _():
        for h in range(hb):
            so_ref[h] = st_ref[h].T.astype(so_ref.dtype)


def hgrn_recurrence(q, k, lf, v, g, g_gain, state0, *, batch, seq, heads, rows, chunk, hb, out_dtype):
    n_r = seq // rows
    hd = C_HEAD_DIM
    with_state = state0 is not None
    row_spec = pl.BlockSpec((rows, hb * hd), lambda b, h, r: (b * n_r + r, h))
    state_spec = pl.BlockSpec((None, hb, hd, hd), lambda b, h, r: (b, h, 0, 0))
    in_specs = [row_spec] * 5 + [pl.BlockSpec((1, hd), lambda b, h, r: (0, 0))]
    args = [q, k, lf, v, g, g_gain.reshape(1, hd)]
    if with_state:
        in_specs.append(state_spec)
        args.append(state0)
    return pl.pallas_call(
        functools.partial(_hgrn_body, chunk=chunk, n_chunks=rows // chunk, n_r=n_r, hb=hb,
                          with_state=with_state),
        grid=(batch, heads // hb, n_r),
        in_specs=in_specs,
        out_specs=[row_spec, state_spec],
        out_shape=[jax.ShapeDtypeStruct((batch * seq, heads * hd), out_dtype),
                   jax.ShapeDtypeStruct((batch, heads, hd, hd), F32)],
        scratch_shapes=[pltpu.VMEM((hb, hd, hd), F32), pltpu.VMEM((hb, 3, chunk, hd), F32)],
        compiler_params=_cparams(3),
        name="hgrn_recurrence",
    )(*args)


def _tiles(m):
    big = m >= 1024
    return dict(
        norm_tm=512 if big else m,
        proj_tm=1024 if big else m,
        proj_tn=256,
        a_proj_tn=1024,
        out_tm=512 if big else m,
        ffn_tm=1024 if big else m,
        ffn_th=512,
        spatial_tm=512 if big else m,
        attn_tq=512,
        hgrn_rows=1024,
        attn_heads=8,
        attn_heads_fixed_shift=16,
        hgrn_heads=8,
    )


def _mixer_a_core(uv, v_gain, w_s, b_s, *, chunk_len, n_seq, tiles, with_vn):
    causal = jnp.tril(jnp.ones((A_CHUNK, A_CHUNK), bool))
    w_masked = jnp.where(causal[None], w_s, 0.0)
    if chunk_len == A_CHUNK:
        wm, bs, chunk = w_masked, b_s, A_CHUNK
    else:
        eye = jnp.eye(n_seq, dtype=w_s.dtype)
        small = w_masked[:, :chunk_len, :chunk_len]
        wm = jnp.einsum("ab,gts->gatbs", eye, small).reshape(
            A_GROUPS, n_seq * chunk_len, n_seq * chunk_len)
        bs = jnp.tile(b_s[:, :chunk_len], (1, n_seq))
        chunk = n_seq * chunk_len
    p, vn = spatial_mix(uv, wm.astype(BF16), bs[:, :, None], v_gain, chunk=chunk,
                        tm=max(tiles["spatial_tm"], chunk) if chunk_len == A_CHUNK else chunk,
                        with_vn=with_vn)
    return p, vn


def _mixer_a_proj(xn, xns, w_in, layer, *, tiles):
    width2 = w_in.shape[2]
    (uv,), (uv_s,) = seg_matmul(xn, xns, w_in, layer, (0,), width2, _gelu_epilogue, (BF16,),
                                tm=tiles["proj_tm"], tn=tiles["a_proj_tn"], name="a_in_proj")
    return uv, uv_s


def _mixer_b_proj(xn, xns, w_in, layer, q_gain, k_gain, *, heads, tiles):
    width = heads * 2 * B_HEAD_DIM
    tn = tiles["proj_tn"]
    lane_group = np.arange(tn) // B_HEAD_DIM
    group_ones = jnp.asarray(lane_group[:, None] == lane_group[None, :], BF16)
    reps = width // B_HEAD_DIM
    gq = jnp.tile(q_gain.astype(F32), reps).reshape(1, width)
    gk = jnp.tile(k_gain.astype(F32), reps).reshape(1, width)
    return seg_matmul(xn, xns, w_in, layer, (0, width, 2 * width), width, _headnorm_epilogue,
                      (F32, F32, F32), vecs=(gq, gk), consts=(group_ones,),
                      tm=tiles["proj_tm"], tn=tn, name="b_in_proj")


def _mixer_c_proj(xn, xns, w_in, layer, lower_bound, *, tiles):
    width = w_in.shape[2] // 4
    return seg_matmul(xn, xns, w_in, layer, (0, width, 2 * width, 3 * width), width, _hgrn_gate_epilogue,
                      (F32,) * 5, vecs=(lower_bound.reshape(1, width),),
                      tm=tiles["proj_tm"], tn=tiles["proj_tn"], name="c_in_proj")


def kernel(x_prompt, x_sample, cache_k, cache_v, page_table, state_hgrn, norm_mix, norm_ffn, ffn_w_gu, ffn_w_down, a_w_in, a_v_norm, a_w_s, a_b_s, a_w_out, b_w_in, b_q_norm, b_k_norm, b_lambda_q1, b_lambda_k1, b_lambda_q2, b_lambda_k2, b_subln, b_w_out, c_w_in, c_g_norm, c_lower_bounds, c_w_out):
    batch, seq, d_model = x_prompt.shape
    dec_b, dec_seq, _ = x_sample.shape
    depth = norm_mix.shape[0]
    b_heads = d_model // (2 * B_HEAD_DIM)
    c_heads = d_model // C_HEAD_DIM
    mp, ms = batch * seq, dec_b * dec_seq
    tp, ts = _tiles(mp), _tiles(ms)

    probs = jax.nn.softmax(c_lower_bounds.astype(F32), axis=0)
    lower_bound = jnp.cumsum(probs, axis=0) - probs[0]

    h_p = x_prompt.reshape(mp, d_model)
    h_s = x_sample.reshape(ms, d_model)
    xn_p = norm_rows(h_p, norm_mix[0], tm=tp["norm_tm"])
    xn_s = norm_rows(h_s, norm_mix[0], tm=ts["norm_tm"])

    n_phys = cache_k.shape[1]
    cache_k2 = cache_k.reshape(cache_k.shape[0], n_phys, PAGE_SIZE * b_heads, 2 * B_HEAD_DIM)
    cache_v2 = cache_v.reshape(cache_v.shape[0], n_phys, PAGE_SIZE * b_heads, B_V_DIM)

    k_p_rows, v_p_rows, k_s_rows, v_s_rows = [], [], [], []
    hgrn_p, hgrn_s, chunk_v_s = [], [], []
    for i in range(depth):
        kind, j = i % 3, i // 3
        if kind == 0:
            uv_p, uv_s = _mixer_a_proj(xn_p, xn_s, a_w_in, j, tiles=tp)
            y_p, _ = _mixer_a_core(uv_p, a_v_norm[j], a_w_s[j], a_b_s[j],
                                   chunk_len=A_CHUNK, n_seq=batch, tiles=tp, with_vn=False)
            y_s, vn_s = _mixer_a_core(uv_s, a_v_norm[j], a_w_s[j], a_b_s[j],
                                      chunk_len=dec_seq, n_seq=dec_b, tiles=ts, with_vn=True)
            chunk_v_s.append(vn_s.reshape(dec_b, dec_seq, -1))
            w_out = a_w_out
        elif kind == 1:
            lam_init = 0.8 - 0.6 * math.exp(-0.3 * i)
            lam = (jnp.exp(jnp.sum(b_lambda_q1[j].astype(F32) * b_lambda_k1[j].astype(F32)))
                   - jnp.exp(jnp.sum(b_lambda_q2[j].astype(F32) * b_lambda_k2[j].astype(F32)))
                   + lam_init)
            (q_p, k_p, v_p), (q_s, k_s, v_s) = _mixer_b_proj(
                xn_p, xn_s, b_w_in, j, b_q_norm[j], b_k_norm[j], heads=b_heads, tiles=tp)
            score_bound = (B_HEAD_DIM ** 0.5) * jnp.max(jnp.abs(b_q_norm[j].astype(F32))) * jnp.max(
                jnp.abs(b_k_norm[j].astype(F32)))
            attn = functools.partial(diff_attn_prompt, batch=batch, seq=seq, heads=b_heads,
                                     tq=tp["attn_tq"])
            y_p = lax.cond(
                score_bound <= ATTN_FIXED_SHIFT_MAX_BOUND,
                lambda *a: attn(*a, lam_init, hb=min(b_heads, tp["attn_heads_fixed_shift"]),
                                fixed_shift=True),
                lambda *a: attn(*a, lam_init, hb=min(b_heads, tp["attn_heads"]), fixed_shift=False),
                q_p, k_p, v_p, lam, score_bound, b_subln[j])
            k_p_rows.append(k_p.reshape(batch, seq, b_heads, 2 * B_HEAD_DIM))
            v_p_rows.append(v_p.reshape(batch, seq, b_heads, B_V_DIM))

            q5 = (q_s * (B_HEAD_DIM ** -0.5 * LOG2_E)).reshape(dec_b, dec_seq, b_heads, 2, B_HEAD_DIM)
            qt = jnp.einsum("bthcd,ce->bhcted", q5, jnp.eye(2, dtype=F32))
            qt = qt.reshape(dec_b, b_heads // SUBLANES, SUBLANES * 2 * dec_seq, 2 * B_HEAD_DIM)
            qt = qt.transpose(0, 1, 3, 2)
            k_new = k_s.reshape(dec_b, dec_seq * b_heads, 2 * B_HEAD_DIM)
            v_new = v_s.reshape(dec_b, dec_seq * b_heads, B_V_DIM)
            o_s = diff_attn_decode(qt, cache_k2, cache_v2, page_table, k_new, v_new, lam,
                                   b_subln[j], lam_init, layer=j, heads=b_heads, n_q=dec_seq,
                                   pages=DECODE_PAGES_PER_STEP)
            y_s = o_s.transpose(0, 2, 1, 3).reshape(ms, d_model)
            k_s_rows.append(k_s.reshape(dec_b, dec_seq, b_heads, 2 * B_HEAD_DIM))
            v_s_rows.append(v_s.reshape(dec_b, dec_seq, b_heads, B_V_DIM))
            w_out = b_w_out
        else:
            qkv_p, qkv_s = _mixer_c_proj(xn_p, xn_s, c_w_in, j, lower_bound[i], tiles=tp)
            y_p, st_p = hgrn_recurrence(*qkv_p, c_g_norm[j], None, batch=batch, seq=seq,
                                        heads=c_heads, rows=tp["hgrn_rows"], chunk=C_HEAD_DIM,
                                        hb=tp["hgrn_heads"], out_dtype=BF16)
            y_s, st_s = hgrn_recurrence(*qkv_s, c_g_norm[j], state_hgrn[j], batch=dec_b,
                                        seq=dec_seq, heads=c_heads, rows=dec_seq, chunk=dec_seq,
                                        hb=ts["hgrn_heads"], out_dtype=F32)
            hgrn_p.append(st_p)
            hgrn_s.append(st_s)
            w_out = c_w_out
        h_p, xf_p, h_s, xf_s = out_proj(y_p, y_s, w_out, j, h_p, h_s, norm_ffn[i], tm=tp["out_tm"])
        gain_next = norm_mix[i + 1] if i + 1 < depth else None
        h_p, h_s, xn_p, xn_s = ffn(xf_p, h_p, xf_s, h_s, ffn_w_gu, ffn_w_down, i, gain_next,
                                   tm=tp["ffn_tm"], th=math.gcd(ffn_w_down.shape[1], tp["ffn_th"]))
    return (h_p.reshape(batch, seq, d_model), h_s.reshape(dec_b, dec_seq, d_model),
            jnp.stack(k_p_rows), jnp.stack(v_p_rows), jnp.stack(k_s_rows), jnp.stack(v_s_rows),
            jnp.stack(hgrn_p), jnp.stack(hgrn_s), jnp.stack(chunk_v_s))
```

```python
import functools
import math

import jax
import jax.numpy as jnp
import numpy as np
from jax import lax
from jax.experimental import pallas as pl
from jax.experimental.pallas import tpu as pltpu

F32 = jnp.float32
BF16 = jnp.bfloat16
EPS = 1e-6
LOG2_E = 1.4426950408889634

LANES = 128
SUBLANES = 8
VMEM_LIMIT_BYTES = 56 << 20
FFN_VMEM_LIMIT_BYTES = 63 << 20
FFN_RESIDUAL_SLICES = 8

A_CHUNK = 128
A_GROUPS = 8
B_HEAD_DIM = 64
B_V_DIM = 2 * B_HEAD_DIM
C_HEAD_DIM = 128
PAGE_SIZE = 128
HGRN_SUB = SUBLANES
DECODE_PAGES_PER_STEP = 8
ATTN_FIXED_SHIFT_MAX_BOUND = 30.0


def _cparams(n_axes):
    return pltpu.CompilerParams(
        dimension_semantics=("arbitrary",) * n_axes,
        vmem_limit_bytes=VMEM_LIMIT_BYTES,
    )


def _rmsnorm_f32(x, gain):
    return x * lax.rsqrt(jnp.mean(x * x, axis=-1, keepdims=True) + EPS) * gain


def _sigmoid(x):
    return 1.0 / (1.0 + jnp.exp(-x))


def _norm_rows_body(x_ref, g_ref, o_ref):
    o_ref[...] = _rmsnorm_f32(x_ref[...], g_ref[...]).astype(o_ref.dtype)


def norm_rows(x, gain, *, tm):
    m, d = x.shape
    return pl.pallas_call(
        _norm_rows_body,
        grid=(m // tm,),
        in_specs=[pl.BlockSpec((tm, d), lambda i: (i, 0)),
                  pl.BlockSpec((1, d), lambda i: (0, 0))],
        out_specs=pl.BlockSpec((tm, d), lambda i: (i, 0)),
        out_shape=jax.ShapeDtypeStruct((m, d), BF16),
        compiler_params=_cparams(1),
        name="norm_rows",
    )(x, gain.reshape(1, d))


def _seg_matmul_body(*refs, ns, nv, nc, no, epilogue):
    x_ref, xs_ref = refs[0:2]
    w_refs = refs[2:2 + ns]
    vec_refs = refs[2 + ns:2 + ns + nv]
    const_refs = refs[2 + ns + nv:2 + ns + nv + nc]
    out_refs = refs[2 + ns + nv + nc:2 + ns + nv + nc + no]
    sample_out_refs = refs[2 + ns + nv + nc + no:-1]
    wb_ref = refs[-1]
    first_row_tile = pl.program_id(1) == 0

    @pl.when(first_row_tile)
    def _():
        for s in range(ns):
            wb_ref[s] = w_refs[s][...].astype(BF16)

    def project(rows_ref, dst_refs):
        x = rows_ref[...]
        accs = [jnp.dot(x, wb_ref[s], preferred_element_type=F32) for s in range(ns)]
        outs = epilogue(accs, [r[...] for r in vec_refs], [r[...] for r in const_refs])
        for r, o in zip(dst_refs, outs):
            r[...] = o.astype(r.dtype)

    project(x_ref, out_refs)

    @pl.when(first_row_tile)
    def _():
        project(xs_ref, sample_out_refs)


def seg_matmul(x, xs, w, layer, seg_starts, seg_width, epilogue, out_dtypes, vecs=(), consts=(),
               *, tm, tn, name):
    m, k = x.shape
    ms = xs.shape[0]
    ns = len(seg_starts)
    no = len(out_dtypes)
    in_specs = [pl.BlockSpec((tm, k), lambda j, i: (i, 0)),
                pl.BlockSpec((ms, k), lambda j, i: (0, 0))]
    for st in seg_starts:
        in_specs.append(pl.BlockSpec((None, k, tn), lambda j, i, off=st // tn: (layer, 0, off + j)))
    for _ in vecs:
        in_specs.append(pl.BlockSpec((1, tn), lambda j, i: (0, j)))
    for c in consts:
        in_specs.append(pl.BlockSpec(c.shape, lambda j, i, nd=c.ndim: (0,) * nd))
    res = pl.pallas_call(
        functools.partial(_seg_matmul_body, ns=ns, nv=len(vecs), nc=len(consts), no=no,
                          epilogue=epilogue),
        grid=(seg_width // tn, m // tm),
        in_specs=in_specs,
        out_specs=([pl.BlockSpec((tm, tn), lambda j, i: (i, j)) for _ in out_dtypes]
                   + [pl.BlockSpec((ms, tn), lambda j, i: (0, j)) for _ in out_dtypes]),
        out_shape=([jax.ShapeDtypeStruct((m, seg_width), dt) for dt in out_dtypes]
                   + [jax.ShapeDtypeStruct((ms, seg_width), dt) for dt in out_dtypes]),
        scratch_shapes=[pltpu.VMEM((ns, k, tn), BF16)],
        compiler_params=_cparams(2),
        name=name,
    )(x, xs, *([w] * ns), *vecs, *consts)
    return res[:no], res[no:]


def _gelu_exact_f32(a):
    z = a * (2.0 ** -0.5)
    az = jnp.abs(z)
    t = 1.0 / (1.0 + 0.3275911 * az)
    poly = t * (0.254829592 + t * (-0.284496736 + t * (1.421413741
                                                       + t * (-1.453152027 + t * 1.061405429))))
    erfc_abs = poly * jnp.exp(-az * az)
    return 0.5 * a * jnp.where(z >= 0, 2.0 - erfc_abs, erfc_abs)


def _gelu_epilogue(accs, vecs, consts):
    (a,) = accs
    return [_gelu_exact_f32(a)]


def _headnorm_epilogue(accs, vecs, consts):
    aq, ak, av = accs
    gq, gk = vecs
    (group_ones,) = consts

    def head_norm(a, g):
        ms = jnp.dot((a * a).astype(BF16), group_ones, preferred_element_type=F32) * (1.0 / B_HEAD_DIM)
        return a * lax.rsqrt(ms + EPS) * g

    return [head_norm(aq, gq), head_norm(ak, gk), av]


def _hgrn_gate_epilogue(accs, vecs, consts):
    aq, af, av, ag = accs
    (lb,) = vecs
    q = aq * _sigmoid(aq)
    sig = _sigmoid(af)
    f = lb + (1.0 - lb) * sig
    k = (1.0 - lb) * (1.0 - sig)
    return [q, k, jnp.log(f), av, ag]


def _out_proj_body(y_ref, ys_ref, w_ref, h_ref, hs_ref, g_ref, ho_ref, xo_ref, hso_ref, xso_ref,
                   wb_ref, *, cast_rows):
    first_row_tile = pl.program_id(0) == 0

    @pl.when(first_row_tile)
    def _():
        def cast(r, carry):
            sl = pl.ds(pl.multiple_of(r * cast_rows, cast_rows), cast_rows)
            wb_ref[sl, :] = w_ref[sl, :].astype(BF16)
            return carry
        lax.fori_loop(0, w_ref.shape[0] // cast_rows, cast, 0)

    def project(rows_ref, res_ref, h_out_ref, x_out_ref):
        hn = res_ref[...] + jnp.dot(rows_ref[...].astype(BF16), wb_ref[...],
                                    preferred_element_type=F32)
        h_out_ref[...] = hn
        x_out_ref[...] = _rmsnorm_f32(hn, g_ref[...]).astype(x_out_ref.dtype)

    project(y_ref, h_ref, ho_ref, xo_ref)

    @pl.when(first_row_tile)
    def _():
        project(ys_ref, hs_ref, hso_ref, xso_ref)


def out_proj(y, ys, w, layer, h, hs, gain_next, *, tm):
    m, k = y.shape
    ms = ys.shape[0]
    n = w.shape[2]
    row = lambda cols: pl.BlockSpec((tm, cols), lambda i: (i, 0))
    sample = lambda cols: pl.BlockSpec((ms, cols), lambda i: (0, 0))
    return pl.pallas_call(
        functools.partial(_out_proj_body, cast_rows=256),
        grid=(m // tm,),
        in_specs=[row(k), sample(k),
                  pl.BlockSpec((None, k, n), lambda i: (layer, 0, 0), pipeline_mode=pl.Buffered(1)),
                  row(n), sample(n),
                  pl.BlockSpec((1, n), lambda i: (0, 0))],
        out_specs=[row(n), row(n), sample(n), sample(n)],
        out_shape=[jax.ShapeDtypeStruct((m, n), F32), jax.ShapeDtypeStruct((m, n), BF16),
                   jax.ShapeDtypeStruct((ms, n), F32), jax.ShapeDtypeStruct((ms, n), BF16)],
        scratch_shapes=[pltpu.VMEM((k, n), BF16)],
        compiler_params=_cparams(1),
        name="out_proj",
    )(y, ys, w, h, hs, gain_next.reshape(1, n))


def _ffn_body(x_ref, h_ref, xs_ref, hs_ref, wg_ref, wu_ref, wd_ref, g_ref, *out_refs, n_t, with_norm):
    if with_norm:
        ho_ref, hso_ref, xo_ref, xso_ref = out_refs
    else:
        ho_ref, hso_ref = out_refs
    i = pl.program_id(0)
    t = pl.program_id(1)

    @pl.when(t == 0)
    def _():
        ho_ref[...] = jnp.zeros(ho_ref.shape, F32)

    slice_rows = h_ref.shape[0]
    for p in range(FFN_RESIDUAL_SLICES):
        @pl.when(t == p)
        def _(p=p):
            ho_ref[p * slice_rows:(p + 1) * slice_rows, :] += h_ref[...]

    def swiglu(x):
        gate = jnp.dot(x, wg_ref[...].astype(BF16), preferred_element_type=F32)
        up = jnp.dot(x, wu_ref[...].astype(BF16), preferred_element_type=F32)
        act = (gate * _sigmoid(gate) * up).astype(BF16)
        return jnp.dot(act, wd_ref[...].astype(BF16), preferred_element_type=F32)

    ho_ref[...] += swiglu(x_ref[...])

    @pl.when((i == 0) & (t == 0))
    def _():
        hso_ref[...] = hs_ref[...]

    @pl.when(i == 0)
    def _():
        hso_ref[...] += swiglu(xs_ref[...])

    if with_norm:
        @pl.when(t == n_t - 1)
        def _():
            xo_ref[...] = _rmsnorm_f32(ho_ref[...], g_ref[...]).astype(BF16)

        @pl.when((i == 0) & (t == n_t - 1))
        def _():
            xso_ref[...] = _rmsnorm_f32(hso_ref[...], g_ref[...]).astype(BF16)


def ffn(x, h, xs, hs, w_gu, w_down, layer, gain_next, *, tm, th):
    m, d = x.shape
    ms = xs.shape[0]
    hidden = w_down.shape[1]
    n_t = hidden // th
    with_norm = gain_next is not None
    gain = gain_next if with_norm else jnp.ones((d,), F32)
    row_out = pl.BlockSpec((tm, d), lambda i, t: (i, 0))
    sample_block = pl.BlockSpec((ms, d), lambda i, t: (0, 0))
    out_specs = [row_out, sample_block]
    out_shape = [jax.ShapeDtypeStruct((m, d), F32), jax.ShapeDtypeStruct((ms, d), F32)]
    if with_norm:
        out_specs += [row_out, sample_block]
        out_shape += [jax.ShapeDtypeStruct((m, d), BF16), jax.ShapeDtypeStruct((ms, d), BF16)]
    assert n_t >= FFN_RESIDUAL_SLICES and tm % (FFN_RESIDUAL_SLICES * SUBLANES) == 0
    row_block = pl.BlockSpec((tm, d), lambda i, t: (i, 0), pipeline_mode=pl.Buffered(1))
    last_slice = FFN_RESIDUAL_SLICES - 1
    residual_slice = pl.BlockSpec(
        (tm // FFN_RESIDUAL_SLICES, d),
        lambda i, t: (i * FFN_RESIDUAL_SLICES + jnp.minimum(t, last_slice), 0))
    res = pl.pallas_call(
        functools.partial(_ffn_body, n_t=n_t, with_norm=with_norm),
        grid=(m // tm, n_t),
        in_specs=[row_block, residual_slice, sample_block, sample_block,
                  pl.BlockSpec((None, d, th), lambda i, t: (layer, 0, t)),
                  pl.BlockSpec((None, d, th), lambda i, t: (layer, 0, n_t + t)),
                  pl.BlockSpec((None, th, d), lambda i, t: (layer, t, 0)),
                  pl.BlockSpec((1, d), lambda i, t: (0, 0))],
        out_specs=out_specs,
        out_shape=out_shape,
        compiler_params=pltpu.CompilerParams(dimension_semantics=("arbitrary",) * 2,
                                             vmem_limit_bytes=FFN_VMEM_LIMIT_BYTES),
        name="ffn",
    )(x, h, xs, hs, w_gu, w_gu, w_down, gain.reshape(1, d))
    return tuple(res) if with_norm else (res[0], res[1], None, None)


def _spatial_body(u_ref, v_ref, wm_ref, bs_ref, vg_ref, p_ref, *maybe_vn_ref, chunk, groups):
    v = v_ref[...].astype(F32)
    vn = _rmsnorm_f32(v, vg_ref[...])
    if maybe_vn_ref:
        maybe_vn_ref[0][...] = vn
    vnb = vn.astype(BF16)
    rows, width = v.shape
    gw = width // groups
    for c in range(rows // chunk):
        r0 = c * chunk
        for g in range(groups):
            c0 = g * gw
            s = jnp.dot(wm_ref[g], vnb[r0:r0 + chunk, c0:c0 + gw], preferred_element_type=F32)
            s = s + bs_ref[g]
            u = u_ref[r0:r0 + chunk, c0:c0 + gw].astype(F32)
            p_ref[r0:r0 + chunk, c0:c0 + gw] = (u * s).astype(p_ref.dtype)


def spatial_mix(uv, wm, bs, v_gain, *, chunk, tm, with_vn):
    m, w2 = uv.shape
    width = w2 // 2
    groups = wm.shape[0]
    row_out = pl.BlockSpec((tm, width), lambda i: (i, 0))
    out_specs = [row_out]
    out_shape = [jax.ShapeDtypeStruct((m, width), BF16)]
    if with_vn:
        out_specs.append(row_out)
        out_shape.append(jax.ShapeDtypeStruct((m, width), F32))
    res = pl.pallas_call(
        functools.partial(_spatial_body, chunk=chunk, groups=groups),
        grid=(m // tm,),
        in_specs=[pl.BlockSpec((tm, width), lambda i: (i, 0)),
                  pl.BlockSpec((tm, width), lambda i: (i, 1)),
                  pl.BlockSpec(wm.shape, lambda i: (0, 0, 0)),
                  pl.BlockSpec(bs.shape, lambda i: (0, 0, 0)),
                  pl.BlockSpec((1, width), lambda i: (0, 0))],
        out_specs=out_specs,
        out_shape=out_shape,
        compiler_params=_cparams(1),
        name="spatial_mix",
    )(uv, uv, wm, bs, v_gain.reshape(1, width))
    return (res[0], res[1]) if with_vn else (res[0], None)


def _diff_attn_body(qt_tab, kt_tab, q_ref, k_ref, v_ref, sc_ref, sub_ref, o_ref,
                    qs_ref, m_ref, l_ref, acc_ref, *, tq, hb, out_scale, fixed_shift):
    t = pl.program_id(2)
    qi = qt_tab[t]
    ki = kt_tab[t]
    hd = 2 * B_HEAD_DIM

    @pl.when(ki == 0)
    def _():
        for h in range(hb):
            q = q_ref[:, h * hd:(h + 1) * hd].astype(F32) * (B_HEAD_DIM ** -0.5)
            lane = lax.broadcasted_iota(jnp.int32, q.shape, 1)
            qs_ref[h, 0:tq, :] = jnp.where(lane < B_HEAD_DIM, q, 0.0)
            qs_ref[h, tq:2 * tq, :] = jnp.where(lane >= B_HEAD_DIM, q, 0.0)
        if not fixed_shift:
            m_ref[...] = jnp.full(m_ref.shape, -jnp.inf, F32)
        l_ref[...] = jnp.zeros(l_ref.shape, F32)
        acc_ref[...] = jnp.zeros(acc_ref.shape, F32)

    def strip(h, r0, masked):
        rs = pl.ds(r0, tq)
        keys = k_ref[:, h * hd:(h + 1) * hd]
        vals = v_ref[:, h * B_V_DIM:(h + 1) * B_V_DIM]
        s = lax.dot_general(qs_ref[h, rs, :], keys, (((1,), (1,)), ((), ())),
                            preferred_element_type=F32)
        if masked:
            row = lax.broadcasted_iota(jnp.int32, s.shape, 0)
            col = lax.broadcasted_iota(jnp.int32, s.shape, 1)
            s = jnp.where(col <= row, s, -jnp.inf)
        if fixed_shift:
            p = jnp.exp(s - sc_ref[1])
            part = p[:, 0:LANES]
            for c0 in range(LANES, p.shape[1], LANES):
                part = part + p[:, c0:c0 + LANES]
            l_ref[h, rs, :] += part
            acc_ref[h, rs, :] += jnp.dot(p, vals, preferred_element_type=F32)
        else:
            m_prev = m_ref[h, rs, :]
            m_new = jnp.maximum(m_prev, jnp.max(s, axis=1, keepdims=True))
            alpha = jnp.exp(m_prev - m_new)
            p = jnp.exp(s - m_new[:, 0:1])
            l_ref[h, rs, :] = alpha * l_ref[h, rs, :] + jnp.sum(p, axis=1, keepdims=True)
            acc_ref[h, rs, :] = alpha * acc_ref[h, rs, :] + jnp.dot(p, vals, preferred_element_type=F32)
            m_ref[h, rs, :] = m_new

    @pl.when(ki < qi)
    def _():
        for h in range(hb):
            strip(h, 0, False)
            strip(h, tq, False)

    @pl.when(ki == qi)
    def _():
        for h in range(hb):
            strip(h, 0, True)
            strip(h, tq, True)
            l = l_ref[h]
            if fixed_shift:
                l = jnp.sum(l, axis=1, keepdims=True)
            o = acc_ref[h] / l
            d = o[0:tq] - sc_ref[0] * o[tq:2 * tq]
            o_ref[:, h * B_V_DIM:(h + 1) * B_V_DIM] = (
                _rmsnorm_f32(d, sub_ref[...]) * out_scale).astype(o_ref.dtype)


def diff_attn_prompt(q, k, v, lam, score_bound, subln, lam_init, *, batch, seq, heads, tq, hb,
                     fixed_shift):
    nq = seq // tq
    tri = [(qi, ki) for qi in range(nq) for ki in range(qi + 1)]
    qt_tab = jnp.asarray([a for a, _ in tri], jnp.int32)
    kt_tab = jnp.asarray([b for _, b in tri], jnp.int32)
    hd = 2 * B_HEAD_DIM
    grid_spec = pltpu.PrefetchScalarGridSpec(
        num_scalar_prefetch=2,
        grid=(batch, heads // hb, len(tri)),
        in_specs=[pl.BlockSpec((tq, hb * hd), lambda b, h, t, qt, kt: (b * nq + qt[t], h)),
                  pl.BlockSpec((tq, hb * hd), lambda b, h, t, qt, kt: (b * nq + kt[t], h)),
                  pl.BlockSpec((tq, hb * B_V_DIM), lambda b, h, t, qt, kt: (b * nq + kt[t], h)),
                  pl.BlockSpec(memory_space=pltpu.SMEM),
                  pl.BlockSpec((1, B_V_DIM), lambda b, h, t, qt, kt: (0, 0))],
        out_specs=pl.BlockSpec((tq, hb * B_V_DIM), lambda b, h, t, qt, kt: (b * nq + qt[t], h)),
        scratch_shapes=[pltpu.VMEM((hb, 2 * tq, hd), F32),
                        pltpu.VMEM((hb, 2 * tq, LANES), F32),
                        pltpu.VMEM((hb, 2 * tq, LANES), F32),
                        pltpu.VMEM((hb, 2 * tq, B_V_DIM), F32)],
    )
    return pl.pallas_call(
        functools.partial(_diff_attn_body, tq=tq, hb=hb, out_scale=1.0 - lam_init,
                          fixed_shift=fixed_shift),
        grid_spec=grid_spec,
        out_shape=jax.ShapeDtypeStruct((batch * seq, heads * B_V_DIM), BF16),
        compiler_params=_cparams(3),
        name="diff_attn_prompt_fixed_shift" if fixed_shift else "diff_attn_prompt",
    )(qt_tab, kt_tab, q, k, v, jnp.stack([lam, score_bound]).astype(F32),
      subln.reshape(1, B_V_DIM))


def _decode_attn_body(pt_ref, *refs, pages, heads, n_q, n_groups, out_scale):
    k_refs = refs[:pages]
    v_refs = refs[pages:2 * pages]
    kn_ref, vn_ref, qt_ref, lam_ref, sub_ref, o_ref, m_ref, l_ref, acc_ref = refs[2 * pages:]
    g = pl.program_id(1)
    hg = heads // SUBLANES
    hc = 2 * n_q
    cols = SUBLANES * hc

    @pl.when(g == 0)
    def _():
        m_ref[...] = jnp.full(m_ref.shape, -jnp.inf, F32)
        l_ref[...] = jnp.zeros(l_ref.shape, F32)
        acc_ref[...] = jnp.zeros(acc_ref.shape, F32)

    sub = lax.broadcasted_iota(jnp.int32, (SUBLANES, cols), 0)
    lane = lax.broadcasted_iota(jnp.int32, (SUBLANES, cols), 1)
    own = sub == lane // hc

    def to_column(x8):
        r = jnp.sum(jnp.where(own, x8, 0.0), axis=0, keepdims=True)
        return jnp.broadcast_to(r, (LANES, cols)).T

    def group_rows(ref, j, n_pos):
        x = ref[0:n_pos * heads, :].reshape(n_pos, hg, SUBLANES, 2 * B_HEAD_DIM)
        return x[:, j].reshape(n_pos * SUBLANES, 2 * B_HEAD_DIM)

    def process(page_k_refs, page_v_refs, n_pos, new_tokens):
        for j in range(hg):
            scores = []
            for k_ref in page_k_refs:
                s = jnp.dot(group_rows(k_ref, j, n_pos), qt_ref[j], preferred_element_type=F32)
                s = s.reshape(n_pos, SUBLANES, cols)
                valid = own[None]
                if new_tokens:
                    pos = lax.broadcasted_iota(jnp.int32, s.shape, 0)
                    qry = lax.broadcasted_iota(jnp.int32, s.shape, 2) % n_q
                    valid = valid & (pos <= qry)
                scores.append(jnp.where(valid, s, -jnp.inf))
            m_prev = m_ref[j]
            m_new = m_prev
            for s in scores:
                m_new = jnp.maximum(m_new, jnp.max(s, axis=0))
            m_safe = jnp.where(own, m_new, 0.0)
            alpha = jnp.exp2(m_prev - m_safe)
            l_new = alpha * l_ref[j]
            pv = jnp.zeros((cols, B_V_DIM), F32)
            for s, v_ref in zip(scores, page_v_refs):
                p = jnp.exp2(s - m_safe[None])
                l_new = l_new + jnp.sum(p, axis=0)
                pv = pv + lax.dot_general(p.reshape(n_pos * SUBLANES, cols), group_rows(v_ref, j, n_pos),
                                          (((0,), (0,)), ((), ())), preferred_element_type=F32)
            acc_ref[j] = acc_ref[j] * to_column(alpha) + pv
            l_ref[j] = l_new
            m_ref[j] = m_new

    process(k_refs, v_refs, PAGE_SIZE, False)

    @pl.when(g == n_groups - 1)
    def _():
        process([kn_ref], [vn_ref], n_q, True)
        for j in range(hg):
            o = acc_ref[j] / to_column(l_ref[j])
            o = o.reshape(SUBLANES, 2, n_q, B_V_DIM)
            d = o[:, 0] - lam_ref[0] * o[:, 1]
            d = d * lax.rsqrt(jnp.mean(d * d, axis=-1, keepdims=True) + EPS) * sub_ref[...]
            o_ref[j * SUBLANES:(j + 1) * SUBLANES] = (d * out_scale).astype(o_ref.dtype)


def diff_attn_decode(qt, cache_k, cache_v, page_table, k_new, v_new, lam, subln, lam_init,
                     *, layer, heads, n_q, pages):
    dec_b, n_pages = page_table.shape
    hd = 2 * B_HEAD_DIM
    rows = PAGE_SIZE * heads
    hg = heads // SUBLANES
    cols = SUBLANES * 2 * n_q
    n_groups = n_pages // pages
    page_spec = lambda p_i: pl.BlockSpec(
        (None, None, rows, hd),
        lambda b, g, pt, p_i=p_i: (layer, pt[b, g * pages + p_i], 0, 0))
    grid_spec = pltpu.PrefetchScalarGridSpec(
        num_scalar_prefetch=1,
        grid=(dec_b, n_groups),
        in_specs=([page_spec(p_i) for p_i in range(pages)] * 2
                  + [pl.BlockSpec((None, n_q * heads, hd), lambda b, g, pt: (b, 0, 0)),
                     pl.BlockSpec((None, n_q * heads, hd), lambda b, g, pt: (b, 0, 0)),
                     pl.BlockSpec((None, hg, hd, cols), lambda b, g, pt: (b, 0, 0, 0)),
                     pl.BlockSpec(memory_space=pltpu.SMEM),
                     pl.BlockSpec((1, B_V_DIM), lambda b, g, pt: (0, 0))]),
        out_specs=pl.BlockSpec((None, heads, n_q, B_V_DIM), lambda b, g, pt: (b, 0, 0, 0)),
        scratch_shapes=[pltpu.VMEM((hg, SUBLANES, cols), F32),
                        pltpu.VMEM((hg, SUBLANES, cols), F32),
                        pltpu.VMEM((hg, cols, B_V_DIM), F32)],
    )
    return pl.pallas_call(
        functools.partial(_decode_attn_body, pages=pages, heads=heads, n_q=n_q,
                          n_groups=n_groups, out_scale=1.0 - lam_init),
        grid_spec=grid_spec,
        out_shape=jax.ShapeDtypeStruct((dec_b, heads, n_q, B_V_DIM), F32),
        compiler_params=_cparams(2),
        name="diff_attn_decode",
    )(page_table, *([cache_k] * pages), *([cache_v] * pages), k_new, v_new, qt,
      lam.reshape(1), subln.reshape(1, B_V_DIM))


def _cumsum_rows(x):
    c = x.shape[0]
    sub = lax.broadcasted_iota(jnp.int32, x.shape, 0) % HGRN_SUB
    d = 1
    while d < HGRN_SUB:
        x = x + jnp.where(sub >= d, pltpu.roll(x, d, axis=0), 0.0)
        d *= 2
    blocks = []
    carry = None
    for j in range(c // HGRN_SUB):
        blk = x[j * HGRN_SUB:(j + 1) * HGRN_SUB]
        if carry is not None:
            blk = blk + carry
        blocks.append(blk)
        carry = blk[HGRN_SUB - 1:HGRN_SUB]
    return jnp.concatenate(blocks, axis=0) if len(blocks) > 1 else blocks[0]


def _hgrn_chunk(q, k, lf, v, st, row_scr):
    c = q.shape[0]
    nb = c // HGRN_SUB
    gcum = _cumsum_rows(lf) * LOG2_E
    row_scr[0] = gcum
    row_scr[1] = k
    row_scr[2] = v
    o = lax.dot_general(q * jnp.exp2(gcum), st, (((1,), (1,)), ((), ())), preferred_element_type=F32)

    if nb > 1:
        row = lax.broadcasted_iota(jnp.int32, (c, C_HEAD_DIM), 0)
        t_idx = lax.broadcasted_iota(jnp.int32, (c, c), 0)
        s_idx = lax.broadcasted_iota(jnp.int32, (c, c), 1)
        a_off = None
        size = 2 * HGRN_SUB
        while size <= c:
            half = size // 2
            if size < c:
                g_mid = jnp.concatenate(
                    [jnp.broadcast_to(gcum[b0 + half - 1:b0 + half], (size, C_HEAD_DIM))
                     for b0 in range(0, c, size)], axis=0)
            else:
                g_mid = gcum[half - 1:half]
            upper = (row % size) >= half
            qd = q * jnp.exp2(jnp.where(upper, gcum - g_mid, -jnp.inf))
            kd = k * jnp.exp2(jnp.where(upper, -jnp.inf, g_mid - gcum))
            a = lax.dot_general(qd, kd, (((1,), (1,)), ((), ())), preferred_element_type=F32)
            if size < c:
                a = jnp.where((t_idx // size) == (s_idx // size), a, 0.0)
            a_off = a if a_off is None else a_off + a
            size *= 2
        o = o + jnp.dot(a_off, v, preferred_element_type=F32)

    sub_row = lax.broadcasted_iota(jnp.int32, (HGRN_SUB, C_HEAD_DIM), 0)
    o_blocks = []
    for i in range(nb):
        r0 = i * HGRN_SUB
        gi = gcum[r0:r0 + HGRN_SUB]
        qi = q[r0:r0 + HGRN_SUB]
        oi = o[r0:r0 + HGRN_SUB]
        for s in range(HGRN_SUB):
            r = r0 + s
            dec = jnp.exp2(jnp.where(sub_row >= s, gi - row_scr[0, r:r + 1, :], -jnp.inf))
            a_col = jnp.sum(qi * row_scr[1, r:r + 1, :] * dec, axis=-1, keepdims=True)
            oi = oi + a_col * row_scr[2, r:r + 1, :]
        o_blocks.append(oi)
    o = jnp.concatenate(o_blocks, axis=0) if nb > 1 else o_blocks[0]
    g_last = gcum[c - 1:c]
    kd = k * jnp.exp2(g_last - gcum)
    if c < C_HEAD_DIM:
        pad = jnp.zeros((C_HEAD_DIM - c, C_HEAD_DIM), F32)
        kd = jnp.concatenate([kd, pad], axis=0)
        v = jnp.concatenate([v, pad], axis=0)
    st_new = st * jnp.exp2(g_last) + jnp.dot(v.T, kd, preferred_element_type=F32)
    return o, st_new


def _hgrn_body(*refs, chunk, n_chunks, n_r, hb, with_state):
    if with_state:
        q_ref, k_ref, lf_ref, v_ref, g_ref, gg_ref, s0_ref, o_ref, so_ref, st_ref, gs_ref = refs
    else:
        q_ref, k_ref, lf_ref, v_ref, g_ref, gg_ref, o_ref, so_ref, st_ref, gs_ref = refs
    r = pl.program_id(2)
    hd = C_HEAD_DIM

    @pl.when(r == 0)
    def _():
        for h in range(hb):
            if with_state:
                st_ref[h] = s0_ref[h].astype(F32).T
            else:
                st_ref[h] = jnp.zeros((hd, hd), F32)

    def step(ci, carry):
        base = pl.multiple_of(ci * chunk, chunk)
        sl = pl.ds(base, chunk)
        for h in range(hb):
            cs = slice(h * hd, (h + 1) * hd)
            o, st_new = _hgrn_chunk(q_ref[sl, cs], k_ref[sl, cs], lf_ref[sl, cs], v_ref[sl, cs],
                                    st_ref[h], gs_ref.at[h])
            st_ref[h] = st_new
            gate = g_ref[sl, cs]
            o = _rmsnorm_f32(o, gg_ref[...]) * (gate * _sigmoid(gate))
            o_ref[sl, cs] = o.astype(o_ref.dtype)
        return carry

    lax.fori_loop(0, n_chunks, step, 0)

    @pl.when(r == n_r - 1)
    def _():
        for h in range(hb):
            so_ref[h] = st_ref[h].T.astype(so_ref.dtype)


def hgrn_recurrence(q, k, lf, v, g, g_gain, state0, *, batch, seq, heads, rows, chunk, hb, out_dtype):
    n_r = seq // rows
    hd = C_HEAD_DIM
    with_state = state0 is not None
    row_spec = pl.BlockSpec((rows, hb * hd), lambda b, h, r: (b * n_r + r, h))
    state_spec = pl.BlockSpec((None, hb, hd, hd), lambda b, h, r: (b, h, 0, 0))
    in_specs = [row_spec] * 5 + [pl.BlockSpec((1, hd), lambda b, h, r: (0, 0))]
    args = [q, k, lf, v, g, g_gain.reshape(1, hd)]
    if with_state:
        in_specs.append(state_spec)
        args.append(state0)
    return pl.pallas_call(
        functools.partial(_hgrn_body, chunk=chunk, n_chunks=rows // chunk, n_r=n_r, hb=hb,
                          with_state=with_state),
        grid=(batch, heads // hb, n_r),
        in_specs=in_specs,
        out_specs=[row_spec, state_spec],
        out_shape=[jax.ShapeDtypeStruct((batch * seq, heads * hd), out_dtype),
                   jax.ShapeDtypeStruct((batch, heads, hd, hd), F32)],
        scratch_shapes=[pltpu.VMEM((hb, hd, hd), F32), pltpu.VMEM((hb, 3, chunk, hd), F32)],
        compiler_params=_cparams(3),
        name="hgrn_recurrence",
    )(*args)


def _tiles(m):
    big = m >= 1024
    return dict(
        norm_tm=512 if big else m,
        proj_tm=1024 if big else m,
        b_proj_tm=2048 if big else m,
        proj_tn=256,
        a_proj_tn=1024,
        out_tm=512 if big else m,
        ffn_tm=1024 if big else m,
        ffn_th=512,
        spatial_tm=512 if big else m,
        attn_tq=512,
        hgrn_rows=1024,
        attn_heads=8,
        attn_heads_fixed_shift=16,
        hgrn_heads=8,
    )


def _mixer_a_core(uv, v_gain, w_s, b_s, *, chunk_len, n_seq, tiles, with_vn):
    causal = jnp.tril(jnp.ones((A_CHUNK, A_CHUNK), bool))
    w_masked = jnp.where(causal[None], w_s, 0.0)
    if chunk_len == A_CHUNK:
        wm, bs, chunk = w_masked, b_s, A_CHUNK
    else:
        eye = jnp.eye(n_seq, dtype=w_s.dtype)
        small = w_masked[:, :chunk_len, :chunk_len]
        wm = jnp.einsum("ab,gts->gatbs", eye, small).reshape(
            A_GROUPS, n_seq * chunk_len, n_seq * chunk_len)
        bs = jnp.tile(b_s[:, :chunk_len], (1, n_seq))
        chunk = n_seq * chunk_len
    p, vn = spatial_mix(uv, wm.astype(BF16), bs[:, :, None], v_gain, chunk=chunk,
                        tm=max(tiles["spatial_tm"], chunk) if chunk_len == A_CHUNK else chunk,
                        with_vn=with_vn)
    return p, vn


def _mixer_a_proj(xn, xns, w_in, layer, *, tiles):
    width2 = w_in.shape[2]
    (uv,), (uv_s,) = seg_matmul(xn, xns, w_in, layer, (0,), width2, _gelu_epilogue, (BF16,),
                                tm=tiles["proj_tm"], tn=tiles["a_proj_tn"], name="a_in_proj")
    return uv, uv_s


def _mixer_b_proj(xn, xns, w_in, layer, q_gain, k_gain, *, heads, tiles):
    width = heads * 2 * B_HEAD_DIM
    tn = tiles["proj_tn"]
    lane_group = np.arange(tn) // B_HEAD_DIM
    group_ones = jnp.asarray(lane_group[:, None] == lane_group[None, :], BF16)
    reps = width // B_HEAD_DIM
    gq = jnp.tile(q_gain.astype(F32), reps).reshape(1, width)
    gk = jnp.tile(k_gain.astype(F32), reps).reshape(1, width)
    return seg_matmul(xn, xns, w_in, layer, (0, width, 2 * width), width, _headnorm_epilogue,
                      (F32, F32, F32), vecs=(gq, gk), consts=(group_ones,),
                      tm=tiles["b_proj_tm"], tn=tn, name="b_in_proj")


def _mixer_c_proj(xn, xns, w_in, layer, lower_bound, *, tiles):
    width = w_in.shape[2] // 4
    return seg_matmul(xn, xns, w_in, layer, (0, width, 2 * width, 3 * width), width, _hgrn_gate_epilogue,
                      (F32,) * 5, vecs=(lower_bound.reshape(1, width),),
                      tm=tiles["proj_tm"], tn=tiles["proj_tn"], name="c_in_proj")


def kernel(x_prompt, x_sample, cache_k, cache_v, page_table, state_hgrn, norm_mix, norm_ffn, ffn_w_gu, ffn_w_down, a_w_in, a_v_norm, a_w_s, a_b_s, a_w_out, b_w_in, b_q_norm, b_k_norm, b_lambda_q1, b_lambda_k1, b_lambda_q2, b_lambda_k2, b_subln, b_w_out, c_w_in, c_g_norm, c_lower_bounds, c_w_out):
    batch, seq, d_model = x_prompt.shape
    dec_b, dec_seq, _ = x_sample.shape
    depth = norm_mix.shape[0]
    b_heads = d_model // (2 * B_HEAD_DIM)
    c_heads = d_model // C_HEAD_DIM
    mp, ms = batch * seq, dec_b * dec_seq
    tp, ts = _tiles(mp), _tiles(ms)

    probs = jax.nn.softmax(c_lower_bounds.astype(F32), axis=0)
    lower_bound = jnp.cumsum(probs, axis=0) - probs[0]

    h_p = x_prompt.reshape(mp, d_model)
    h_s = x_sample.reshape(ms, d_model)
    xn_p = norm_rows(h_p, norm_mix[0], tm=tp["norm_tm"])
    xn_s = norm_rows(h_s, norm_mix[0], tm=ts["norm_tm"])

    n_phys = cache_k.shape[1]
    cache_k2 = cache_k.reshape(cache_k.shape[0], n_phys, PAGE_SIZE * b_heads, 2 * B_HEAD_DIM)
    cache_v2 = cache_v.reshape(cache_v.shape[0], n_phys, PAGE_SIZE * b_heads, B_V_DIM)

    k_p_rows, v_p_rows, k_s_rows, v_s_rows = [], [], [], []
    hgrn_p, hgrn_s, chunk_v_s = [], [], []
    for i in range(depth):
        kind, j = i % 3, i // 3
        if kind == 0:
            uv_p, uv_s = _mixer_a_proj(xn_p, xn_s, a_w_in, j, tiles=tp)
            y_p, _ = _mixer_a_core(uv_p, a_v_norm[j], a_w_s[j], a_b_s[j],
                                   chunk_len=A_CHUNK, n_seq=batch, tiles=tp, with_vn=False)
            y_s, vn_s = _mixer_a_core(uv_s, a_v_norm[j], a_w_s[j], a_b_s[j],
                                      chunk_len=dec_seq, n_seq=dec_b, tiles=ts, with_vn=True)
            chunk_v_s.append(vn_s.reshape(dec_b, dec_seq, -1))
            w_out = a_w_out
        elif kind == 1:
            lam_init = 0.8 - 0.6 * math.exp(-0.3 * i)
            lam = (jnp.exp(jnp.sum(b_lambda_q1[j].astype(F32) * b_lambda_k1[j].astype(F32)))
                   - jnp.exp(jnp.sum(b_lambda_q2[j].astype(F32) * b_lambda_k2[j].astype(F32)))
                   + lam_init)
            (q_p, k_p, v_p), (q_s, k_s, v_s) = _mixer_b_proj(
                xn_p, xn_s, b_w_in, j, b_q_norm[j], b_k_norm[j], heads=b_heads, tiles=tp)
            score_bound = (B_HEAD_DIM ** 0.5) * jnp.max(jnp.abs(b_q_norm[j].astype(F32))) * jnp.max(
                jnp.abs(b_k_norm[j].astype(F32)))
            attn = functools.partial(diff_attn_prompt, batch=batch, seq=seq, heads=b_heads,
                                     tq=tp["attn_tq"])
            y_p = lax.cond(
                score_bound <= ATTN_FIXED_SHIFT_MAX_BOUND,
                lambda *a: attn(*a, lam_init, hb=min(b_heads, tp["attn_heads_fixed_shift"]),
                                fixed_shift=True),
                lambda *a: attn(*a, lam_init, hb=min(b_heads, tp["attn_heads"]), fixed_shift=False),
                q_p, k_p, v_p, lam, score_bound, b_subln[j])
            k_p_rows.append(k_p.reshape(batch, seq, b_heads, 2 * B_HEAD_DIM))
            v_p_rows.append(v_p.reshape(batch, seq, b_heads, B_V_DIM))

            q5 = (q_s * (B_HEAD_DIM ** -0.5 * LOG2_E)).reshape(dec_b, dec_seq, b_heads, 2, B_HEAD_DIM)
            qt = jnp.einsum("bthcd,ce->bhcted", q5, jnp.eye(2, dtype=F32))
            qt = qt.reshape(dec_b, b_heads // SUBLANES, SUBLANES * 2 * dec_seq, 2 * B_HEAD_DIM)
            qt = qt.transpose(0, 1, 3, 2)
            k_new = k_s.reshape(dec_b, dec_seq * b_heads, 2 * B_HEAD_DIM)
            v_new = v_s.reshape(dec_b, dec_seq * b_heads, B_V_DIM)
            o_s = diff_attn_decode(qt, cache_k2, cache_v2, page_table, k_new, v_new, lam,
                                   b_subln[j], lam_init, layer=j, heads=b_heads, n_q=dec_seq,
                                   pages=DECODE_PAGES_PER_STEP)
            y_s = o_s.transpose(0, 2, 1, 3).reshape(ms, d_model)
            k_s_rows.append(k_s.reshape(dec_b, dec_seq, b_heads, 2 * B_HEAD_DIM))
            v_s_rows.append(v_s.reshape(dec_b, dec_seq, b_heads, B_V_DIM))
            w_out = b_w_out
        else:
            qkv_p, qkv_s = _mixer_c_proj(xn_p, xn_s, c_w_in, j, lower_bound[i], tiles=tp)
            y_p, st_p = hgrn_recurrence(*qkv_p, c_g_norm[j], None, batch=batch, seq=seq,
                                        heads=c_heads, rows=tp["hgrn_rows"], chunk=C_HEAD_DIM,
                                        hb=tp["hgrn_heads"], out_dtype=BF16)
            y_s, st_s = hgrn_recurrence(*qkv_s, c_g_norm[j], state_hgrn[j], batch=dec_b,
                                        seq=dec_seq, heads=c_heads, rows=dec_seq, chunk=dec_seq,
                                        hb=ts["hgrn_heads"], out_dtype=F32)
            hgrn_p.append(st_p)
            hgrn_s.append(st_s)
            w_out = c_w_out
        h_p, xf_p, h_s, xf_s = out_proj(y_p, y_s, w_out, j, h_p, h_s, norm_ffn[i], tm=tp["out_tm"])
        gain_next = norm_mix[i + 1] if i + 1 < depth else None
        h_p, h_s, xn_p, xn_s = ffn(xf_p, h_p, xf_s, h_s, ffn_w_gu, ffn_w_down, i, gain_next,
                                   tm=tp["ffn_tm"], th=math.gcd(ffn_w_down.shape[1], tp["ffn_th"]))
    return (h_p.reshape(batch, seq, d_model), h_s.reshape(dec_b, dec_seq, d_model),
            jnp.stack(k_p_rows), jnp.stack(v_p_rows), jnp.stack(k_s_rows), jnp.stack(v_s_rows),
            jnp.stack(hgrn_p), jnp.stack(hgrn_s), jnp.stack(chunk_v_s))
```
